```python
import math
import jax, jax.numpy as jnp
from jax import lax
import numpy as np

D_MODEL = 1024
BATCH = 16
SEQ = 2048
DEPTH = 4

CHUNK = 64
D_MIX = D_MODEL
N_GROUPS = 4
GROUP_W = D_MIX // N_GROUPS
CONV_W = 3
GMLP_HEADS = 4
GMLP_HD = GROUP_W // GMLP_HEADS
GMLP_BLOCK = 128
DIFF_HEADS = 4
DIFF_VD = GROUP_W // DIFF_HEADS
DIFF_QD = DIFF_VD // 2
DSA_HEADS = 4
DSA_HD = GROUP_W // DSA_HEADS
IDX_HEADS = 4
IDX_HD = 32
DSA_TOPK_MAX = 256
Q_BLOCK = 128
EPS = 1e-6

COL_SIZES = (
    GROUP_W, GROUP_W, GROUP_W, GROUP_W,
    GROUP_W, GROUP_W, GROUP_W,
    GROUP_W, GROUP_W, GROUP_W, GROUP_W,
    GROUP_W, DSA_HD, DSA_HD, GROUP_W,
    IDX_HEADS * IDX_HD, IDX_HD, IDX_HEADS,
)
N_IN = sum(COL_SIZES)

kernel_name = "hybrid_chunk_causal_parallel_groups"


def rms_norm(x, g):
    xf = x.astype(jnp.float32)
    y = xf * lax.rsqrt(jnp.mean(xf * xf, axis=-1, keepdims=True) + EPS)
    return (y * g.astype(jnp.float32)).astype(x.dtype)


def split_cols(y, sizes):
    out, off = [], 0
    for s in sizes:
        out.append(y[..., off:off + s])
        off += s
    return out


def alibi_slopes():
    n = DIFF_HEADS + DSA_HEADS
    s = (2.0 ** (-8.0 * np.arange(1, n + 1) / n)).astype(np.float32)
    return jnp.asarray(s[0::2]), jnp.asarray(s[1::2])


def chunk_mask(qpos, kpos):
    return (qpos[:, None] // CHUNK) >= (kpos[None, :] // CHUNK)


def short_conv_mixer(h, b_gate, c_gate, w, bias):
    u = c_gate * h
    T = u.shape[1]
    up = jnp.pad(u, ((0, 0), (CONV_W - 1, 0), (0, 0)))
    y = up[:, 0:T] * w[0]
    for j in range(1, CONV_W):
        y = y + up[:, j:j + T] * w[j]
    return b_gate * (y + bias)


def gmlp_mixer(u, v, g, ws, bs):
    Bn, T, _ = u.shape
    u = jax.nn.gelu(u, approximate=False)
    v = jax.nn.gelu(v, approximate=False)
    vh = rms_norm(v.reshape(Bn, T, GMLP_HEADS, GMLP_HD), g.reshape(GMLP_HEADS, GMLP_HD))
    vh = vh.reshape(Bn, T // GMLP_BLOCK, GMLP_BLOCK, GMLP_HEADS, GMLP_HD)
    pos = jnp.arange(GMLP_BLOCK)
    wm = jnp.where(chunk_mask(pos, pos)[None], ws, jnp.zeros_like(ws))
    s = jnp.einsum('hts,bnshd->bnthd', wm, vh) + bs.T[:, :, None]
    return u * s.reshape(Bn, T, GROUP_W)


def diff_attention(q, k, v, q_g, k_g, lam_p, sub_g, slopes, layer_idx):
    Bn, T, _ = q.shape
    q = rms_norm(q.reshape(Bn, T, DIFF_HEADS, 2, DIFF_QD), q_g)
    k = rms_norm(k.reshape(Bn, T, DIFF_HEADS, 2, DIFF_QD), k_g)
    v = v.reshape(Bn, T, DIFF_HEADS, DIFF_VD)
    lam_init = 0.8 - 0.6 * math.exp(-0.3 * layer_idx)
    lp = lam_p.astype(jnp.float32)
    lam = jnp.exp(jnp.sum(lp[0] * lp[1])) - jnp.exp(jnp.sum(lp[2] * lp[3])) + lam_init
    scale = DIFF_QD ** -0.5
    kpos = jnp.arange(T)
    nb = T // Q_BLOCK
    qb = q.reshape(Bn, nb, Q_BLOCK, DIFF_HEADS, 2, DIFF_QD).swapaxes(0, 1)

    def block(args):
        qi, bi = args
        qpos = bi * Q_BLOCK + jnp.arange(Q_BLOCK)
        s = jnp.einsum('bthjd,bshjd->bhjts', qi, k).astype(jnp.float32) * scale
        dist = jnp.abs(qpos[:, None] - kpos[None, :]).astype(jnp.float32)
        s = s - (slopes[:, None, None] * dist)[:, None]
        s = jnp.where(chunk_mask(qpos, kpos), s, -jnp.inf)
        p = jax.nn.softmax(s, axis=-1)
        pd = (p[:, :, 0] - lam * p[:, :, 1]).astype(v.dtype)
        return jnp.einsum('bhts,bshd->bthd', pd, v)

    o = lax.map(block, (qb, jnp.arange(nb)))
    o = o.swapaxes(0, 1).reshape(Bn, T, DIFF_HEADS, DIFF_VD)
    o = rms_norm(o, sub_g.reshape(DIFF_HEADS, DIFF_VD)) * (1.0 - lam_init)
    return o.reshape(Bn, T, GROUP_W)


def dsa_attention(q, k, v, iq, ik, iw, q_g, k_g, slopes):
    Bn, T, _ = q.shape
    q = rms_norm(q.reshape(Bn, T, DSA_HEADS, DSA_HD), q_g)
    k = rms_norm(k, k_g)
    iq = iq.reshape(Bn, T, IDX_HEADS, IDX_HD)
    topk = min(DSA_TOPK_MAX, T // 4)
    scale = DSA_HD ** -0.5
    kpos = jnp.arange(T)
    nb = T // Q_BLOCK
    qb = q.reshape(Bn, nb, Q_BLOCK, DSA_HEADS, DSA_HD).swapaxes(0, 1)
    iqb = iq.reshape(Bn, nb, Q_BLOCK, IDX_HEADS, IDX_HD).swapaxes(0, 1)
    iwb = iw.reshape(Bn, nb, Q_BLOCK, IDX_HEADS).swapaxes(0, 1)

    def block(args):
        qi, iqi, iwi, bi = args
        qpos = bi * Q_BLOCK + jnp.arange(Q_BLOCK)
        logits = jnp.einsum('bthd,bsd->bths', iqi, ik).astype(jnp.float32) * (IDX_HD ** -0.5)
        score = jnp.einsum('bths,bth->bts', jax.nn.relu(logits), iwi.astype(jnp.float32)) * (IDX_HEADS ** -0.5)
        score = jnp.where(chunk_mask(qpos, kpos)[None], score, -jnp.inf)
        _, idx = lax.top_k(score, topk)
        valid = (idx // CHUNK) <= (qpos[None, :, None] // CHUNK)
        kg = jax.vmap(lambda kb, ib: kb[ib])(k, idx)
        vg = jax.vmap(lambda vb, ib: vb[ib])(v, idx)
        s = jnp.einsum('bthd,btkd->bhtk', qi, kg).astype(jnp.float32) * scale
        dist = jnp.abs(qpos[None, :, None] - idx).astype(jnp.float32)
        s = s - slopes[None, :, None, None] * dist[:, None]
        s = jnp.where(valid[:, None], s, -jnp.inf)
        p = jax.nn.softmax(s, axis=-1).astype(v.dtype)
        return jnp.einsum('bhtk,btkd->bthd', p, vg)

    o = lax.map(block, (qb, iqb, iwb, jnp.arange(nb)))
    return o.swapaxes(0, 1).reshape(Bn, T, GROUP_W)


def setup_inputs(seed: int = 0) -> dict:
    key = jax.random.key(seed)
    ks = jax.random.split(key, 16)
    f32 = jnp.float32
    nrm = lambda k, shp: jax.random.normal(k, shp, f32)
    return {
        "x": nrm(ks[0], (BATCH, SEQ, D_MODEL)),
        "norm_g": 1.0 + 0.02 * nrm(ks[1], (DEPTH, D_MODEL)),
        "w_in": nrm(ks[2], (DEPTH, D_MODEL, N_IN)) * (D_MODEL ** -0.5),
        "conv_w": nrm(ks[3], (DEPTH, CONV_W, GROUP_W)) * (CONV_W ** -0.5),
        "conv_b": 0.02 * nrm(ks[4], (DEPTH, GROUP_W)),
        "gmlp_g": 1.0 + 0.02 * nrm(ks[5], (DEPTH, GROUP_W)),
        "gmlp_ws": nrm(ks[6], (DEPTH, GMLP_HEADS, GMLP_BLOCK, GMLP_BLOCK)) * (0.5 * GMLP_BLOCK ** -0.5),
        "gmlp_b": 1.0 + 0.1 * nrm(ks[7], (DEPTH, GMLP_HEADS, GMLP_BLOCK)),
        "diff_qg": 1.0 + 0.02 * nrm(ks[8], (DEPTH, DIFF_QD)),
        "diff_kg": 1.0 + 0.02 * nrm(ks[9], (DEPTH, DIFF_QD)),
        "diff_lam": 0.1 * nrm(ks[10], (DEPTH, 4, DIFF_QD)),
        "diff_subg": 1.0 + 0.02 * nrm(ks[11], (DEPTH, GROUP_W)),
        "dsa_qg": 1.0 + 0.02 * nrm(ks[12], (DEPTH, DSA_HD)),
        "dsa_kg": 1.0 + 0.02 * nrm(ks[13], (DEPTH, DSA_HD)),
        "w_out": nrm(ks[14], (DEPTH, D_MIX, D_MODEL)) * (0.5 * D_MIX ** -0.5),
    }


def reference(x, norm_g, w_in, conv_w, conv_b, gmlp_g, gmlp_ws, gmlp_b, diff_qg, diff_kg,
              diff_lam, diff_subg, dsa_qg, dsa_kg, w_out):
    slopes_c, slopes_d = alibi_slopes()
    for l in range(DEPTH):
        xn = rms_norm(x, norm_g[l])
        y = jnp.einsum('btd,dn->btn', xn, w_in[l])
        (a_h, a_b, a_c, a_z,
         g_u, g_v, g_z,
         c_q, c_k, c_v, c_z,
         d_q, d_k, d_v, d_z,
         i_q, i_k, i_w) = split_cols(y, COL_SIZES)
        ya = short_conv_mixer(a_h, a_b, a_c, conv_w[l], conv_b[l]) * jax.nn.silu(a_z)
        yb = gmlp_mixer(g_u, g_v, gmlp_g[l], gmlp_ws[l], gmlp_b[l]) * jax.nn.silu(g_z)
        yc = diff_attention(c_q, c_k, c_v, diff_qg[l], diff_kg[l], diff_lam[l], diff_subg[l],
                            slopes_c, l) * jax.nn.silu(c_z)
        yd = dsa_attention(d_q, d_k, d_v, i_q, i_k, i_w, dsa_qg[l], dsa_kg[l], slopes_d) * jax.nn.silu(d_z)
        mix = jnp.concatenate([ya, yb, yc, yd], axis=-1)
        x = x + jnp.einsum('btm,md->btd', mix, w_out[l])
    return x
```

```python
import functools
import math

import numpy as np
import jax
import jax.numpy as jnp
from jax import lax
from jax.experimental import pallas as pl
from jax.experimental.pallas import tpu as pltpu

F32 = jnp.float32
BF16 = jnp.bfloat16
I32 = jnp.int32

GROUP_W = 256
CHUNK = 64
CONV_W = 3
GMLP_BLOCK = 128
DIFF_QD = 32
DSA_HD = 64
IDX_HD = 32
IDX_HEADS = 4
DSA_TOPK_MAX = 256
EPS = 1e-6
LOG2E = math.log2(math.e)
INT_MIN = -2 ** 31

TQ = 128
KC = 512
TM = 512
HALO = 16

_SLOPES = 2.0 ** (-8.0 * np.arange(1, 9) / 8.0)
SLOPES_C = [float(s) for s in _SLOPES[0::2]]
SLOPES_D = [float(s) for s in _SLOPES[1::2]]

VMEM_LIMIT = 56 * 1024 * 1024


def _block_diag_mean(width, seg):
    idx = np.arange(width) // seg
    return jnp.asarray((idx[:, None] == idx[None, :]).astype(np.float32) / seg, dtype=BF16)


def _seg_mean(x2, p):
    hi = x2.astype(BF16)
    lo = (x2 - hi.astype(F32)).astype(BF16)
    return (jnp.dot(hi, p, preferred_element_type=F32)
            + jnp.dot(lo, p, preferred_element_type=F32))


def _silu(z):
    return z * jax.nn.sigmoid(z)


def _gelu(x):
    return 0.5 * x * (1.0 + lax.erf(x * (2.0 ** -0.5)))


def _select_scalar(h, vals):
    out = jnp.float32(vals[-1])
    for j in range(len(vals) - 2, -1, -1):
        out = jnp.where(h == j, jnp.float32(vals[j]), out)
    return out


def _inproj_kernel(x_ref, g_ref, wa_ref, wb_ref, wc_ref, wd_ref, ya_ref, yb_ref, yc_ref, yd_ref):
    x = x_ref[...]
    ms = jnp.mean(x * x, axis=-1, keepdims=True)
    xn = (x * lax.rsqrt(ms + EPS) * g_ref[...]).astype(BF16)
    for w_ref, y_ref in ((wa_ref, ya_ref), (wb_ref, yb_ref), (wc_ref, yc_ref), (wd_ref, yd_ref)):
        y_ref[...] = jnp.dot(xn, w_ref[...], preferred_element_type=F32).astype(y_ref.dtype)


def _inproj(xf, g, wa, wb, wc, wd):
    m, d = xf.shape
    ws = (wa, wb, wc, wd)
    return pl.pallas_call(
        _inproj_kernel,
        grid=(m // TM,),
        in_specs=[pl.BlockSpec((TM, d), lambda i: (i, 0)),
                  pl.BlockSpec((1, d), lambda i: (0, 0))]
                 + [pl.BlockSpec(w.shape, lambda i: (0, 0)) for w in ws],
        out_specs=[pl.BlockSpec((TM, w.shape[1]), lambda i: (i, 0)) for w in ws],
        out_shape=[jax.ShapeDtypeStruct((m, w.shape[1]), BF16) for w in ws],
        compiler_params=pltpu.CompilerParams(dimension_semantics=("arbitrary",),
                                             vmem_limit_bytes=VMEM_LIMIT),
        name="inproj",
    )(xf, g, wa, wb, wc, wd)


def _outproj_kernel(x_ref, mab_ref, mc_ref, md_ref, wo_ref, o_ref):
    acc = x_ref[...]
    acc = acc + jnp.dot(mab_ref[...], wo_ref[0:2 * GROUP_W, :], preferred_element_type=F32)
    acc = acc + jnp.dot(mc_ref[...], wo_ref[2 * GROUP_W:3 * GROUP_W, :], preferred_element_type=F32)
    acc = acc + jnp.dot(md_ref[...], wo_ref[3 * GROUP_W:4 * GROUP_W, :], preferred_element_type=F32)
    o_ref[...] = acc


def _outproj(xf, mab, mc, md, wo):
    m, d = xf.shape
    return pl.pallas_call(
        _outproj_kernel,
        grid=(m // TM,),
        in_specs=[pl.BlockSpec((TM, d), lambda i: (i, 0)),
                  pl.BlockSpec((TM, 2 * GROUP_W), lambda i: (i, 0)),
                  pl.BlockSpec((TM, GROUP_W), lambda i: (i, 0)),
                  pl.BlockSpec((TM, GROUP_W), lambda i: (i, 0)),
                  pl.BlockSpec(wo.shape, lambda i: (0, 0))],
        out_specs=pl.BlockSpec((TM, d), lambda i: (i, 0)),
        out_shape=jax.ShapeDtypeStruct((m, d), F32),
        compiler_params=pltpu.CompilerParams(dimension_semantics=("arbitrary",),
                                             vmem_limit_bytes=VMEM_LIMIT),
        name="outproj",
    )(xf, mab, mc, md, wo)


def _mixab_kernel(ya_ref, halo_ref, yb_ref, cw_ref, cb_ref, gg_ref, ws_ref, bfull_ref, p64_ref,
                  o_ref, u_scr):
    i = pl.program_id(1)
    W = GROUP_W
    h = ya_ref[:, 0:W].astype(F32)
    bg = ya_ref[:, W:2 * W].astype(F32)
    cg = ya_ref[:, 2 * W:3 * W].astype(F32)
    za = ya_ref[:, 3 * W:4 * W].astype(F32)
    uh = halo_ref[:, 2 * W:3 * W].astype(F32) * halo_ref[:, 0:W].astype(F32)
    u_scr[0:HALO, :] = jnp.where(i > 0, uh, 0.0)
    u_scr[HALO:HALO + TQ, :] = cg * h
    y = u_scr[HALO - 2:HALO - 2 + TQ, :] * cw_ref[0:1, :]
    y = y + u_scr[HALO - 1:HALO - 1 + TQ, :] * cw_ref[1:2, :]
    y = y + u_scr[HALO:HALO + TQ, :] * cw_ref[2:3, :]
    out_a = bg * (y + cb_ref[...]) * _silu(za)
    o_ref[:, 0:W] = out_a.astype(o_ref.dtype)

    u = _gelu(yb_ref[:, 0:W].astype(F32))
    v = _gelu(yb_ref[:, W:2 * W].astype(F32))
    zb = yb_ref[:, 2 * W:3 * W].astype(F32)
    vn = v * lax.rsqrt(_seg_mean(v * v, p64_ref[...]) + EPS) * gg_ref[...]
    t_idx = lax.broadcasted_iota(I32, (GMLP_BLOCK, GMLP_BLOCK), 0)
    s_idx = lax.broadcasted_iota(I32, (GMLP_BLOCK, GMLP_BLOCK), 1)
    causal = (t_idx >> 6) >= (s_idx >> 6)
    lane = lax.broadcasted_iota(I32, (1, W), 1)
    s = bfull_ref[...]
    for hd in range(4):
        wm = jnp.where(causal, ws_ref[hd], 0.0).astype(BF16)
        vh = jnp.where((lane >> 6) == hd, vn, 0.0).astype(BF16)
        s = s + jnp.dot(wm, vh, preferred_element_type=F32)
    out_b = u * s * _silu(zb)
    o_ref[:, W:2 * W] = out_b.astype(o_ref.dtype)


def _mixab(ya, yb, cw, cb, gg, ws, bfull, p64):
    b, t, _ = ya.shape
    nq = t // TQ
    return pl.pallas_call(
        _mixab_kernel,
        grid=(b, nq),
        in_specs=[pl.BlockSpec((None, TQ, 4 * GROUP_W), lambda bi, i: (bi, i, 0)),
                  pl.BlockSpec((None, HALO, 4 * GROUP_W),
                               lambda bi, i: (bi, jnp.maximum(i * (TQ // HALO) - 1, 0), 0)),
                  pl.BlockSpec((None, TQ, 3 * GROUP_W), lambda bi, i: (bi, i, 0)),
                  pl.BlockSpec(cw.shape, lambda bi, i: (0, 0)),
                  pl.BlockSpec(cb.shape, lambda bi, i: (0, 0)),
                  pl.BlockSpec(gg.shape, lambda bi, i: (0, 0)),
                  pl.BlockSpec(ws.shape, lambda bi, i: (0, 0, 0)),
                  pl.BlockSpec(bfull.shape, lambda bi, i: (0, 0)),
                  pl.BlockSpec(p64.shape, lambda bi, i: (0, 0))],
        out_specs=pl.BlockSpec((None, TQ, 2 * GROUP_W), lambda bi, i: (bi, i, 0)),
        out_shape=jax.ShapeDtypeStruct((b, t, 2 * GROUP_W), BF16),
        scratch_shapes=[pltpu.VMEM((HALO + TQ, GROUP_W), F32)],
        compiler_params=pltpu.CompilerParams(dimension_semantics=("arbitrary", "arbitrary"),
                                             vmem_limit_bytes=VMEM_LIMIT),
        name="mixab",
    )(ya, ya, yb, cw, cb, gg, ws, bfull, p64)


def _diff_kernel(q_ref, k_ref, v_ref, z_ref, qg_ref, kg_ref, lam_ref, subg_ref, p32_ref, p64_ref,
                 o_ref, kn_scr, dm_scr, s_scr, o_scr, *, lam_init, seq):
    i = pl.program_id(1)
    q0 = i * TQ
    nchunk = q0 // KC + 1
    p32 = p32_ref[...]

    @pl.when(i == 0)
    def _():
        for c in range(seq // KC):
            kk = k_ref[c * KC:(c + 1) * KC, :].astype(F32)
            ms = _seg_mean(kk * kk, p32)
            kn_scr[c * KC:(c + 1) * KC, :] = (kk * lax.rsqrt(ms + EPS) * kg_ref[...]).astype(BF16)

    q = q_ref[...].astype(F32)
    qn = q * lax.rsqrt(_seg_mean(q * q, p32) + EPS) * (qg_ref[...] * (DIFF_QD ** -0.5 * LOG2E))

    lp = lam_ref[...]
    lam = (jnp.exp(jnp.sum(lp[0:1] * lp[1:2], axis=-1, keepdims=True))
           - jnp.exp(jnp.sum(lp[2:3] * lp[3:4], axis=-1, keepdims=True)) + lam_init)

    row = q0 + lax.broadcasted_iota(I32, (TQ, KC), 0)

    def dm_body(c, carry):
        col = c * KC + lax.broadcasted_iota(I32, (TQ, KC), 1)
        dist = jnp.abs(row - col).astype(F32)
        dm_scr[c] = jnp.where((col >> 6) <= (row >> 6), dist, jnp.inf)
        return carry

    lax.fori_loop(0, nchunk, dm_body, 0)

    o_scr[...] = jnp.zeros(o_scr.shape, F32)
    lane = lax.broadcasted_iota(I32, (1, GROUP_W), 1)

    def head_body(h, carry):
        slope = _select_scalar(h, [s * LOG2E for s in SLOPES_C])
        sums = []
        for j in range(2):
            qm = jnp.where((lane >> 5) == 2 * h + j, qn, 0.0).astype(BF16)

            def s_body(c, m, qm=qm, j=j):
                kc = kn_scr[pl.ds(pl.multiple_of(c * KC, KC), KC), :]
                s = lax.dot_general(qm, kc, (((1,), (1,)), ((), ())), preferred_element_type=F32)
                s = s - slope * dm_scr[c]
                s_scr[j, c] = s
                return jnp.maximum(m, jnp.max(s, axis=1, keepdims=True))

            m = lax.fori_loop(0, nchunk, s_body, jnp.full((TQ, 1), -jnp.inf, F32))

            def e_body(c, l, m=m, j=j):
                p = jnp.exp2(s_scr[j, c] - m)
                s_scr[j, c] = p
                return l + jnp.sum(p, axis=1, keepdims=True)

            sums.append(lax.fori_loop(0, nchunk, e_body, jnp.zeros((TQ, 1), F32)))

        a = 1.0 / sums[0]
        b = lam / sums[1]

        def pv_body(c, acc):
            pd = (s_scr[0, c] * a - s_scr[1, c] * b).astype(BF16)
            vc = v_ref[pl.ds(pl.multiple_of(c * KC, KC), KC), :]
            return acc + jnp.dot(pd, vc, preferred_element_type=F32)

        acc = lax.fori_loop(0, nchunk, pv_body, jnp.zeros((TQ, GROUP_W), F32))
        o_scr[...] += jnp.where((lane >> 6) == h, acc, 0.0)
        return carry

    lax.fori_loop(0, 4, head_body, 0)

    o = o_scr[...]
    ms = _seg_mean(o * o, p64_ref[...])
    o = o * lax.rsqrt(ms + EPS) * (subg_ref[...] * (1.0 - lam_init))
    z = z_ref[...].astype(F32)
    o_ref[...] = (o * _silu(z)).astype(o_ref.dtype)


def _diff(yc, qg, kg, lam_p, subg, p32, p64, lam_init):
    b, t, _ = yc.shape
    nq = t // TQ
    nkc = t // KC
    W = GROUP_W
    kern = functools.partial(_diff_kernel, lam_init=lam_init, seq=t)
    small = lambda a: pl.BlockSpec(a.shape, lambda bi, i: (0,) * a.ndim)
    return pl.pallas_call(
        kern,
        grid=(b, nq),
        in_specs=[pl.BlockSpec((None, TQ, W), lambda bi, i: (bi, i, 0)),
                  pl.BlockSpec((None, t, W), lambda bi, i: (bi, 0, 1)),
                  pl.BlockSpec((None, t, W), lambda bi, i: (bi, 0, 2)),
                  pl.BlockSpec((None, TQ, W), lambda bi, i: (bi, i, 3)),
                  small(qg), small(kg), small(lam_p), small(subg), small(p32), small(p64)],
        out_specs=pl.BlockSpec((None, TQ, W), lambda bi, i: (bi, i, 0)),
        out_shape=jax.ShapeDtypeStruct((b, t, W), BF16),
        scratch_shapes=[pltpu.VMEM((t, W), BF16),
                        pltpu.VMEM((nkc, TQ, KC), F32),
                        pltpu.VMEM((2, nkc, TQ, KC), F32),
                        pltpu.VMEM((TQ, W), F32)],
        compiler_params=pltpu.CompilerParams(dimension_semantics=("arbitrary", "arbitrary"),
                                             vmem_limit_bytes=VMEM_LIMIT),
        name="diffattn",
    )(yc, yc, yc, yc, qg, kg, lam_p, subg, p32, p64)


def _dsa_kernel(q_ref, z_ref, iq_ref, ikwq_ref, kv_ref, ikw_ref, qg_ref, kg_ref, p64_ref, p64h_ref,
                o_ref, knv_scr, vt_scr, key_scr, tk_scr, dm_scr, a_scr, p_scr, *, seq, topk):
    i = pl.program_id(1)
    q0 = i * TQ
    nchunk = q0 // KC + 1
    SB = 256
    n_sb = KC // SB
    kf = jnp.float32(topk)
    big = jnp.int32(2 ** 30)

    @pl.when(i == 0)
    def _():
        p64h = p64h_ref[...]
        for c in range(seq // 128):
            blk = kv_ref[c * 128:(c + 1) * 128, :].astype(F32)
            ms = _seg_mean(blk * blk, p64h)
            knv_scr[c * 128:(c + 1) * 128, :] = (blk * lax.rsqrt(ms + EPS) * kg_ref[...]).astype(BF16)
            vt = blk.T
            cc, off = divmod(c * 128, KC)
            vt_scr[cc, :, off:off + 128] = vt[DSA_HD:2 * DSA_HD, :].astype(BF16)

    iq_t = iq_ref[...].astype(F32).T
    iw_t = ikwq_ref[...].astype(F32).T[IDX_HD:IDX_HD + 8, :]
    wq = iw_t * (IDX_HEADS ** -0.5 * IDX_HD ** -0.5)
    zpad_i = jnp.zeros((128 - IDX_HD, TQ), F32)
    rhs_idx = jnp.concatenate(
        [jnp.concatenate([iq_t[IDX_HD * h:IDX_HD * (h + 1), :], zpad_i], axis=0) for h in range(IDX_HEADS)],
        axis=1).astype(BF16)

    q = q_ref[...].astype(F32)
    qn = q * lax.rsqrt(_seg_mean(q * q, p64_ref[...]) + EPS) * (qg_ref[...] * (DSA_HD ** -0.5 * LOG2E))
    qn_t = qn.T
    zpad_q = jnp.zeros((128 - DSA_HD, TQ), F32)
    rhs_main = jnp.concatenate(
        [jnp.concatenate([qn_t[DSA_HD * h:DSA_HD * (h + 1), :], zpad_q], axis=0) for h in range(4)],
        axis=1).astype(BF16)

    qpos = q0 + lax.broadcasted_iota(I32, (SB, TQ), 1)

    def idx_body(c, carry):
        for sb in range(n_sb):
            r0 = pl.multiple_of(c * KC + sb * SB, SB)
            logit = jnp.dot(ikw_ref[pl.ds(r0, SB), :], rhs_idx, preferred_element_type=F32)
            sc = jnp.maximum(logit[:, 0:TQ], 0.0) * wq[0:1, :]
            for h in range(1, IDX_HEADS):
                sc = sc + jnp.maximum(logit[:, h * TQ:(h + 1) * TQ], 0.0) * wq[h:h + 1, :]
            bits = lax.bitcast_convert_type(sc, I32)
            key = bits ^ ((bits >> 31) & jnp.int32(0x7FFFFFFF))
            key = jnp.where(key == -1, 0, key)
            krow = r0 + lax.broadcasted_iota(I32, (SB, TQ), 0)
            allowed = (krow >> 6) <= (qpos >> 6)
            key_scr[pl.ds(r0, SB), :] = jnp.where(allowed, key, jnp.int32(INT_MIN))
        return carry

    lax.fori_loop(0, nchunk, idx_body, 0)

    def count(pred_fn):
        def body(c, acc):
            r0 = pl.multiple_of(c * KC, KC)
            return acc + jnp.sum(jnp.where(pred_fn(r0), 1.0, 0.0), axis=0, keepdims=True)
        return lax.fori_loop(0, nchunk, body, jnp.zeros((1, TQ), F32))

    def bit_body(it, u):
        cand_u = u | jnp.left_shift(jnp.int32(1), 31 - it)
        cand = cand_u ^ jnp.int32(INT_MIN)
        cnt = count(lambda r0: key_scr[pl.ds(r0, KC), :] >= cand)
        return jnp.where(cnt >= kf, cand_u, u)

    tau = lax.fori_loop(0, 32, bit_body, jnp.zeros((1, TQ), I32)) ^ jnp.int32(INT_MIN)

    def tie_body(c, acc):
        r0 = pl.multiple_of(c * KC, KC)
        key = key_scr[pl.ds(r0, KC), :]
        krow = r0 + lax.broadcasted_iota(I32, (KC, TQ), 0)
        tk_scr[pl.ds(r0, KC), :] = jnp.where(key == tau, krow, big)
        return acc + jnp.sum(jnp.where(key > tau, 1.0, 0.0), axis=0, keepdims=True)

    need = kf - lax.fori_loop(0, nchunk, tie_body, jnp.zeros((1, TQ), F32))
    nbits = int(seq).bit_length()

    def jbit_body(it, jj):
        cand = jj | jnp.left_shift(jnp.int32(1), nbits - 1 - it)
        cnt = count(lambda r0: tk_scr[pl.ds(r0, KC), :] < cand)
        return jnp.where(cnt <= need, cand, jj)

    jsel = lax.fori_loop(0, nbits, jbit_body, jnp.zeros((1, TQ), I32))
    jsel = jnp.where(tau == jnp.int32(INT_MIN), 0, jsel)

    def dm_body(c, carry):
        r0 = pl.multiple_of(c * KC, KC)
        krow = r0 + lax.broadcasted_iota(I32, (KC, TQ), 0)
        qp = q0 + lax.broadcasted_iota(I32, (KC, TQ), 1)
        dist = jnp.abs(qp - krow).astype(F32)
        inner = jnp.where(tk_scr[pl.ds(r0, KC), :] < jsel, dist, jnp.inf)
        dm_scr[pl.ds(r0, KC), :] = jnp.where(key_scr[pl.ds(r0, KC), :] > tau, dist, inner)
        return carry

    lax.fori_loop(0, nchunk, dm_body, 0)

    slopes = [s * LOG2E for s in SLOPES_D]

    def a_body(c, ms):
        ms = list(ms)
        for sb in range(n_sb):
            r0 = pl.multiple_of(c * KC + sb * SB, SB)
            att = jnp.dot(knv_scr[pl.ds(r0, SB), :], rhs_main, preferred_element_type=F32)
            dm = dm_scr[pl.ds(r0, SB), :]
            for h in range(4):
                a = att[:, h * TQ:(h + 1) * TQ] - slopes[h] * dm
                a_scr[h, pl.ds(r0, SB), :] = a
                ms[h] = jnp.maximum(ms[h], jnp.max(a, axis=0, keepdims=True))
        return tuple(ms)

    neg = jnp.full((1, TQ), -jnp.inf, F32)
    ms = lax.fori_loop(0, nchunk, a_body, (neg, neg, neg, neg))

    def e_body(c, ls):
        ls = list(ls)
        for sb in range(n_sb):
            r0 = pl.multiple_of(c * KC + sb * SB, SB)
            for h in range(4):
                p = jnp.exp2(a_scr[h, pl.ds(r0, SB), :] - ms[h])
                ls[h] = ls[h] + jnp.sum(p, axis=0, keepdims=True)
                p_scr[pl.ds(r0, SB), h * TQ:(h + 1) * TQ] = p.astype(BF16)
        return tuple(ls)

    zero = jnp.zeros((1, TQ), F32)
    ls = lax.fori_loop(0, nchunk, e_body, (zero, zero, zero, zero))

    def pv_body(c, acc):
        r0 = pl.multiple_of(c * KC, KC)
        return acc + jnp.dot(vt_scr[c], p_scr[pl.ds(r0, KC), :], preferred_element_type=F32)

    out_t = lax.fori_loop(0, nchunk, pv_body, jnp.zeros((DSA_HD, 4 * TQ), F32))
    o_t = jnp.concatenate([out_t[:, h * TQ:(h + 1) * TQ] * (1.0 / ls[h]) for h in range(4)], axis=0)
    o = o_t.T
    z = z_ref[...].astype(F32)
    o_ref[...] = (o * _silu(z)).astype(o_ref.dtype)


def _dsa(yd, qg, kg, p64, p64h):
    b, t, _ = yd.shape
    nq = t // TQ
    nkc = t // KC
    W = GROUP_W
    topk = min(DSA_TOPK_MAX, t // 4)
    kern = functools.partial(_dsa_kernel, seq=t, topk=topk)
    small = lambda a: pl.BlockSpec(a.shape, lambda bi, i: (0,) * a.ndim)
    return pl.pallas_call(
        kern,
        grid=(b, nq),
        in_specs=[pl.BlockSpec((None, TQ, W), lambda bi, i: (bi, i, 0)),
                  pl.BlockSpec((None, TQ, W), lambda bi, i: (bi, i, 1)),
                  pl.BlockSpec((None, TQ, 128), lambda bi, i: (bi, i, 5)),
                  pl.BlockSpec((None, TQ, 128), lambda bi, i: (bi, i, 6)),
                  pl.BlockSpec((None, t, 128), lambda bi, i: (bi, 0, 4)),
                  pl.BlockSpec((None, t, 128), lambda bi, i: (bi, 0, 6)),
                  small(qg), small(kg), small(p64), small(p64h)],
        out_specs=pl.BlockSpec((None, TQ, W), lambda bi, i: (bi, i, 0)),
        out_shape=jax.ShapeDtypeStruct((b, t, W), BF16),
        scratch_shapes=[pltpu.VMEM((t, 128), BF16),
                        pltpu.VMEM((nkc, DSA_HD, KC), BF16),
                        pltpu.VMEM((t, TQ), I32),
                        pltpu.VMEM((t, TQ), I32),
                        pltpu.VMEM((t, TQ), F32),
                        pltpu.VMEM((4, t, TQ), F32),
                        pltpu.VMEM((t, 4 * TQ), BF16)],
        compiler_params=pltpu.CompilerParams(dimension_semantics=("arbitrary", "arbitrary"),
                                             vmem_limit_bytes=VMEM_LIMIT),
        name="dsa",
    )(yd, yd, yd, yd, yd, yd, qg, kg, p64, p64h)


def _split_w_in(w):
    W = GROUP_W
    wa = w[:, 0:4 * W]
    wb = w[:, 4 * W:7 * W]
    wc = w[:, 7 * W:11 * W]
    d0 = 11 * W
    dq = w[:, d0:d0 + W]
    dkv = w[:, d0 + W:d0 + W + 128]
    dz = w[:, d0 + W + 128:d0 + 2 * W + 128]
    diq = w[:, d0 + 2 * W + 128:d0 + 2 * W + 256]
    dikw = w[:, d0 + 2 * W + 256:]
    dikw = jnp.pad(dikw, ((0, 0), (0, 128 - dikw.shape[1])))
    wd = jnp.concatenate([dq, dz, dkv, diq, dikw], axis=1)
    return [a.astype(BF16) for a in (wa, wb, wc, wd)]


def kernel(x, norm_g, w_in, conv_w, conv_b, gmlp_g, gmlp_ws, gmlp_b, diff_qg, diff_kg, diff_lam,
           diff_subg, dsa_qg, dsa_kg, w_out):
    b, t, d = x.shape
    depth = w_in.shape[0]
    p32 = _block_diag_mean(GROUP_W, 32)
    p64 = _block_diag_mean(GROUP_W, 64)
    p64h = _block_diag_mean(128, 64)
    xf = x.reshape(b * t, d)
    for l in range(depth):
        wa, wb, wc, wd = _split_w_in(w_in[l])
        ya, yb, yc, yd = _inproj(xf, norm_g[l].reshape(1, d), wa, wb, wc, wd)
        ya, yb, yc, yd = (a.reshape(b, t, a.shape[-1]) for a in (ya, yb, yc, yd))
        bfull = jnp.repeat(gmlp_b[l].T, GROUP_W // 4, axis=1)
        mab = _mixab(ya, yb, conv_w[l], conv_b[l].reshape(1, -1), gmlp_g[l].reshape(1, -1),
                     gmlp_ws[l], bfull, p64)
        lam_init = 0.8 - 0.6 * math.exp(-0.3 * l)
        mc = _diff(yc, jnp.tile(diff_qg[l], 8).reshape(1, -1), jnp.tile(diff_kg[l], 8).reshape(1, -1),
                   diff_lam[l], diff_subg[l].reshape(1, -1), p32, p64, lam_init)
        md = _dsa(yd, jnp.tile(dsa_qg[l], 4).reshape(1, -1), jnp.tile(dsa_kg[l], 2).reshape(1, -1),
                  p64, p64h)
        xf = _outproj(xf, mab.reshape(b * t, -1), mc.reshape(b * t, -1), md.reshape(b * t, -1),
                      w_out[l].astype(BF16))
    return xf.reshape(b, t, d)
```

```python
import functools
import math

import numpy as np
import jax
import jax.numpy as jnp
from jax import lax
from jax.experimental import pallas as pl
from jax.experimental.pallas import tpu as pltpu

F32 = jnp.float32
BF16 = jnp.bfloat16
I32 = jnp.int32

GROUP_W = 256
CHUNK = 64
CONV_W = 3
GMLP_BLOCK = 128
DIFF_QD = 32
DSA_HD = 64
IDX_HD = 32
IDX_HEADS = 4
DSA_TOPK_MAX = 256
EPS = 1e-6
LOG2E = math.log2(math.e)
INT_MIN = -2 ** 31
LANES = 128
SUBLANES = 8

TQ = 128
KC = 512
TM = 512
HALO = 16
NHC = 8

_SLOPES = 2.0 ** (-8.0 * np.arange(1, 9) / 8.0)
SLOPES_C = [float(s) for s in _SLOPES[0::2]]
SLOPES_D = [float(s) for s in _SLOPES[1::2]]

VMEM_LIMIT = 56 * 1024 * 1024


def _block_diag_mean(width, seg):
    idx = np.arange(width) // seg
    return jnp.asarray((idx[:, None] == idx[None, :]).astype(np.float32) / seg, dtype=BF16)


def _seg_mean(x2, p):
    hi = x2.astype(BF16)
    lo = (x2 - hi.astype(F32)).astype(BF16)
    return (jnp.dot(hi, p, preferred_element_type=F32)
            + jnp.dot(lo, p, preferred_element_type=F32))


def _silu(z):
    return z * jax.nn.sigmoid(z)


def _gelu(x):
    return 0.5 * x * (1.0 + lax.erf(x * (2.0 ** -0.5)))


def _fold_rows(x, op):
    r = x.shape[0]
    while r > SUBLANES:
        r //= 2
        x = op(x[:r], x[r:])
    return x


def _fold_lanes(x, op):
    c = x.shape[1]
    while c > LANES:
        c //= 2
        x = op(x[:, :c], x[:, c:])
    return x


def _wprep_kernel(w_ref, o_ref, *, n_valid):
    col = pl.program_id(1) * LANES + lax.broadcasted_iota(I32, (1, LANES), 1)
    o_ref[...] = jnp.where(col < n_valid, w_ref[...], 0.0).astype(BF16)


def _wprep(w):
    depth, d, n = w.shape
    nt = pl.cdiv(n, LANES)
    return pl.pallas_call(
        functools.partial(_wprep_kernel, n_valid=n),
        grid=(depth, nt),
        in_specs=[pl.BlockSpec((None, d, LANES), lambda l, j: (l, 0, j))],
        out_specs=pl.BlockSpec((None, d, LANES), lambda l, j: (l, 0, j)),
        out_shape=jax.ShapeDtypeStruct((depth, d, nt * LANES), BF16),
        compiler_params=pltpu.CompilerParams(dimension_semantics=("arbitrary", "arbitrary")),
        name="wprep",
    )(w)


_A0, _B0, _C0, _D0 = 0, 4 * GROUP_W, 7 * GROUP_W, 11 * GROUP_W
_DQ, _DKV, _DZ, _DIQ, _DEND = _D0, _D0 + 256, _D0 + 384, _D0 + 640, _D0 + 896
YD_W = 896


def _inproj_kernel(x_ref, g_ref, w_ref, ya_ref, yb_ref, yc_ref, yd_ref):
    x = x_ref[...]
    ms = jnp.mean(x * x, axis=-1, keepdims=True)
    xn = (x * lax.rsqrt(ms + EPS) * g_ref[...]).astype(BF16)

    def proj(lo, hi):
        return jnp.dot(xn, w_ref[:, lo:hi], preferred_element_type=F32).astype(BF16)

    ya_ref[...] = proj(_A0, _B0)
    yb_ref[...] = proj(_B0, _C0)
    yc_ref[...] = proj(_C0, _D0)
    yd_ref[:, 0:256] = proj(_DQ, _DKV)
    yd_ref[:, 256:512] = proj(_DZ, _DIQ)
    yd_ref[:, 512:640] = proj(_DKV, _DZ)
    yd_ref[:, 640:896] = proj(_DIQ, _DEND)


def _inproj(xf, g, wb16, layer):
    m, d = xf.shape
    widths = (_B0 - _A0, _C0 - _B0, _D0 - _C0, YD_W)
    return pl.pallas_call(
        _inproj_kernel,
        grid=(m // TM,),
        in_specs=[pl.BlockSpec((TM, d), lambda i: (i, 0)),
                  pl.BlockSpec((1, d), lambda i: (0, 0)),
                  pl.BlockSpec((None,) + wb16.shape[1:], lambda i: (layer, 0, 0))],
        out_specs=[pl.BlockSpec((TM, w), lambda i: (i, 0)) for w in widths],
        out_shape=[jax.ShapeDtypeStruct((m, w), BF16) for w in widths],
        compiler_params=pltpu.CompilerParams(dimension_semantics=("arbitrary",),
                                             vmem_limit_bytes=VMEM_LIMIT),
        name="inproj",
    )(xf, g, wb16)


def _outproj_kernel(x_ref, mab_ref, mc_ref, md_ref, wo_ref, o_ref):
    acc = x_ref[...]
    acc = acc + jnp.dot(mab_ref[...], wo_ref[0:2 * GROUP_W, :], preferred_element_type=F32)
    acc = acc + jnp.dot(mc_ref[...], wo_ref[2 * GROUP_W:3 * GROUP_W, :], preferred_element_type=F32)
    acc = acc + jnp.dot(md_ref[...], wo_ref[3 * GROUP_W:4 * GROUP_W, :], preferred_element_type=F32)
    o_ref[...] = acc


def _outproj(xf, mab, mc, md, wo, layer):
    m, d = xf.shape
    return pl.pallas_call(
        _outproj_kernel,
        grid=(m // TM,),
        in_specs=[pl.BlockSpec((TM, d), lambda i: (i, 0)),
                  pl.BlockSpec((TM, 2 * GROUP_W), lambda i: (i, 0)),
                  pl.BlockSpec((TM, GROUP_W), lambda i: (i, 0)),
                  pl.BlockSpec((TM, GROUP_W), lambda i: (i, 0)),
                  pl.BlockSpec((None,) + wo.shape[1:], lambda i: (layer, 0, 0))],
        out_specs=pl.BlockSpec((TM, d), lambda i: (i, 0)),
        out_shape=jax.ShapeDtypeStruct((m, d), F32),
        compiler_params=pltpu.CompilerParams(dimension_semantics=("arbitrary",),
                                             vmem_limit_bytes=VMEM_LIMIT),
        name="outproj",
    )(xf, mab, mc, md, wo)


def _mixab_kernel(ya_ref, halo_ref, yb_ref, cw_ref, cb_ref, gg_ref, ws_ref, bfull_ref, p64_ref,
                  o_ref, u_scr):
    i = pl.program_id(1)
    W = GROUP_W
    h = ya_ref[:, 0:W].astype(F32)
    bg = ya_ref[:, W:2 * W].astype(F32)
    cg = ya_ref[:, 2 * W:3 * W].astype(F32)
    za = ya_ref[:, 3 * W:4 * W].astype(F32)
    uh = halo_ref[:, 2 * W:3 * W].astype(F32) * halo_ref[:, 0:W].astype(F32)
    u_scr[0:HALO, :] = jnp.where(i > 0, uh, 0.0)
    u_scr[HALO:HALO + TQ, :] = cg * h
    y = u_scr[HALO - 2:HALO - 2 + TQ, :] * cw_ref[0:1, :]
    y = y + u_scr[HALO - 1:HALO - 1 + TQ, :] * cw_ref[1:2, :]
    y = y + u_scr[HALO:HALO + TQ, :] * cw_ref[2:3, :]
    out_a = bg * (y + cb_ref[...]) * _silu(za)
    o_ref[:, 0:W] = out_a.astype(o_ref.dtype)

    u = _gelu(yb_ref[:, 0:W].astype(F32))
    v = _gelu(yb_ref[:, W:2 * W].astype(F32))
    zb = yb_ref[:, 2 * W:3 * W].astype(F32)
    vn = v * lax.rsqrt(_seg_mean(v * v, p64_ref[...]) + EPS) * gg_ref[...]
    t_idx = lax.broadcasted_iota(I32, (GMLP_BLOCK, GMLP_BLOCK), 0)
    s_idx = lax.broadcasted_iota(I32, (GMLP_BLOCK, GMLP_BLOCK), 1)
    causal = (t_idx >> 6) >= (s_idx >> 6)
    lane = lax.broadcasted_iota(I32, (1, W), 1)
    s = bfull_ref[...]
    for hd in range(4):
        wm = jnp.where(causal, ws_ref[hd], 0.0).astype(BF16)
        vh = jnp.where((lane >> 6) == hd, vn, 0.0).astype(BF16)
        s = s + jnp.dot(wm, vh, preferred_element_type=F32)
    out_b = u * s * _silu(zb)
    o_ref[:, W:2 * W] = out_b.astype(o_ref.dtype)


def _mixab(ya, yb, cw, cb, gg, ws, bfull, p64):
    b, t, _ = ya.shape
    nq = t // TQ
    return pl.pallas_call(
        _mixab_kernel,
        grid=(b, nq),
        in_specs=[pl.BlockSpec((None, TQ, 4 * GROUP_W), lambda bi, i: (bi, i, 0)),
                  pl.BlockSpec((None, HALO, 4 * GROUP_W),
                               lambda bi, i: (bi, jnp.maximum(i * (TQ // HALO) - 1, 0), 0)),
                  pl.BlockSpec((None, TQ, 3 * GROUP_W), lambda bi, i: (bi, i, 0)),
                  pl.BlockSpec(cw.shape, lambda bi, i: (0, 0)),
                  pl.BlockSpec(cb.shape, lambda bi, i: (0, 0)),
                  pl.BlockSpec(gg.shape, lambda bi, i: (0, 0)),
                  pl.BlockSpec(ws.shape, lambda bi, i: (0, 0, 0)),
                  pl.BlockSpec(bfull.shape, lambda bi, i: (0, 0)),
                  pl.BlockSpec(p64.shape, lambda bi, i: (0, 0))],
        out_specs=pl.BlockSpec((None, TQ, 2 * GROUP_W), lambda bi, i: (bi, i, 0)),
        out_shape=jax.ShapeDtypeStruct((b, t, 2 * GROUP_W), BF16),
        scratch_shapes=[pltpu.VMEM((HALO + TQ, GROUP_W), F32)],
        compiler_params=pltpu.CompilerParams(dimension_semantics=("arbitrary", "arbitrary"),
                                             vmem_limit_bytes=VMEM_LIMIT),
        name="mixab",
    )(ya, ya, yb, cw, cb, gg, ws, bfull, p64)


def _diff_kernel(q_ref, k_ref, v_ref, z_ref, qg_ref, kg_ref, lam_ref, subg_ref, p32_ref, p64_ref,
                 o_ref, kn_scr, qs_scr, s_scr, m_scr, l_scr, p_scr, acc_scr, *, lam_init, seq):
    i = pl.program_id(1)
    q0 = i * TQ
    nchunk = q0 // KC + 1
    p32 = p32_ref[...]

    @pl.when(i == 0)
    def _():
        for c in range(seq // KC):
            kk = k_ref[c * KC:(c + 1) * KC, :].astype(F32)
            ms = _seg_mean(kk * kk, p32)
            kn_scr[c * KC:(c + 1) * KC, :] = (kk * lax.rsqrt(ms + EPS) * kg_ref[...]).astype(BF16)

    q = q_ref[...].astype(F32)
    qn = q * lax.rsqrt(_seg_mean(q * q, p32) + EPS) * (qg_ref[...] * (DIFF_QD ** -0.5 * LOG2E))
    lane = lax.broadcasted_iota(I32, (1, GROUP_W), 1)
    for hc in range(NHC):
        qs_scr[hc * TQ:(hc + 1) * TQ, :] = jnp.where((lane >> 5) == hc, qn, 0.0).astype(BF16)

    lp = lam_ref[...]
    lam = (jnp.exp(jnp.sum(lp[0:1] * lp[1:2], axis=-1, keepdims=True))
           - jnp.exp(jnp.sum(lp[2:3] * lp[3:4], axis=-1, keepdims=True)) + lam_init)

    m_scr[...] = jnp.full(m_scr.shape, -jnp.inf, F32)
    l_scr[...] = jnp.zeros(l_scr.shape, F32)
    acc_scr[...] = jnp.zeros(acc_scr.shape, F32)
    row = q0 + lax.broadcasted_iota(I32, (TQ, KC), 0)

    def s_body(c, carry):
        col = c * KC + lax.broadcasted_iota(I32, (TQ, KC), 1)
        dist = jnp.abs(row - col).astype(F32)
        dm = jnp.where((col >> 6) <= (row >> 6), dist, jnp.inf)
        kc = kn_scr[pl.ds(pl.multiple_of(c * KC, KC), KC), :]
        s_all = lax.dot_general(qs_scr[...], kc, (((1,), (1,)), ((), ())), preferred_element_type=F32)
        for h in range(NHC // 2):
            bias = (SLOPES_C[h] * LOG2E) * dm
            for j in range(2):
                hc = 2 * h + j
                s = s_all[hc * TQ:(hc + 1) * TQ, :] - bias
                s_scr[c, hc * TQ:(hc + 1) * TQ, :] = s
                m_scr[hc * TQ:(hc + 1) * TQ, :] = jnp.maximum(m_scr[hc * TQ:(hc + 1) * TQ, :],
                                                              _fold_lanes(s, jnp.maximum))
        return carry

    lax.fori_loop(0, nchunk, s_body, 0)

    for hc in range(NHC):
        m = jnp.max(m_scr[hc * TQ:(hc + 1) * TQ, :], axis=1, keepdims=True)
        m_scr[hc * TQ:(hc + 1) * TQ, :] = jnp.broadcast_to(m, (TQ, LANES))

    def e_body(c, carry):
        for hc in range(NHC):
            m = m_scr[hc * TQ:(hc + 1) * TQ, :]
            s = s_scr[c, hc * TQ:(hc + 1) * TQ, :]
            ps = [jnp.exp2(s[:, k * LANES:(k + 1) * LANES] - m) for k in range(KC // LANES)]
            l_scr[hc * TQ:(hc + 1) * TQ, :] += (ps[0] + ps[1]) + (ps[2] + ps[3])
            p_scr[hc * TQ:(hc + 1) * TQ, :] = jnp.concatenate(ps, axis=1).astype(BF16)
        vc = v_ref[pl.ds(pl.multiple_of(c * KC, KC), KC), :]
        acc_scr[...] += jnp.dot(p_scr[...], vc, preferred_element_type=F32)
        return carry

    lax.fori_loop(0, nchunk, e_body, 0)

    o = jnp.zeros((TQ, GROUP_W), F32)
    for h in range(NHC // 2):
        r1 = slice(2 * h * TQ, (2 * h + 1) * TQ)
        r2 = slice((2 * h + 1) * TQ, (2 * h + 2) * TQ)
        l1 = jnp.sum(l_scr[r1, :], axis=1, keepdims=True)
        l2 = jnp.sum(l_scr[r2, :], axis=1, keepdims=True)
        o_h = acc_scr[r1, :] * (1.0 / l1) - acc_scr[r2, :] * (lam / l2)
        o = jnp.where((lane >> 6) == h, o_h, o)

    ms = _seg_mean(o * o, p64_ref[...])
    o = o * lax.rsqrt(ms + EPS) * (subg_ref[...] * (1.0 - lam_init))
    z = z_ref[...].astype(F32)
    o_ref[...] = (o * _silu(z)).astype(o_ref.dtype)


def _diff(yc, qg, kg, lam_p, subg, p32, p64, lam_init):
    b, t, _ = yc.shape
    nq = t // TQ
    nkc = t // KC
    W = GROUP_W
    kern = functools.partial(_diff_kernel, lam_init=lam_init, seq=t)
    small = lambda a: pl.BlockSpec(a.shape, lambda bi, i: (0,) * a.ndim)
    return pl.pallas_call(
        kern,
        grid=(b, nq),
        in_specs=[pl.BlockSpec((None, TQ, W), lambda bi, i: (bi, i, 0)),
                  pl.BlockSpec((None, t, W), lambda bi, i: (bi, 0, 1)),
                  pl.BlockSpec((None, t, W), lambda bi, i: (bi, 0, 2)),
                  pl.BlockSpec((None, TQ, W), lambda bi, i: (bi, i, 3)),
                  small(qg), small(kg), small(lam_p), small(subg), small(p32), small(p64)],
        out_specs=pl.BlockSpec((None, TQ, W), lambda bi, i: (bi, i, 0)),
        out_shape=jax.ShapeDtypeStruct((b, t, W), BF16),
        scratch_shapes=[pltpu.VMEM((t, W), BF16),
                        pltpu.VMEM((NHC * TQ, W), BF16),
                        pltpu.VMEM((nkc, NHC * TQ, KC), F32),
                        pltpu.VMEM((NHC * TQ, LANES), F32),
                        pltpu.VMEM((NHC * TQ, LANES), F32),
                        pltpu.VMEM((NHC * TQ, KC), BF16),
                        pltpu.VMEM((NHC * TQ, W), F32)],
        compiler_params=pltpu.CompilerParams(dimension_semantics=("arbitrary", "arbitrary"),
                                             vmem_limit_bytes=VMEM_LIMIT),
        name="diffattn",
    )(yc, yc, yc, yc, qg, kg, lam_p, subg, p32, p64)


def _dsa_kernel(q_ref, z_ref, iq_ref, ikwq_ref, kv_ref, ikw_ref, qg_ref, kg_ref, p64_ref, p64h_ref,
                o_ref, knv_scr, vt_scr, key_scr, tk_scr, dm_scr, a_scr, p_scr, *, seq, topk):
    i = pl.program_id(1)
    q0 = i * TQ
    nchunk = q0 // KC + 1
    SB = 256
    n_sb = KC // SB
    kf = jnp.float32(topk)
    big = jnp.int32(2 ** 30)

    @pl.when(i == 0)
    def _():
        p64h = p64h_ref[...]
        for c in range(seq // 128):
            blk = kv_ref[c * 128:(c + 1) * 128, :].astype(F32)
            ms = _seg_mean(blk * blk, p64h)
            knv_scr[c * 128:(c + 1) * 128, :] = (blk * lax.rsqrt(ms + EPS) * kg_ref[...]).astype(BF16)
            vt = blk.T
            cc, off = divmod(c * 128, KC)
            vt_scr[cc, :, off:off + 128] = vt[DSA_HD:2 * DSA_HD, :].astype(BF16)

    iq_t = iq_ref[...].astype(F32).T
    iw_t = ikwq_ref[...].astype(F32).T[IDX_HD:IDX_HD + 8, :]
    wq = iw_t * (IDX_HEADS ** -0.5 * IDX_HD ** -0.5)
    zpad_i = jnp.zeros((128 - IDX_HD, TQ), F32)
    rhs_idx = jnp.concatenate(
        [jnp.concatenate([iq_t[IDX_HD * h:IDX_HD * (h + 1), :], zpad_i], axis=0) for h in range(IDX_HEADS)],
        axis=1).astype(BF16)

    q = q_ref[...].astype(F32)
    qn = q * lax.rsqrt(_seg_mean(q * q, p64_ref[...]) + EPS) * (qg_ref[...] * (DSA_HD ** -0.5 * LOG2E))
    qn_t = qn.T
    zpad_q = jnp.zeros((128 - DSA_HD, TQ), F32)
    rhs_main = jnp.concatenate(
        [jnp.concatenate([qn_t[DSA_HD * h:DSA_HD * (h + 1), :], zpad_q], axis=0) for h in range(4)],
        axis=1).astype(BF16)

    qpos = q0 + lax.broadcasted_iota(I32, (SB, TQ), 1)

    def idx_body(c, carry):
        for sb in range(n_sb):
            r0 = pl.multiple_of(c * KC + sb * SB, SB)
            logit = jnp.dot(ikw_ref[pl.ds(r0, SB), :], rhs_idx, preferred_element_type=F32)
            sc = ((jnp.maximum(logit[:, 0:TQ], 0.0) * wq[0:1, :]
                   + jnp.maximum(logit[:, TQ:2 * TQ], 0.0) * wq[1:2, :])
                  + (jnp.maximum(logit[:, 2 * TQ:3 * TQ], 0.0) * wq[2:3, :]
                     + jnp.maximum(logit[:, 3 * TQ:4 * TQ], 0.0) * wq[3:4, :]))
            bits = lax.bitcast_convert_type(sc, I32)
            key = bits ^ ((bits >> 31) & jnp.int32(0x7FFFFFFF))
            key = jnp.where(key == -1, 0, key)
            krow = r0 + lax.broadcasted_iota(I32, (SB, TQ), 0)
            allowed = (krow >> 6) <= (qpos >> 6)
            key_scr[pl.ds(r0, SB), :] = jnp.where(allowed, key, jnp.int32(INT_MIN))
        return carry

    lax.fori_loop(0, nchunk, idx_body, 0)

    def count(pred_fn):
        def body(c, acc):
            r0 = pl.multiple_of(c * KC, KC)
            return acc + _fold_rows(jnp.where(pred_fn(r0), 1.0, 0.0), jnp.add)
        acc = lax.fori_loop(0, nchunk, body, jnp.zeros((SUBLANES, TQ), F32))
        return jnp.sum(acc, axis=0, keepdims=True)

    def bit_body(it, u):
        cand_u = u | jnp.left_shift(jnp.int32(1), 31 - it)
        cand = cand_u ^ jnp.int32(INT_MIN)
        cnt = count(lambda r0: key_scr[pl.ds(r0, KC), :] >= cand)
        return jnp.where(cnt >= kf, cand_u, u)

    tau = lax.fori_loop(0, 32, bit_body, jnp.zeros((1, TQ), I32)) ^ jnp.int32(INT_MIN)

    def tie_body(c, acc):
        r0 = pl.multiple_of(c * KC, KC)
        key = key_scr[pl.ds(r0, KC), :]
        krow = r0 + lax.broadcasted_iota(I32, (KC, TQ), 0)
        tk_scr[pl.ds(r0, KC), :] = jnp.where(key == tau, krow, big)
        return acc + _fold_rows(jnp.where(key > tau, 1.0, 0.0), jnp.add)

    n_gt = lax.fori_loop(0, nchunk, tie_body, jnp.zeros((SUBLANES, TQ), F32))
    need = kf - jnp.sum(n_gt, axis=0, keepdims=True)
    nbits = int(seq).bit_length()

    def jbit_body(it, jj):
        cand = jj | jnp.left_shift(jnp.int32(1), nbits - 1 - it)
        cnt = count(lambda r0: tk_scr[pl.ds(r0, KC), :] < cand)
        return jnp.where(cnt <= need, cand, jj)

    jsel = lax.fori_loop(0, nbits, jbit_body, jnp.zeros((1, TQ), I32))
    jsel = jnp.where(tau == jnp.int32(INT_MIN), 0, jsel)

    def dm_body(c, carry):
        r0 = pl.multiple_of(c * KC, KC)
        krow = r0 + lax.broadcasted_iota(I32, (KC, TQ), 0)
        qp = q0 + lax.broadcasted_iota(I32, (KC, TQ), 1)
        dist = jnp.abs(qp - krow).astype(F32)
        inner = jnp.where(tk_scr[pl.ds(r0, KC), :] < jsel, dist, jnp.inf)
        dm_scr[pl.ds(r0, KC), :] = jnp.where(key_scr[pl.ds(r0, KC), :] > tau, dist, inner)
        return carry

    lax.fori_loop(0, nchunk, dm_body, 0)

    slopes = [s * LOG2E for s in SLOPES_D]

    def a_body(c, ms):
        ms = list(ms)
        for sb in range(n_sb):
            r0 = pl.multiple_of(c * KC + sb * SB, SB)
            att = jnp.dot(knv_scr[pl.ds(r0, SB), :], rhs_main, preferred_element_type=F32)
            dm = dm_scr[pl.ds(r0, SB), :]
            for h in range(4):
                a = att[:, h * TQ:(h + 1) * TQ] - slopes[h] * dm
                a_scr[h, pl.ds(r0, SB), :] = a
                ms[h] = jnp.maximum(ms[h], _fold_rows(a, jnp.maximum))
        return tuple(ms)

    neg = jnp.full((SUBLANES, TQ), -jnp.inf, F32)
    ms = lax.fori_loop(0, nchunk, a_body, (neg, neg, neg, neg))
    ms = [jnp.max(m, axis=0, keepdims=True) for m in ms]

    def e_body(c, ls):
        ls = list(ls)
        for sb in range(n_sb):
            r0 = pl.multiple_of(c * KC + sb * SB, SB)
            for h in range(4):
                p = jnp.exp2(a_scr[h, pl.ds(r0, SB), :] - ms[h])
                ls[h] = ls[h] + _fold_rows(p, jnp.add)
                p_scr[pl.ds(r0, SB), h * TQ:(h + 1) * TQ] = p.astype(BF16)
        return tuple(ls)

    zero = jnp.zeros((SUBLANES, TQ), F32)
    ls = lax.fori_loop(0, nchunk, e_body, (zero, zero, zero, zero))
    ls = [jnp.sum(l, axis=0, keepdims=True) for l in ls]

    def pv_body(c, acc):
        r0 = pl.multiple_of(c * KC, KC)
        return acc + jnp.dot(vt_scr[c], p_scr[pl.ds(r0, KC), :], preferred_element_type=F32)

    out_t = lax.fori_loop(0, nchunk, pv_body, jnp.zeros((DSA_HD, 4 * TQ), F32))
    o_t = jnp.concatenate([out_t[:, h * TQ:(h + 1) * TQ] * (1.0 / ls[h]) for h in range(4)], axis=0)
    o = o_t.T
    z = z_ref[...].astype(F32)
    o_ref[...] = (o * _silu(z)).astype(o_ref.dtype)


def _dsa(yd, qg, kg, p64, p64h):
    b, t, _ = yd.shape
    nq = t // TQ
    nkc = t // KC
    W = GROUP_W
    topk = min(DSA_TOPK_MAX, t // 4)
    kern = functools.partial(_dsa_kernel, seq=t, topk=topk)
    small = lambda a: pl.BlockSpec(a.shape, lambda bi, i: (0,) * a.ndim)
    return pl.pallas_call(
        kern,
        grid=(b, nq),
        in_specs=[pl.BlockSpec((None, TQ, W), lambda bi, i: (bi, i, 0)),
                  pl.BlockSpec((None, TQ, W), lambda bi, i: (bi, i, 1)),
                  pl.BlockSpec((None, TQ, 128), lambda bi, i: (bi, i, 5)),
                  pl.BlockSpec((None, TQ, 128), lambda bi, i: (bi, i, 6)),
                  pl.BlockSpec((None, t, 128), lambda bi, i: (bi, 0, 4)),
                  pl.BlockSpec((None, t, 128), lambda bi, i: (bi, 0, 6)),
                  small(qg), small(kg), small(p64), small(p64h)],
        out_specs=pl.BlockSpec((None, TQ, W), lambda bi, i: (bi, i, 0)),
        out_shape=jax.ShapeDtypeStruct((b, t, W), BF16),
        scratch_shapes=[pltpu.VMEM((t, 128), BF16),
                        pltpu.VMEM((nkc, DSA_HD, KC), BF16),
                        pltpu.VMEM((t, TQ), I32),
                        pltpu.VMEM((t, TQ), I32),
                        pltpu.VMEM((t, TQ), F32),
                        pltpu.VMEM((4, t, TQ), F32),
                        pltpu.VMEM((t, 4 * TQ), BF16)],
        compiler_params=pltpu.CompilerParams(dimension_semantics=("arbitrary", "arbitrary"),
                                             vmem_limit_bytes=VMEM_LIMIT),
        name="dsa",
    )(yd, yd, yd, yd, yd, yd, qg, kg, p64, p64h)


def kernel(x, norm_g, w_in, conv_w, conv_b, gmlp_g, gmlp_ws, gmlp_b, diff_qg, diff_kg, diff_lam,
           diff_subg, dsa_qg, dsa_kg, w_out):
    b, t, d = x.shape
    depth = w_in.shape[0]
    p32 = _block_diag_mean(GROUP_W, 32)
    p64 = _block_diag_mean(GROUP_W, 64)
    p64h = _block_diag_mean(128, 64)
    w_in16 = _wprep(w_in)
    w_out16 = _wprep(w_out)
    xf = x.reshape(b * t, d)
    for l in range(depth):
        ya, yb, yc, yd = _inproj(xf, norm_g[l].reshape(1, d), w_in16, l)
        ya, yb, yc, yd = (a.reshape(b, t, a.shape[-1]) for a in (ya, yb, yc, yd))
        bfull = jnp.repeat(gmlp_b[l].T, GROUP_W // 4, axis=1)
        mab = _mixab(ya, yb, conv_w[l], conv_b[l].reshape(1, -1), gmlp_g[l].reshape(1, -1),
                     gmlp_ws[l], bfull, p64)
        lam_init = 0.8 - 0.6 * math.exp(-0.3 * l)
        mc = _diff(yc, jnp.tile(diff_qg[l], 8).reshape(1, -1), jnp.tile(diff_kg[l], 8).reshape(1, -1),
                   diff_lam[l], diff_subg[l].reshape(1, -1), p32, p64, lam_init)
        md = _dsa(yd, jnp.tile(dsa_qg[l], 4).reshape(1, -1), jnp.tile(dsa_kg[l], 2).reshape(1, -1),
                  p64, p64h)
        xf = _outproj(xf, mab.reshape(b * t, -1), mc.reshape(b * t, -1), md.reshape(b * t, -1),
                      w_out16, l)
    return xf.reshape(b, t, d)
```

```python
import functools
import math

import numpy as np
import jax
import jax.numpy as jnp
from jax import lax
from jax.experimental import pallas as pl
from jax.experimental.pallas import tpu as pltpu

F32 = jnp.float32
BF16 = jnp.bfloat16
I32 = jnp.int32
I16 = jnp.int16

GROUP_W = 256
CHUNK = 64
CONV_W = 3
GMLP_BLOCK = 128
DIFF_QD = 32
DSA_HD = 64
IDX_HD = 32
IDX_HEADS = 4
DSA_TOPK_MAX = 256
EPS = 1e-6
LOG2E = math.log2(math.e)
INT_MIN = -2 ** 31
LANES = 128
SUBLANES = 8
PACKED_ROWS = 16

TQ = 128
KC = 512
TM = 512
HALO = 16
NHC = 8

_SLOPES = 2.0 ** (-8.0 * np.arange(1, 9) / 8.0)
SLOPES_C = [float(s) for s in _SLOPES[0::2]]
SLOPES_D = [float(s) for s in _SLOPES[1::2]]

VMEM_LIMIT = 56 * 1024 * 1024


def _block_diag_mean(width, seg):
    idx = np.arange(width) // seg
    return jnp.asarray((idx[:, None] == idx[None, :]).astype(np.float32) / seg, dtype=BF16)


def _seg_mean(x2, p):
    hi = x2.astype(BF16)
    lo = (x2 - hi.astype(F32)).astype(BF16)
    return (jnp.dot(hi, p, preferred_element_type=F32)
            + jnp.dot(lo, p, preferred_element_type=F32))


def _silu(z):
    return z * jax.nn.sigmoid(z)


def _gelu(x):
    return 0.5 * x * (1.0 + lax.erf(x * (2.0 ** -0.5)))


def _fold_rows(x, op, stop=SUBLANES):
    r = x.shape[0]
    while r > stop:
        r //= 2
        x = op(x[:r], x[r:])
    return x


def _fold_lanes(x, op):
    c = x.shape[1]
    while c > LANES:
        c //= 2
        x = op(x[:, :c], x[:, c:])
    return x


def _wprep_kernel(w_ref, o_ref, *, n_valid):
    col = pl.program_id(1) * LANES + lax.broadcasted_iota(I32, (1, LANES), 1)
    o_ref[...] = jnp.where(col < n_valid, w_ref[...], 0.0).astype(BF16)


def _wprep(w):
    depth, d, n = w.shape
    nt = pl.cdiv(n, LANES)
    return pl.pallas_call(
        functools.partial(_wprep_kernel, n_valid=n),
        grid=(depth, nt),
        in_specs=[pl.BlockSpec((None, d, LANES), lambda l, j: (l, 0, j))],
        out_specs=pl.BlockSpec((None, d, LANES), lambda l, j: (l, 0, j)),
        out_shape=jax.ShapeDtypeStruct((depth, d, nt * LANES), BF16),
        compiler_params=pltpu.CompilerParams(dimension_semantics=("arbitrary", "arbitrary")),
        name="wprep",
    )(w)


_A0, _B0, _C0, _D0 = 0, 4 * GROUP_W, 7 * GROUP_W, 11 * GROUP_W
_DQ, _DKV, _DZ, _DIQ, _DEND = _D0, _D0 + 256, _D0 + 384, _D0 + 640, _D0 + 896
YD_W = 896


def _inproj_kernel(x_ref, g_ref, w_ref, ya_ref, yb_ref, yc_ref, yd_ref):
    x = x_ref[...]
    ms = jnp.mean(x * x, axis=-1, keepdims=True)
    xn = (x * lax.rsqrt(ms + EPS) * g_ref[...]).astype(BF16)

    def proj(lo, hi):
        return jnp.dot(xn, w_ref[:, lo:hi], preferred_element_type=F32).astype(BF16)

    ya_ref[...] = proj(_A0, _B0)
    yb_ref[...] = proj(_B0, _C0)
    yc_ref[...] = proj(_C0, _D0)
    yd_ref[:, 0:256] = proj(_DQ, _DKV)
    yd_ref[:, 256:512] = proj(_DZ, _DIQ)
    yd_ref[:, 512:640] = proj(_DKV, _DZ)
    yd_ref[:, 640:896] = proj(_DIQ, _DEND)


def _inproj(xf, g, wb16, layer):
    m, d = xf.shape
    widths = (_B0 - _A0, _C0 - _B0, _D0 - _C0, YD_W)
    return pl.pallas_call(
        _inproj_kernel,
        grid=(m // TM,),
        in_specs=[pl.BlockSpec((TM, d), lambda i: (i, 0)),
                  pl.BlockSpec((1, d), lambda i: (0, 0)),
                  pl.BlockSpec((None,) + wb16.shape[1:], lambda i: (layer, 0, 0))],
        out_specs=[pl.BlockSpec((TM, w), lambda i: (i, 0)) for w in widths],
        out_shape=[jax.ShapeDtypeStruct((m, w), BF16) for w in widths],
        compiler_params=pltpu.CompilerParams(dimension_semantics=("arbitrary",),
                                             vmem_limit_bytes=VMEM_LIMIT),
        name="inproj",
    )(xf, g, wb16)


def _outproj_kernel(x_ref, mab_ref, mc_ref, md_ref, wo_ref, o_ref):
    acc = x_ref[...]
    acc = acc + jnp.dot(mab_ref[...], wo_ref[0:2 * GROUP_W, :], preferred_element_type=F32)
    acc = acc + jnp.dot(mc_ref[...], wo_ref[2 * GROUP_W:3 * GROUP_W, :], preferred_element_type=F32)
    acc = acc + jnp.dot(md_ref[...], wo_ref[3 * GROUP_W:4 * GROUP_W, :], preferred_element_type=F32)
    o_ref[...] = acc


def _outproj(xf, mab, mc, md, wo, layer):
    m, d = xf.shape
    return pl.pallas_call(
        _outproj_kernel,
        grid=(m // TM,),
        in_specs=[pl.BlockSpec((TM, d), lambda i: (i, 0)),
                  pl.BlockSpec((TM, 2 * GROUP_W), lambda i: (i, 0)),
                  pl.BlockSpec((TM, GROUP_W), lambda i: (i, 0)),
                  pl.BlockSpec((TM, GROUP_W), lambda i: (i, 0)),
                  pl.BlockSpec((None,) + wo.shape[1:], lambda i: (layer, 0, 0))],
        out_specs=pl.BlockSpec((TM, d), lambda i: (i, 0)),
        out_shape=jax.ShapeDtypeStruct((m, d), F32),
        compiler_params=pltpu.CompilerParams(dimension_semantics=("arbitrary",),
                                             vmem_limit_bytes=VMEM_LIMIT),
        name="outproj",
    )(xf, mab, mc, md, wo)


def _mixab_kernel(ya_ref, halo_ref, yb_ref, cw_ref, cb_ref, gg_ref, ws_ref, bfull_ref, p64_ref,
                  o_ref, u_scr):
    i = pl.program_id(1)
    W = GROUP_W
    h = ya_ref[:, 0:W].astype(F32)
    bg = ya_ref[:, W:2 * W].astype(F32)
    cg = ya_ref[:, 2 * W:3 * W].astype(F32)
    za = ya_ref[:, 3 * W:4 * W].astype(F32)
    uh = halo_ref[:, 2 * W:3 * W].astype(F32) * halo_ref[:, 0:W].astype(F32)
    u_scr[0:HALO, :] = jnp.where(i > 0, uh, 0.0)
    u_scr[HALO:HALO + TQ, :] = cg * h
    y = u_scr[HALO - 2:HALO - 2 + TQ, :] * cw_ref[0:1, :]
    y = y + u_scr[HALO - 1:HALO - 1 + TQ, :] * cw_ref[1:2, :]
    y = y + u_scr[HALO:HALO + TQ, :] * cw_ref[2:3, :]
    out_a = bg * (y + cb_ref[...]) * _silu(za)
    o_ref[:, 0:W] = out_a.astype(o_ref.dtype)

    u = _gelu(yb_ref[:, 0:W].astype(F32))
    v = _gelu(yb_ref[:, W:2 * W].astype(F32))
    zb = yb_ref[:, 2 * W:3 * W].astype(F32)
    vn = v * lax.rsqrt(_seg_mean(v * v, p64_ref[...]) + EPS) * gg_ref[...]
    t_idx = lax.broadcasted_iota(I32, (GMLP_BLOCK, GMLP_BLOCK), 0)
    s_idx = lax.broadcasted_iota(I32, (GMLP_BLOCK, GMLP_BLOCK), 1)
    causal = (t_idx >> 6) >= (s_idx >> 6)
    lane = lax.broadcasted_iota(I32, (1, W), 1)
    s = bfull_ref[...]
    for hd in range(4):
        wm = jnp.where(causal, ws_ref[hd], 0.0).astype(BF16)
        vh = jnp.where((lane >> 6) == hd, vn, 0.0).astype(BF16)
        s = s + jnp.dot(wm, vh, preferred_element_type=F32)
    out_b = u * s * _silu(zb)
    o_ref[:, W:2 * W] = out_b.astype(o_ref.dtype)


def _mixab(ya, yb, cw, cb, gg, ws, bfull, p64):
    b, t, _ = ya.shape
    nq = t // TQ
    return pl.pallas_call(
        _mixab_kernel,
        grid=(b, nq),
        in_specs=[pl.BlockSpec((None, TQ, 4 * GROUP_W), lambda bi, i: (bi, i, 0)),
                  pl.BlockSpec((None, HALO, 4 * GROUP_W),
                               lambda bi, i: (bi, jnp.maximum(i * (TQ // HALO) - 1, 0), 0)),
                  pl.BlockSpec((None, TQ, 3 * GROUP_W), lambda bi, i: (bi, i, 0)),
                  pl.BlockSpec(cw.shape, lambda bi, i: (0, 0)),
                  pl.BlockSpec(cb.shape, lambda bi, i: (0, 0)),
                  pl.BlockSpec(gg.shape, lambda bi, i: (0, 0)),
                  pl.BlockSpec(ws.shape, lambda bi, i: (0, 0, 0)),
                  pl.BlockSpec(bfull.shape, lambda bi, i: (0, 0)),
                  pl.BlockSpec(p64.shape, lambda bi, i: (0, 0))],
        out_specs=pl.BlockSpec((None, TQ, 2 * GROUP_W), lambda bi, i: (bi, i, 0)),
        out_shape=jax.ShapeDtypeStruct((b, t, 2 * GROUP_W), BF16),
        scratch_shapes=[pltpu.VMEM((HALO + TQ, GROUP_W), F32)],
        compiler_params=pltpu.CompilerParams(dimension_semantics=("arbitrary", "arbitrary"),
                                             vmem_limit_bytes=VMEM_LIMIT),
        name="mixab",
    )(ya, ya, yb, cw, cb, gg, ws, bfull, p64)


def _diff_kernel(q_ref, k_ref, v_ref, z_ref, qg_ref, kg_ref, lam_ref, subg_ref, p32_ref, p64_ref,
                 o_ref, kn_scr, qs_scr, s_scr, m_scr, l_scr, p_scr, acc_scr, *, lam_init, seq):
    i = pl.program_id(1)
    q0 = i * TQ
    nchunk = q0 // KC + 1
    p32 = p32_ref[...]

    @pl.when(i == 0)
    def _():
        for c in range(seq // KC):
            kk = k_ref[c * KC:(c + 1) * KC, :].astype(F32)
            ms = _seg_mean(kk * kk, p32)
            kn_scr[c * KC:(c + 1) * KC, :] = (kk * lax.rsqrt(ms + EPS) * kg_ref[...]).astype(BF16)

    q = q_ref[...].astype(F32)
    qn = q * lax.rsqrt(_seg_mean(q * q, p32) + EPS) * (qg_ref[...] * (DIFF_QD ** -0.5 * LOG2E))
    lane = lax.broadcasted_iota(I32, (1, GROUP_W), 1)
    for hc in range(NHC):
        qs_scr[hc * TQ:(hc + 1) * TQ, :] = jnp.where((lane >> 5) == hc, qn, 0.0).astype(BF16)

    lp = lam_ref[...]
    lam = (jnp.exp(jnp.sum(lp[0:1] * lp[1:2], axis=-1, keepdims=True))
           - jnp.exp(jnp.sum(lp[2:3] * lp[3:4], axis=-1, keepdims=True)) + lam_init)

    m_scr[...] = jnp.full(m_scr.shape, -jnp.inf, F32)
    l_scr[...] = jnp.zeros(l_scr.shape, F32)
    acc_scr[...] = jnp.zeros(acc_scr.shape, F32)
    row = q0 + lax.broadcasted_iota(I32, (TQ, KC), 0)

    def s_body(c, carry):
        col = c * KC + lax.broadcasted_iota(I32, (TQ, KC), 1)
        dist = jnp.abs(row - col).astype(F32)
        dm = jnp.where((col >> 6) <= (row >> 6), dist, jnp.inf)
        kc = kn_scr[pl.ds(pl.multiple_of(c * KC, KC), KC), :]
        s_all = lax.dot_general(qs_scr[...], kc, (((1,), (1,)), ((), ())), preferred_element_type=F32)
        for h in range(NHC // 2):
            bias = (SLOPES_C[h] * LOG2E) * dm
            for j in range(2):
                hc = 2 * h + j
                s = s_all[hc * TQ:(hc + 1) * TQ, :] - bias
                s_scr[c, hc * TQ:(hc + 1) * TQ, :] = s
                m_scr[hc * TQ:(hc + 1) * TQ, :] = jnp.maximum(m_scr[hc * TQ:(hc + 1) * TQ, :],
                                                              _fold_lanes(s, jnp.maximum))
        return carry

    lax.fori_loop(0, nchunk, s_body, 0)

    for hc in range(NHC):
        m = jnp.max(m_scr[hc * TQ:(hc + 1) * TQ, :], axis=1, keepdims=True)
        m_scr[hc * TQ:(hc + 1) * TQ, :] = jnp.broadcast_to(m, (TQ, LANES))

    def e_body(c, carry):
        for hc in range(NHC):
            m = m_scr[hc * TQ:(hc + 1) * TQ, :]
            s = s_scr[c, hc * TQ:(hc + 1) * TQ, :]
            ps = [jnp.exp2(s[:, k * LANES:(k + 1) * LANES] - m) for k in range(KC // LANES)]
            l_scr[hc * TQ:(hc + 1) * TQ, :] += (ps[0] + ps[1]) + (ps[2] + ps[3])
            p_scr[hc * TQ:(hc + 1) * TQ, :] = jnp.concatenate(ps, axis=1).astype(BF16)
        vc = v_ref[pl.ds(pl.multiple_of(c * KC, KC), KC), :]
        acc_scr[...] += jnp.dot(p_scr[...], vc, preferred_element_type=F32)
        return carry

    lax.fori_loop(0, nchunk, e_body, 0)

    o = jnp.zeros((TQ, GROUP_W), F32)
    for h in range(NHC // 2):
        r1 = slice(2 * h * TQ, (2 * h + 1) * TQ)
        r2 = slice((2 * h + 1) * TQ, (2 * h + 2) * TQ)
        l1 = jnp.sum(l_scr[r1, :], axis=1, keepdims=True)
        l2 = jnp.sum(l_scr[r2, :], axis=1, keepdims=True)
        o_h = acc_scr[r1, :] * (1.0 / l1) - acc_scr[r2, :] * (lam / l2)
        o = jnp.where((lane >> 6) == h, o_h, o)

    ms = _seg_mean(o * o, p64_ref[...])
    o = o * lax.rsqrt(ms + EPS) * (subg_ref[...] * (1.0 - lam_init))
    z = z_ref[...].astype(F32)
    o_ref[...] = (o * _silu(z)).astype(o_ref.dtype)


def _diff(yc, qg, kg, lam_p, subg, p32, p64, lam_init):
    b, t, _ = yc.shape
    nq = t // TQ
    nkc = t // KC
    W = GROUP_W
    kern = functools.partial(_diff_kernel, lam_init=lam_init, seq=t)
    small = lambda a: pl.BlockSpec(a.shape, lambda bi, i: (0,) * a.ndim)
    return pl.pallas_call(
        kern,
        grid=(b, nq),
        in_specs=[pl.BlockSpec((None, TQ, W), lambda bi, i: (bi, i, 0)),
                  pl.BlockSpec((None, t, W), lambda bi, i: (bi, 0, 1)),
                  pl.BlockSpec((None, t, W), lambda bi, i: (bi, 0, 2)),
                  pl.BlockSpec((None, TQ, W), lambda bi, i: (bi, i, 3)),
                  small(qg), small(kg), small(lam_p), small(subg), small(p32), small(p64)],
        out_specs=pl.BlockSpec((None, TQ, W), lambda bi, i: (bi, i, 0)),
        out_shape=jax.ShapeDtypeStruct((b, t, W), BF16),
        scratch_shapes=[pltpu.VMEM((t, W), BF16),
                        pltpu.VMEM((NHC * TQ, W), BF16),
                        pltpu.VMEM((nkc, NHC * TQ, KC), F32),
                        pltpu.VMEM((NHC * TQ, LANES), F32),
                        pltpu.VMEM((NHC * TQ, LANES), F32),
                        pltpu.VMEM((NHC * TQ, KC), BF16),
                        pltpu.VMEM((NHC * TQ, W), F32)],
        compiler_params=pltpu.CompilerParams(dimension_semantics=("arbitrary", "arbitrary"),
                                             vmem_limit_bytes=VMEM_LIMIT),
        name="diffattn",
    )(yc, yc, yc, yc, qg, kg, lam_p, subg, p32, p64)


def _dsa_kernel(q_ref, z_ref, iq_ref, ikwq_ref, kv_ref, ikw_ref, qg_ref, kg_ref, p64_ref, p64h_ref,
                o_ref, knv_scr, vt_scr, key_scr, hi_scr, lo_scr, tk_scr, dm_scr, a_scr, p_scr, *, seq, topk):
    i = pl.program_id(1)
    q0 = i * TQ
    nchunk = q0 // KC + 1

    @pl.when(i == 0)
    def _():
        p64h = p64h_ref[...]
        for c in range(seq // 128):
            blk = kv_ref[c * 128:(c + 1) * 128, :].astype(F32)
            ms = _seg_mean(blk * blk, p64h)
            knv_scr[c * 128:(c + 1) * 128, :] = (blk * lax.rsqrt(ms + EPS) * kg_ref[...]).astype(BF16)
            vt = blk.T
            cc, off = divmod(c * 128, KC)
            vt_scr[cc, :, off:off + 128] = vt[DSA_HD:2 * DSA_HD, :].astype(BF16)

    iq_t = iq_ref[...].astype(F32).T
    iw_t = ikwq_ref[...].astype(F32).T[IDX_HD:IDX_HD + 8, :]
    wq = iw_t * (IDX_HEADS ** -0.5 * IDX_HD ** -0.5)
    zpad_i = jnp.zeros((128 - IDX_HD, TQ), F32)
    rhs_idx = jnp.concatenate(
        [jnp.concatenate([iq_t[IDX_HD * h:IDX_HD * (h + 1), :], zpad_i], axis=0) for h in range(IDX_HEADS)],
        axis=1).astype(BF16)

    q = q_ref[...].astype(F32)
    qn = q * lax.rsqrt(_seg_mean(q * q, p64_ref[...]) + EPS) * (qg_ref[...] * (DSA_HD ** -0.5 * LOG2E))
    qn_t = qn.T
    zpad_q = jnp.zeros((128 - DSA_HD, TQ), F32)
    rhs_main = jnp.concatenate(
        [jnp.concatenate([qn_t[DSA_HD * h:DSA_HD * (h + 1), :], zpad_q], axis=0) for h in range(4)],
        axis=1).astype(BF16)

    for n in range(1, seq // KC + 1):
        pl.when(nchunk == n)(functools.partial(
            _dsa_tile, n, q0, rhs_idx, wq, rhs_main, z_ref, ikw_ref, o_ref, knv_scr, vt_scr, key_scr,
            hi_scr, lo_scr, tk_scr, dm_scr, a_scr, p_scr, seq, topk))


def _dsa_tile(nchunk, q0, rhs_idx, wq, rhs_main, z_ref, ikw_ref, o_ref, knv_scr, vt_scr, key_scr,
              hi_scr, lo_scr, tk_scr, dm_scr, a_scr, p_scr, seq, topk):
    SB = 256
    n_sb = KC // SB

    def over_chunks(body, init):
        acc = init
        for c in range(nchunk):
            acc = body(c, acc)
        return acc

    qpos = q0 + lax.broadcasted_iota(I32, (SB, TQ), 1)

    def idx_body(c, carry):
        for sb in range(n_sb):
            r0 = c * KC + sb * SB
            logit = jnp.dot(ikw_ref[pl.ds(r0, SB), :], rhs_idx, preferred_element_type=F32)
            sc = ((jnp.maximum(logit[:, 0:TQ], 0.0) * wq[0:1, :]
                   + jnp.maximum(logit[:, TQ:2 * TQ], 0.0) * wq[1:2, :])
                  + (jnp.maximum(logit[:, 2 * TQ:3 * TQ], 0.0) * wq[2:3, :]
                     + jnp.maximum(logit[:, 3 * TQ:4 * TQ], 0.0) * wq[3:4, :]))
            bits = lax.bitcast_convert_type(sc, I32)
            key = bits ^ ((bits >> 31) & jnp.int32(0x7FFFFFFF))
            key = jnp.where(key == -1, 0, key)
            krow = r0 + lax.broadcasted_iota(I32, (SB, TQ), 0)
            allowed = (krow >> 6) <= (qpos >> 6)
            key = jnp.where(allowed, key, jnp.int32(INT_MIN))
            key_scr[pl.ds(r0, SB), :] = key
            hi_scr[pl.ds(r0, SB), :] = (key >> 16).astype(I16)
            lo_scr[pl.ds(r0, SB), :] = ((key & jnp.int32(0xFFFF)) - 32768).astype(I16)
        return carry

    over_chunks(idx_body, 0)

    one16, zero16 = jnp.int16(1), jnp.int16(0)
    low16 = jnp.int16(-32768)

    def rows_of(scr, c):
        return scr[pl.ds(c * KC, KC), :]

    def finish16(acc):
        return jnp.sum(acc.astype(I32), axis=0, keepdims=True)

    def count16(scr, pred):
        def body(c, acc):
            return acc + _fold_rows(jnp.where(pred(rows_of(scr, c)), one16, zero16), jnp.add, PACKED_ROWS)
        return finish16(over_chunks(body, jnp.zeros((PACKED_ROWS, TQ), I16)))

    def search16(scr, k_needed):
        def body(it, u):
            cand_u = u | jnp.left_shift(jnp.int32(1), 15 - it)
            cand = (cand_u - 32768).astype(I16)
            cnt = count16(scr, lambda blk: blk >= cand)
            return jnp.where(cnt >= k_needed, cand_u, u)
        return lax.fori_loop(0, 16, body, jnp.zeros((1, TQ), I32))

    u_hi = search16(hi_scr, jnp.int32(topk))
    tau_hi = u_hi - 32768
    tau_hi16 = tau_hi.astype(I16)

    def mid_body(c, acc):
        r0 = c * KC
        hi = hi_scr[pl.ds(r0, KC), :]
        lo_scr[pl.ds(r0, KC), :] = jnp.where(hi == tau_hi16, lo_scr[pl.ds(r0, KC), :], low16)
        return acc + _fold_rows(jnp.where(hi > tau_hi16, one16, zero16), jnp.add, PACKED_ROWS)

    n_gt_hi = finish16(over_chunks(mid_body, jnp.zeros((PACKED_ROWS, TQ), I16)))
    u_lo = search16(lo_scr, jnp.int32(topk) - n_gt_hi)
    tau_lo16 = (u_lo - 32768).astype(I16)
    tau = jnp.left_shift(tau_hi, 16) + u_lo

    def tie_body(c, acc):
        r0 = c * KC
        hi = hi_scr[pl.ds(r0, KC), :]
        lo = lo_scr[pl.ds(r0, KC), :]
        pos = (r0 + lax.broadcasted_iota(I32, (KC, TQ), 0)).astype(I16)
        far = jnp.int16(32767)
        tk_scr[pl.ds(r0, KC), :] = jnp.where(hi == tau_hi16, jnp.where(lo == tau_lo16, pos, far), far)
        return acc + _fold_rows(jnp.where(lo > tau_lo16, one16, zero16), jnp.add, PACKED_ROWS)

    n_gt = n_gt_hi + finish16(over_chunks(tie_body, jnp.zeros((PACKED_ROWS, TQ), I16)))
    need = jnp.int32(topk) - n_gt
    nbits = int(seq).bit_length()

    def jbit_body(it, jj):
        cand = jj | jnp.left_shift(jnp.int32(1), nbits - 1 - it)
        cand16 = cand.astype(I16)
        cnt = count16(tk_scr, lambda blk: blk < cand16)
        return jnp.where(cnt <= need, cand, jj)

    jsel = lax.fori_loop(0, nbits, jbit_body, jnp.zeros((1, TQ), I32))
    jsel = jnp.where(tau == jnp.int32(INT_MIN), 0, jsel)

    def dm_body(c, carry):
        r0 = c * KC
        krow = r0 + lax.broadcasted_iota(I32, (KC, TQ), 0)
        qp = q0 + lax.broadcasted_iota(I32, (KC, TQ), 1)
        dist = jnp.abs(qp - krow).astype(F32)
        key = key_scr[pl.ds(r0, KC), :]
        inner = jnp.where(key == tau, jnp.where(krow < jsel, dist, jnp.inf), jnp.inf)
        dm_scr[pl.ds(r0, KC), :] = jnp.where(key > tau, dist, inner)
        return carry

    over_chunks(dm_body, 0)

    slopes = [s * LOG2E for s in SLOPES_D]

    def a_body(c, ms):
        ms = list(ms)
        for sb in range(n_sb):
            r0 = c * KC + sb * SB
            att = jnp.dot(knv_scr[pl.ds(r0, SB), :], rhs_main, preferred_element_type=F32)
            dm = dm_scr[pl.ds(r0, SB), :]
            for h in range(4):
                a = att[:, h * TQ:(h + 1) * TQ] - slopes[h] * dm
                a_scr[h, pl.ds(r0, SB), :] = a
                ms[h] = jnp.maximum(ms[h], _fold_rows(a, jnp.maximum))
        return tuple(ms)

    neg = jnp.full((SUBLANES, TQ), -jnp.inf, F32)
    ms = over_chunks(a_body, (neg, neg, neg, neg))
    ms = [jnp.max(m, axis=0, keepdims=True) for m in ms]

    def e_body(c, carry):
        ls, acc = list(carry[:4]), carry[4]
        for sb in range(n_sb):
            r0 = c * KC + sb * SB
            for h in range(4):
                p = jnp.exp2(a_scr[h, pl.ds(r0, SB), :] - ms[h])
                ls[h] = ls[h] + _fold_rows(p, jnp.add)
                p_scr[pl.ds(r0, SB), h * TQ:(h + 1) * TQ] = p.astype(BF16)
        r0 = c * KC
        acc = acc + jnp.dot(vt_scr[c], p_scr[pl.ds(r0, KC), :], preferred_element_type=F32)
        return (*ls, acc)

    zero = jnp.zeros((SUBLANES, TQ), F32)
    res = over_chunks(e_body, (zero, zero, zero, zero, jnp.zeros((DSA_HD, 4 * TQ), F32)))
    ls = [jnp.sum(l, axis=0, keepdims=True) for l in res[:4]]
    out_t = res[4]
    o_t = jnp.concatenate([out_t[:, h * TQ:(h + 1) * TQ] * (1.0 / ls[h]) for h in range(4)], axis=0)
    o = o_t.T
    z = z_ref[...].astype(F32)
    o_ref[...] = (o * _silu(z)).astype(o_ref.dtype)


def _dsa(yd, qg, kg, p64, p64h):
    b, t, _ = yd.shape
    nq = t // TQ
    nkc = t // KC
    W = GROUP_W
    topk = min(DSA_TOPK_MAX, t // 4)
    kern = functools.partial(_dsa_kernel, seq=t, topk=topk)
    small = lambda a: pl.BlockSpec(a.shape, lambda bi, i: (0,) * a.ndim)
    return pl.pallas_call(
        kern,
        grid=(b, nq),
        in_specs=[pl.BlockSpec((None, TQ, W), lambda bi, i: (bi, i, 0)),
                  pl.BlockSpec((None, TQ, W), lambda bi, i: (bi, i, 1)),
                  pl.BlockSpec((None, TQ, 128), lambda bi, i: (bi, i, 5)),
                  pl.BlockSpec((None, TQ, 128), lambda bi, i: (bi, i, 6)),
                  pl.BlockSpec((None, t, 128), lambda bi, i: (bi, 0, 4)),
                  pl.BlockSpec((None, t, 128), lambda bi, i: (bi, 0, 6)),
                  small(qg), small(kg), small(p64), small(p64h)],
        out_specs=pl.BlockSpec((None, TQ, W), lambda bi, i: (bi, i, 0)),
        out_shape=jax.ShapeDtypeStruct((b, t, W), BF16),
        scratch_shapes=[pltpu.VMEM((t, 128), BF16),
                        pltpu.VMEM((nkc, DSA_HD, KC), BF16),
                        pltpu.VMEM((t, TQ), I32),
                        pltpu.VMEM((t, TQ), I16),
                        pltpu.VMEM((t, TQ), I16),
                        pltpu.VMEM((t, TQ), I16),
                        pltpu.VMEM((t, TQ), F32),
                        pltpu.VMEM((4, t, TQ), F32),
                        pltpu.VMEM((t, 4 * TQ), BF16)],
        compiler_params=pltpu.CompilerParams(dimension_semantics=("arbitrary", "arbitrary"),
                                             vmem_limit_bytes=VMEM_LIMIT),
        name="dsa",
    )(yd, yd, yd, yd, yd, yd, qg, kg, p64, p64h)


def kernel(x, norm_g, w_in, conv_w, conv_b, gmlp_g, gmlp_ws, gmlp_b, diff_qg, diff_kg, diff_lam,
           diff_subg, dsa_qg, dsa_kg, w_out):
    b, t, d = x.shape
    depth = w_in.shape[0]
    p32 = _block_diag_mean(GROUP_W, 32)
    p64 = _block_diag_mean(GROUP_W, 64)
    p64h = _block_diag_mean(128, 64)
    w_in16 = _wprep(w_in)
    w_out16 = _wprep(w_out)
    xf = x.reshape(b * t, d)
    for l in range(depth):
        ya, yb, yc, yd = _inproj(xf, norm_g[l].reshape(1, d), w_in16, l)
        ya, yb, yc, yd = (a.reshape(b, t, a.shape[-1]) for a in (ya, yb, yc, yd))
        bfull = jnp.repeat(gmlp_b[l].T, GROUP_W // 4, axis=1)
        mab = _mixab(ya, yb, conv_w[l], conv_b[l].reshape(1, -1), gmlp_g[l].reshape(1, -1),
                     gmlp_ws[l], bfull, p64)
        lam_init = 0.8 - 0.6 * math.exp(-0.3 * l)
        mc = _diff(yc, jnp.tile(diff_qg[l], 8).reshape(1, -1), jnp.tile(diff_kg[l], 8).reshape(1, -1),
                   diff_lam[l], diff_subg[l].reshape(1, -1), p32, p64, lam_init)
        md = _dsa(yd, jnp.tile(dsa_qg[l], 4).reshape(1, -1), jnp.tile(dsa_kg[l], 2).reshape(1, -1),
                  p64, p64h)
        xf = _outproj(xf, mab.reshape(b * t, -1), mc.reshape(b * t, -1), md.reshape(b * t, -1),
                      w_out16, l)
    return xf.reshape(b, t, d)
```

```python
import functools
import math

import numpy as np
import jax
import jax.numpy as jnp
from jax import lax
from jax.experimental import pallas as pl
from jax.experimental.pallas import tpu as pltpu

F32 = jnp.float32
BF16 = jnp.bfloat16
I32 = jnp.int32
I16 = jnp.int16

GROUP_W = 256
CHUNK = 64
CONV_W = 3
GMLP_BLOCK = 128
DIFF_QD = 32
DSA_HD = 64
IDX_HD = 32
IDX_HEADS = 4
DSA_TOPK_MAX = 256
EPS = 1e-6
LOG2E = math.log2(math.e)
INT_MIN = -2 ** 31
LANES = 128
SUBLANES = 8
PACKED_ROWS = 16

TQ = 128
KC = 512
TM = 512
HALO = 16
NHC = 8

_SLOPES = 2.0 ** (-8.0 * np.arange(1, 9) / 8.0)
SLOPES_C = [float(s) for s in _SLOPES[0::2]]
SLOPES_D = [float(s) for s in _SLOPES[1::2]]

VMEM_LIMIT = 56 * 1024 * 1024


def _block_diag_mean(width, seg):
    idx = np.arange(width) // seg
    return jnp.asarray((idx[:, None] == idx[None, :]).astype(np.float32) / seg, dtype=BF16)


def _seg_mean(x2, p):
    hi = x2.astype(BF16)
    lo = (x2 - hi.astype(F32)).astype(BF16)
    return (jnp.dot(hi, p, preferred_element_type=F32)
            + jnp.dot(lo, p, preferred_element_type=F32))


def _silu(z):
    return z * jax.nn.sigmoid(z)


def _gelu(x):
    return 0.5 * x * (1.0 + lax.erf(x * (2.0 ** -0.5)))


def _fold_rows(x, op, stop=SUBLANES):
    r = x.shape[0]
    while r > stop:
        r //= 2
        x = op(x[:r], x[r:])
    return x


def _fold_lanes(x, op):
    c = x.shape[1]
    while c > LANES:
        c //= 2
        x = op(x[:, :c], x[:, c:])
    return x


def _wprep_kernel(w_ref, o_ref, *, n_valid):
    col = pl.program_id(1) * LANES + lax.broadcasted_iota(I32, (1, LANES), 1)
    o_ref[...] = jnp.where(col < n_valid, w_ref[...], 0.0).astype(BF16)


def _wprep(w):
    depth, d, n = w.shape
    nt = pl.cdiv(n, LANES)
    return pl.pallas_call(
        functools.partial(_wprep_kernel, n_valid=n),
        grid=(depth, nt),
        in_specs=[pl.BlockSpec((None, d, LANES), lambda l, j: (l, 0, j))],
        out_specs=pl.BlockSpec((None, d, LANES), lambda l, j: (l, 0, j)),
        out_shape=jax.ShapeDtypeStruct((depth, d, nt * LANES), BF16),
        compiler_params=pltpu.CompilerParams(dimension_semantics=("arbitrary", "arbitrary")),
        name="wprep",
    )(w)


_A0, _B0, _C0, _D0 = 0, 4 * GROUP_W, 7 * GROUP_W, 11 * GROUP_W
_DQ, _DKV, _DZ, _DIQ, _DEND = _D0, _D0 + 256, _D0 + 384, _D0 + 640, _D0 + 896
YD_W = 896


def _inproj_kernel(x_ref, g_ref, w_ref, ya_ref, yb_ref, yc_ref, yd_ref):
    x = x_ref[...]
    ms = jnp.mean(x * x, axis=-1, keepdims=True)
    xn = (x * lax.rsqrt(ms + EPS) * g_ref[...]).astype(BF16)

    def proj(lo, hi):
        return jnp.dot(xn, w_ref[:, lo:hi], preferred_element_type=F32).astype(BF16)

    ya_ref[...] = proj(_A0, _B0)
    yb_ref[...] = proj(_B0, _C0)
    yc_ref[...] = proj(_C0, _D0)
    yd_ref[:, 0:256] = proj(_DQ, _DKV)
    yd_ref[:, 256:512] = proj(_DZ, _DIQ)
    yd_ref[:, 512:640] = proj(_DKV, _DZ)
    yd_ref[:, 640:896] = proj(_DIQ, _DEND)


def _inproj(xf, g, wb16, layer):
    m, d = xf.shape
    widths = (_B0 - _A0, _C0 - _B0, _D0 - _C0, YD_W)
    return pl.pallas_call(
        _inproj_kernel,
        grid=(m // TM,),
        in_specs=[pl.BlockSpec((TM, d), lambda i: (i, 0)),
                  pl.BlockSpec((1, d), lambda i: (0, 0)),
                  pl.BlockSpec((None,) + wb16.shape[1:], lambda i: (layer, 0, 0))],
        out_specs=[pl.BlockSpec((TM, w), lambda i: (i, 0)) for w in widths],
        out_shape=[jax.ShapeDtypeStruct((m, w), BF16) for w in widths],
        compiler_params=pltpu.CompilerParams(dimension_semantics=("arbitrary",),
                                             vmem_limit_bytes=VMEM_LIMIT),
        name="inproj",
    )(xf, g, wb16)


def _outproj_kernel(x_ref, mab_ref, mc_ref, md_ref, wo_ref, o_ref):
    acc = x_ref[...]
    acc = acc + jnp.dot(mab_ref[...], wo_ref[0:2 * GROUP_W, :], preferred_element_type=F32)
    acc = acc + jnp.dot(mc_ref[...], wo_ref[2 * GROUP_W:3 * GROUP_W, :], preferred_element_type=F32)
    acc = acc + jnp.dot(md_ref[...], wo_ref[3 * GROUP_W:4 * GROUP_W, :], preferred_element_type=F32)
    o_ref[...] = acc


def _outproj(xf, mab, mc, md, wo, layer):
    m, d = xf.shape
    return pl.pallas_call(
        _outproj_kernel,
        grid=(m // TM,),
        in_specs=[pl.BlockSpec((TM, d), lambda i: (i, 0)),
                  pl.BlockSpec((TM, 2 * GROUP_W), lambda i: (i, 0)),
                  pl.BlockSpec((TM, GROUP_W), lambda i: (i, 0)),
                  pl.BlockSpec((TM, GROUP_W), lambda i: (i, 0)),
                  pl.BlockSpec((None,) + wo.shape[1:], lambda i: (layer, 0, 0))],
        out_specs=pl.BlockSpec((TM, d), lambda i: (i, 0)),
        out_shape=jax.ShapeDtypeStruct((m, d), F32),
        compiler_params=pltpu.CompilerParams(dimension_semantics=("arbitrary",),
                                             vmem_limit_bytes=VMEM_LIMIT),
        name="outproj",
    )(xf, mab, mc, md, wo)


def _mixab_kernel(ya_ref, halo_ref, yb_ref, cw_ref, cb_ref, gg_ref, ws_ref, bfull_ref, p64_ref,
                  o_ref, u_scr):
    i = pl.program_id(1)
    W = GROUP_W
    h = ya_ref[:, 0:W].astype(F32)
    bg = ya_ref[:, W:2 * W].astype(F32)
    cg = ya_ref[:, 2 * W:3 * W].astype(F32)
    za = ya_ref[:, 3 * W:4 * W].astype(F32)
    uh = halo_ref[:, 2 * W:3 * W].astype(F32) * halo_ref[:, 0:W].astype(F32)
    u_scr[0:HALO, :] = jnp.where(i > 0, uh, 0.0)
    u_scr[HALO:HALO + TQ, :] = cg * h
    y = u_scr[HALO - 2:HALO - 2 + TQ, :] * cw_ref[0:1, :]
    y = y + u_scr[HALO - 1:HALO - 1 + TQ, :] * cw_ref[1:2, :]
    y = y + u_scr[HALO:HALO + TQ, :] * cw_ref[2:3, :]
    out_a = bg * (y + cb_ref[...]) * _silu(za)
    o_ref[:, 0:W] = out_a.astype(o_ref.dtype)

    u = _gelu(yb_ref[:, 0:W].astype(F32))
    v = _gelu(yb_ref[:, W:2 * W].astype(F32))
    zb = yb_ref[:, 2 * W:3 * W].astype(F32)
    vn = v * lax.rsqrt(_seg_mean(v * v, p64_ref[...]) + EPS) * gg_ref[...]
    t_idx = lax.broadcasted_iota(I32, (GMLP_BLOCK, GMLP_BLOCK), 0)
    s_idx = lax.broadcasted_iota(I32, (GMLP_BLOCK, GMLP_BLOCK), 1)
    causal = (t_idx >> 6) >= (s_idx >> 6)
    lane = lax.broadcasted_iota(I32, (1, W), 1)
    s = bfull_ref[...]
    for hd in range(4):
        wm = jnp.where(causal, ws_ref[hd], 0.0).astype(BF16)
        vh = jnp.where((lane >> 6) == hd, vn, 0.0).astype(BF16)
        s = s + jnp.dot(wm, vh, preferred_element_type=F32)
    out_b = u * s * _silu(zb)
    o_ref[:, W:2 * W] = out_b.astype(o_ref.dtype)


def _mixab(ya, yb, cw, cb, gg, ws, bfull, p64):
    b, t, _ = ya.shape
    nq = t // TQ
    return pl.pallas_call(
        _mixab_kernel,
        grid=(b, nq),
        in_specs=[pl.BlockSpec((None, TQ, 4 * GROUP_W), lambda bi, i: (bi, i, 0)),
                  pl.BlockSpec((None, HALO, 4 * GROUP_W),
                               lambda bi, i: (bi, jnp.maximum(i * (TQ // HALO) - 1, 0), 0)),
                  pl.BlockSpec((None, TQ, 3 * GROUP_W), lambda bi, i: (bi, i, 0)),
                  pl.BlockSpec(cw.shape, lambda bi, i: (0, 0)),
                  pl.BlockSpec(cb.shape, lambda bi, i: (0, 0)),
                  pl.BlockSpec(gg.shape, lambda bi, i: (0, 0)),
                  pl.BlockSpec(ws.shape, lambda bi, i: (0, 0, 0)),
                  pl.BlockSpec(bfull.shape, lambda bi, i: (0, 0)),
                  pl.BlockSpec(p64.shape, lambda bi, i: (0, 0))],
        out_specs=pl.BlockSpec((None, TQ, 2 * GROUP_W), lambda bi, i: (bi, i, 0)),
        out_shape=jax.ShapeDtypeStruct((b, t, 2 * GROUP_W), BF16),
        scratch_shapes=[pltpu.VMEM((HALO + TQ, GROUP_W), F32)],
        compiler_params=pltpu.CompilerParams(dimension_semantics=("arbitrary", "arbitrary"),
                                             vmem_limit_bytes=VMEM_LIMIT),
        name="mixab",
    )(ya, ya, yb, cw, cb, gg, ws, bfull, p64)


def _diff_kernel(q_ref, k_ref, v_ref, z_ref, qg_ref, kg_ref, lam_ref, subg_ref, p32_ref, p64_ref,
                 o_ref, kn_scr, qs_scr, s_scr, m_scr, l_scr, p_scr, acc_scr, *, lam_init, seq):
    i = pl.program_id(1)
    q0 = i * TQ
    nchunk = q0 // KC + 1
    p32 = p32_ref[...]

    @pl.when(i == 0)
    def _():
        for c in range(seq // KC):
            kk = k_ref[c * KC:(c + 1) * KC, :].astype(F32)
            ms = _seg_mean(kk * kk, p32)
            kn_scr[c * KC:(c + 1) * KC, :] = (kk * lax.rsqrt(ms + EPS) * kg_ref[...]).astype(BF16)

    q = q_ref[...].astype(F32)
    qn = q * lax.rsqrt(_seg_mean(q * q, p32) + EPS) * (qg_ref[...] * (DIFF_QD ** -0.5 * LOG2E))
    lane = lax.broadcasted_iota(I32, (1, GROUP_W), 1)
    for hc in range(NHC):
        qs_scr[hc * TQ:(hc + 1) * TQ, :] = jnp.where((lane >> 5) == hc, qn, 0.0).astype(BF16)

    lp = lam_ref[...]
    lam = (jnp.exp(jnp.sum(lp[0:1] * lp[1:2], axis=-1, keepdims=True))
           - jnp.exp(jnp.sum(lp[2:3] * lp[3:4], axis=-1, keepdims=True)) + lam_init)

    m_scr[...] = jnp.full(m_scr.shape, -jnp.inf, F32)
    l_scr[...] = jnp.zeros(l_scr.shape, F32)
    acc_scr[...] = jnp.zeros(acc_scr.shape, F32)
    row = q0 + lax.broadcasted_iota(I32, (TQ, KC), 0)

    def s_body(c, carry):
        col = c * KC + lax.broadcasted_iota(I32, (TQ, KC), 1)
        dist = jnp.abs(row - col).astype(F32)
        dm = jnp.where((col >> 6) <= (row >> 6), dist, jnp.inf)
        kc = kn_scr[pl.ds(pl.multiple_of(c * KC, KC), KC), :]
        s_all = lax.dot_general(qs_scr[...], kc, (((1,), (1,)), ((), ())), preferred_element_type=F32)
        for h in range(NHC // 2):
            bias = (SLOPES_C[h] * LOG2E) * dm
            for j in range(2):
                rows = slice((2 * h + j) * TQ, (2 * h + j + 1) * TQ)
                s = s_all[rows, :] - bias
                s_scr[c, rows, :] = s
                m_scr[rows, :] = jnp.maximum(m_scr[rows, :], _fold_lanes(s, jnp.maximum))
        return carry

    lax.fori_loop(0, nchunk, s_body, 0)

    for hc in range(NHC):
        m = jnp.max(m_scr[hc * TQ:(hc + 1) * TQ, :], axis=1, keepdims=True)
        m_scr[hc * TQ:(hc + 1) * TQ, :] = jnp.broadcast_to(m, (TQ, LANES))

    def e_body(c, carry):
        for hc in range(NHC):
            rows = slice(hc * TQ, (hc + 1) * TQ)
            m = m_scr[rows, :]
            s = s_scr[c, rows, :]
            ps = [jnp.exp2(s[:, k * LANES:(k + 1) * LANES] - m) for k in range(KC // LANES)]
            l_scr[rows, :] += (ps[0] + ps[1]) + (ps[2] + ps[3])
            p_scr[rows, :] = jnp.concatenate(ps, axis=1).astype(BF16)
        vc = v_ref[pl.ds(pl.multiple_of(c * KC, KC), KC), :]
        acc_scr[...] += jnp.dot(p_scr[...], vc, preferred_element_type=F32)
        return carry

    lax.fori_loop(0, nchunk, e_body, 0)

    o = jnp.zeros((TQ, GROUP_W), F32)
    for h in range(NHC // 2):
        r1 = slice(2 * h * TQ, (2 * h + 1) * TQ)
        r2 = slice((2 * h + 1) * TQ, (2 * h + 2) * TQ)
        l1 = jnp.sum(l_scr[r1, :], axis=1, keepdims=True)
        l2 = jnp.sum(l_scr[r2, :], axis=1, keepdims=True)
        o_h = acc_scr[r1, :] * (1.0 / l1) - acc_scr[r2, :] * (lam / l2)
        o = jnp.where((lane >> 6) == h, o_h, o)

    ms = _seg_mean(o * o, p64_ref[...])
    o = o * lax.rsqrt(ms + EPS) * (subg_ref[...] * (1.0 - lam_init))
    z = z_ref[...].astype(F32)
    o_ref[...] = (o * _silu(z)).astype(o_ref.dtype)


def _diff(yc, qg, kg, lam_p, subg, p32, p64, lam_init):
    b, t, _ = yc.shape
    nq = t // TQ
    nkc = t // KC
    W = GROUP_W
    kern = functools.partial(_diff_kernel, lam_init=lam_init, seq=t)
    small = lambda a: pl.BlockSpec(a.shape, lambda bi, i: (0,) * a.ndim)
    return pl.pallas_call(
        kern,
        grid=(b, nq),
        in_specs=[pl.BlockSpec((None, TQ, W), lambda bi, i: (bi, i, 0)),
                  pl.BlockSpec((None, t, W), lambda bi, i: (bi, 0, 1)),
                  pl.BlockSpec((None, t, W), lambda bi, i: (bi, 0, 2)),
                  pl.BlockSpec((None, TQ, W), lambda bi, i: (bi, i, 3)),
                  small(qg), small(kg), small(lam_p), small(subg), small(p32), small(p64)],
        out_specs=pl.BlockSpec((None, TQ, W), lambda bi, i: (bi, i, 0)),
        out_shape=jax.ShapeDtypeStruct((b, t, W), BF16),
        scratch_shapes=[pltpu.VMEM((t, W), BF16),
                        pltpu.VMEM((NHC * TQ, W), BF16),
                        pltpu.VMEM((nkc, NHC * TQ, KC), F32),
                        pltpu.VMEM((NHC * TQ, LANES), F32),
                        pltpu.VMEM((NHC * TQ, LANES), F32),
                        pltpu.VMEM((NHC * TQ, KC), BF16),
                        pltpu.VMEM((NHC * TQ, W), F32)],
        compiler_params=pltpu.CompilerParams(dimension_semantics=("arbitrary", "arbitrary"),
                                             vmem_limit_bytes=VMEM_LIMIT),
        name="diffattn",
    )(yc, yc, yc, yc, qg, kg, lam_p, subg, p32, p64)


def _dsa_kernel(q_ref, z_ref, iq_ref, ikwq_ref, kv_ref, ikw_ref, qg_ref, kg_ref, p64_ref, p64h_ref,
                o_ref, knv_scr, vt_scr, key_scr, hi_scr, lo_scr, tk_scr, dm_scr, a_scr, p_scr, *, seq, topk):
    i = pl.program_id(1)
    q0 = i * TQ
    nchunk = q0 // KC + 1

    @pl.when(i == 0)
    def _():
        p64h = p64h_ref[...]
        for c in range(seq // 128):
            blk = kv_ref[c * 128:(c + 1) * 128, :].astype(F32)
            ms = _seg_mean(blk * blk, p64h)
            knv_scr[c * 128:(c + 1) * 128, :] = (blk * lax.rsqrt(ms + EPS) * kg_ref[...]).astype(BF16)
            vt = blk.T
            cc, off = divmod(c * 128, KC)
            vt_scr[cc, :, off:off + 128] = vt[DSA_HD:2 * DSA_HD, :].astype(BF16)

    iq_t = iq_ref[...].astype(F32).T
    iw_t = ikwq_ref[...].astype(F32).T[IDX_HD:IDX_HD + 8, :]
    wq = iw_t * (IDX_HEADS ** -0.5 * IDX_HD ** -0.5)
    zpad_i = jnp.zeros((128 - IDX_HD, TQ), F32)
    rhs_idx = jnp.concatenate(
        [jnp.concatenate([iq_t[IDX_HD * h:IDX_HD * (h + 1), :], zpad_i], axis=0) for h in range(IDX_HEADS)],
        axis=1).astype(BF16)

    q = q_ref[...].astype(F32)
    qn = q * lax.rsqrt(_seg_mean(q * q, p64_ref[...]) + EPS) * (qg_ref[...] * (DSA_HD ** -0.5 * LOG2E))
    qn_t = qn.T
    zpad_q = jnp.zeros((128 - DSA_HD, TQ), F32)
    rhs_main = jnp.concatenate(
        [jnp.concatenate([qn_t[DSA_HD * h:DSA_HD * (h + 1), :], zpad_q], axis=0) for h in range(4)],
        axis=1).astype(BF16)

    _dsa_tile(nchunk, q0, rhs_idx, wq, rhs_main, z_ref, ikw_ref, o_ref, knv_scr, vt_scr, key_scr,
              hi_scr, lo_scr, tk_scr, dm_scr, a_scr, p_scr, seq, topk)


def _dsa_tile(nchunk, q0, rhs_idx, wq, rhs_main, z_ref, ikw_ref, o_ref, knv_scr, vt_scr, key_scr,
              hi_scr, lo_scr, tk_scr, dm_scr, a_scr, p_scr, seq, topk):
    SB = 256
    n_sb = KC // SB

    def over_chunks(body, init):
        return lax.fori_loop(0, nchunk, body, init)

    qpos = q0 + lax.broadcasted_iota(I32, (SB, TQ), 1)

    def idx_body(c, carry):
        for sb in range(n_sb):
            r0 = pl.multiple_of(c * KC + sb * SB, SB)
            logit = jnp.dot(ikw_ref[pl.ds(r0, SB), :], rhs_idx, preferred_element_type=F32)
            sc = ((jnp.maximum(logit[:, 0:TQ], 0.0) * wq[0:1, :]
                   + jnp.maximum(logit[:, TQ:2 * TQ], 0.0) * wq[1:2, :])
                  + (jnp.maximum(logit[:, 2 * TQ:3 * TQ], 0.0) * wq[2:3, :]
                     + jnp.maximum(logit[:, 3 * TQ:4 * TQ], 0.0) * wq[3:4, :]))
            bits = lax.bitcast_convert_type(sc, I32)
            key = bits ^ ((bits >> 31) & jnp.int32(0x7FFFFFFF))
            key = jnp.where(key == -1, 0, key)
            krow = r0 + lax.broadcasted_iota(I32, (SB, TQ), 0)
            allowed = (krow >> 6) <= (qpos >> 6)
            key = jnp.where(allowed, key, jnp.int32(INT_MIN))
            key_scr[pl.ds(r0, SB), :] = key
            hi_scr[pl.ds(r0, SB), :] = (key >> 16).astype(I16)
            lo_scr[pl.ds(r0, SB), :] = ((key & jnp.int32(0xFFFF)) - 32768).astype(I16)
        return carry

    over_chunks(idx_body, 0)

    one16, zero16 = jnp.int16(1), jnp.int16(0)
    low16 = jnp.int16(-32768)

    def finish16(acc):
        return jnp.sum(acc.astype(I32), axis=0, keepdims=True)

    def count16(n, scr, pred):
        acc = jnp.zeros((PACKED_ROWS, TQ), I16)
        for c in range(n):
            blk = scr[c * KC:(c + 1) * KC, :]
            acc = acc + _fold_rows(jnp.where(pred(blk), one16, zero16), jnp.add, PACKED_ROWS)
        return finish16(acc)

    def bit_search(nbits, step):
        def variant(n):
            def run():
                return lax.fori_loop(0, nbits, lambda it, u: step(n, it, u), jnp.zeros((1, TQ), I32))
            return run
        return lax.switch(nchunk - 1, [variant(n) for n in range(1, seq // KC + 1)])

    def search16(scr, k_needed):
        def step(n, it, u):
            cand_u = u | jnp.left_shift(jnp.int32(1), 15 - it)
            cand = (cand_u - 32768).astype(I16)
            cnt = count16(n, scr, lambda blk: blk >= cand)
            return jnp.where(cnt >= k_needed, cand_u, u)
        return bit_search(16, step)

    u_hi = search16(hi_scr, jnp.int32(topk))
    tau_hi = u_hi - 32768
    tau_hi16 = tau_hi.astype(I16)

    def mid_body(c, acc):
        r0 = pl.multiple_of(c * KC, KC)
        hi = hi_scr[pl.ds(r0, KC), :]
        lo_scr[pl.ds(r0, KC), :] = jnp.where(hi == tau_hi16, lo_scr[pl.ds(r0, KC), :], low16)
        return acc + _fold_rows(jnp.where(hi > tau_hi16, one16, zero16), jnp.add, PACKED_ROWS)

    n_gt_hi = finish16(over_chunks(mid_body, jnp.zeros((PACKED_ROWS, TQ), I16)))
    u_lo = search16(lo_scr, jnp.int32(topk) - n_gt_hi)
    tau_lo16 = (u_lo - 32768).astype(I16)
    tau = jnp.left_shift(tau_hi, 16) + u_lo

    def tie_body(c, acc):
        r0 = pl.multiple_of(c * KC, KC)
        hi = hi_scr[pl.ds(r0, KC), :]
        lo = lo_scr[pl.ds(r0, KC), :]
        pos = (r0 + lax.broadcasted_iota(I32, (KC, TQ), 0)).astype(I16)
        far = jnp.int16(32767)
        tk_scr[pl.ds(r0, KC), :] = jnp.where(hi == tau_hi16, jnp.where(lo == tau_lo16, pos, far), far)
        return acc + _fold_rows(jnp.where(lo > tau_lo16, one16, zero16), jnp.add, PACKED_ROWS)

    n_gt = n_gt_hi + finish16(over_chunks(tie_body, jnp.zeros((PACKED_ROWS, TQ), I16)))
    need = jnp.int32(topk) - n_gt
    nbits = int(seq).bit_length()

    def jbit_step(n, it, jj):
        cand = jj | jnp.left_shift(jnp.int32(1), nbits - 1 - it)
        cand16 = cand.astype(I16)
        cnt = count16(n, tk_scr, lambda blk: blk < cand16)
        return jnp.where(cnt <= need, cand, jj)

    jsel = bit_search(nbits, jbit_step)
    jsel = jnp.where(tau == jnp.int32(INT_MIN), 0, jsel)

    def dm_body(c, carry):
        r0 = pl.multiple_of(c * KC, KC)
        krow = r0 + lax.broadcasted_iota(I32, (KC, TQ), 0)
        qp = q0 + lax.broadcasted_iota(I32, (KC, TQ), 1)
        dist = jnp.abs(qp - krow).astype(F32)
        key = key_scr[pl.ds(r0, KC), :]
        inner = jnp.where(key == tau, jnp.where(krow < jsel, dist, jnp.inf), jnp.inf)
        dm_scr[pl.ds(r0, KC), :] = jnp.where(key > tau, dist, inner)
        return carry

    over_chunks(dm_body, 0)

    slopes = [s * LOG2E for s in SLOPES_D]

    def a_body(c, ms):
        ms = list(ms)
        for sb in range(n_sb):
            r0 = pl.multiple_of(c * KC + sb * SB, SB)
            att = jnp.dot(knv_scr[pl.ds(r0, SB), :], rhs_main, preferred_element_type=F32)
            dm = dm_scr[pl.ds(r0, SB), :]
            for h in range(4):
                a = att[:, h * TQ:(h + 1) * TQ] - slopes[h] * dm
                a_scr[h, pl.ds(r0, SB), :] = a
                ms[h] = jnp.maximum(ms[h], _fold_rows(a, jnp.maximum))
        return tuple(ms)

    neg = jnp.full((SUBLANES, TQ), -jnp.inf, F32)
    ms = over_chunks(a_body, (neg, neg, neg, neg))
    ms = [jnp.max(m, axis=0, keepdims=True) for m in ms]

    def e_body(c, carry):
        ls, acc = list(carry[:4]), carry[4]
        for sb in range(n_sb):
            r0 = pl.multiple_of(c * KC + sb * SB, SB)
            for h in range(4):
                p = jnp.exp2(a_scr[h, pl.ds(r0, SB), :] - ms[h])
                ls[h] = ls[h] + _fold_rows(p, jnp.add)
                p_scr[pl.ds(r0, SB), h * TQ:(h + 1) * TQ] = p.astype(BF16)
        r0 = pl.multiple_of(c * KC, KC)
        acc = acc + jnp.dot(vt_scr[c], p_scr[pl.ds(r0, KC), :], preferred_element_type=F32)
        return (*ls, acc)

    zero = jnp.zeros((SUBLANES, TQ), F32)
    res = over_chunks(e_body, (zero, zero, zero, zero, jnp.zeros((DSA_HD, 4 * TQ), F32)))
    ls = [jnp.sum(l, axis=0, keepdims=True) for l in res[:4]]
    out_t = res[4]
    o_t = jnp.concatenate([out_t[:, h * TQ:(h + 1) * TQ] * (1.0 / ls[h]) for h in range(4)], axis=0)
    o = o_t.T
    z = z_ref[...].astype(F32)
    o_ref[...] = (o * _silu(z)).astype(o_ref.dtype)


def _dsa(yd, qg, kg, p64, p64h):
    b, t, _ = yd.shape
    nq = t // TQ
    nkc = t // KC
    W = GROUP_W
    topk = min(DSA_TOPK_MAX, t // 4)
    kern = functools.partial(_dsa_kernel, seq=t, topk=topk)
    small = lambda a: pl.BlockSpec(a.shape, lambda bi, i: (0,) * a.ndim)
    return pl.pallas_call(
        kern,
        grid=(b, nq),
        in_specs=[pl.BlockSpec((None, TQ, W), lambda bi, i: (bi, i, 0)),
                  pl.BlockSpec((None, TQ, W), lambda bi, i: (bi, i, 1)),
                  pl.BlockSpec((None, TQ, 128), lambda bi, i: (bi, i, 5)),
                  pl.BlockSpec((None, TQ, 128), lambda bi, i: (bi, i, 6)),
                  pl.BlockSpec((None, t, 128), lambda bi, i: (bi, 0, 4)),
                  pl.BlockSpec((None, t, 128), lambda bi, i: (bi, 0, 6)),
                  small(qg), small(kg), small(p64), small(p64h)],
        out_specs=pl.BlockSpec((None, TQ, W), lambda bi, i: (bi, i, 0)),
        out_shape=jax.ShapeDtypeStruct((b, t, W), BF16),
        scratch_shapes=[pltpu.VMEM((t, 128), BF16),
                        pltpu.VMEM((nkc, DSA_HD, KC), BF16),
                        pltpu.VMEM((t, TQ), I32),
                        pltpu.VMEM((t, TQ), I16),
                        pltpu.VMEM((t, TQ), I16),
                        pltpu.VMEM((t, TQ), I16),
                        pltpu.VMEM((t, TQ), F32),
                        pltpu.VMEM((4, t, TQ), F32),
                        pltpu.VMEM((t, 4 * TQ), BF16)],
        compiler_params=pltpu.CompilerParams(dimension_semantics=("arbitrary", "arbitrary"),
                                             vmem_limit_bytes=VMEM_LIMIT),
        name="dsa",
    )(yd, yd, yd, yd, yd, yd, qg, kg, p64, p64h)


def kernel(x, norm_g, w_in, conv_w, conv_b, gmlp_g, gmlp_ws, gmlp_b, diff_qg, diff_kg, diff_lam,
           diff_subg, dsa_qg, dsa_kg, w_out):
    b, t, d = x.shape
    depth = w_in.shape[0]
    p32 = _block_diag_mean(GROUP_W, 32)
    p64 = _block_diag_mean(GROUP_W, 64)
    p64h = _block_diag_mean(128, 64)
    w_in16 = _wprep(w_in)
    w_out16 = _wprep(w_out)
    xf = x.reshape(b * t, d)
    for l in range(depth):
        ya, yb, yc, yd = _inproj(xf, norm_g[l].reshape(1, d), w_in16, l)
        ya, yb, yc, yd = (a.reshape(b, t, a.shape[-1]) for a in (ya, yb, yc, yd))
        bfull = jnp.repeat(gmlp_b[l].T, GROUP_W // 4, axis=1)
        mab = _mixab(ya, yb, conv_w[l], conv_b[l].reshape(1, -1), gmlp_g[l].reshape(1, -1),
                     gmlp_ws[l], bfull, p64)
        lam_init = 0.8 - 0.6 * math.exp(-0.3 * l)
        mc = _diff(yc, jnp.tile(diff_qg[l], 8).reshape(1, -1), jnp.tile(diff_kg[l], 8).reshape(1, -1),
                   diff_lam[l], diff_subg[l].reshape(1, -1), p32, p64, lam_init)
        md = _dsa(yd, jnp.tile(dsa_qg[l], 4).reshape(1, -1), jnp.tile(dsa_kg[l], 2).reshape(1, -1),
                  p64, p64h)
        xf = _outproj(xf, mab.reshape(b * t, -1), mc.reshape(b * t, -1), md.reshape(b * t, -1),
                      w_out16, l)
    return xf.reshape(b, t, d)
```

```python
import functools
import math

import numpy as np
import jax
import jax.numpy as jnp
from jax import lax
from jax.experimental import pallas as pl
from jax.experimental.pallas import tpu as pltpu

F32 = jnp.float32
BF16 = jnp.bfloat16
I32 = jnp.int32

GROUP_W = 256
CHUNK = 64
CONV_W = 3
GMLP_BLOCK = 128
DIFF_QD = 32
DSA_HD = 64
IDX_HD = 32
IDX_HEADS = 4
DSA_TOPK_MAX = 256
EPS = 1e-6
LOG2E = math.log2(math.e)
INT_MIN = -2 ** 31
LANES = 128
SUBLANES = 8

TQ = 128
TQ_DIFF = 256
KC = 512
TM = 512
HALO = 16
NHC = 8

_SLOPES = 2.0 ** (-8.0 * np.arange(1, 9) / 8.0)
SLOPES_C = [float(s) for s in _SLOPES[0::2]]
SLOPES_D = [float(s) for s in _SLOPES[1::2]]

VMEM_LIMIT = 56 * 1024 * 1024


def _block_diag_mean(width, seg):
    idx = np.arange(width) // seg
    return jnp.asarray((idx[:, None] == idx[None, :]).astype(np.float32) / seg, dtype=BF16)


def _seg_mean(x2, p):
    hi = x2.astype(BF16)
    lo = (x2 - hi.astype(F32)).astype(BF16)
    return (jnp.dot(hi, p, preferred_element_type=F32)
            + jnp.dot(lo, p, preferred_element_type=F32))


def _silu(z):
    return z * jax.nn.sigmoid(z)


def _gelu(x):
    return 0.5 * x * (1.0 + lax.erf(x * (2.0 ** -0.5)))


def _fold_rows(x, op, stop=SUBLANES):
    r = x.shape[0]
    while r > stop:
        r //= 2
        x = op(x[:r], x[r:])
    return x


def _fold_lanes(x, op):
    c = x.shape[1]
    while c > LANES:
        c //= 2
        x = op(x[:, :c], x[:, c:])
    return x


def _wprep_kernel(w_ref, o_ref, *, n_valid):
    col = pl.program_id(1) * LANES + lax.broadcasted_iota(I32, (1, LANES), 1)
    o_ref[...] = jnp.where(col < n_valid, w_ref[...], 0.0).astype(BF16)


def _wprep(w):
    depth, d, n = w.shape
    nt = pl.cdiv(n, LANES)
    return pl.pallas_call(
        functools.partial(_wprep_kernel, n_valid=n),
        grid=(depth, nt),
        in_specs=[pl.BlockSpec((None, d, LANES), lambda l, j: (l, 0, j))],
        out_specs=pl.BlockSpec((None, d, LANES), lambda l, j: (l, 0, j)),
        out_shape=jax.ShapeDtypeStruct((depth, d, nt * LANES), BF16),
        compiler_params=pltpu.CompilerParams(dimension_semantics=("arbitrary", "arbitrary")),
        name="wprep",
    )(w)


_A0, _B0, _C0, _D0 = 0, 4 * GROUP_W, 7 * GROUP_W, 11 * GROUP_W
_DQ, _DKV, _DZ, _DIQ, _DEND = _D0, _D0 + 256, _D0 + 384, _D0 + 640, _D0 + 896
YD_W = 896


def _inproj_kernel(x_ref, g_ref, w_ref, ya_ref, yb_ref, yc_ref, yd_ref):
    x = x_ref[...]
    ms = jnp.mean(x * x, axis=-1, keepdims=True)
    xn = (x * lax.rsqrt(ms + EPS) * g_ref[...]).astype(BF16)

    def proj(lo, hi):
        return jnp.dot(xn, w_ref[:, lo:hi], preferred_element_type=F32).astype(BF16)

    ya_ref[...] = proj(_A0, _B0)
    yb_ref[...] = proj(_B0, _C0)
    yc_ref[...] = proj(_C0, _D0)
    yd_ref[:, 0:256] = proj(_DQ, _DKV)
    yd_ref[:, 256:512] = proj(_DZ, _DIQ)
    yd_ref[:, 512:640] = proj(_DKV, _DZ)
    yd_ref[:, 640:896] = proj(_DIQ, _DEND)


def _inproj(xf, g, wb16, layer):
    m, d = xf.shape
    widths = (_B0 - _A0, _C0 - _B0, _D0 - _C0, YD_W)
    return pl.pallas_call(
        _inproj_kernel,
        grid=(m // TM,),
        in_specs=[pl.BlockSpec((TM, d), lambda i: (i, 0)),
                  pl.BlockSpec((1, d), lambda i: (0, 0)),
                  pl.BlockSpec((None,) + wb16.shape[1:], lambda i: (layer, 0, 0))],
        out_specs=[pl.BlockSpec((TM, w), lambda i: (i, 0)) for w in widths],
        out_shape=[jax.ShapeDtypeStruct((m, w), BF16) for w in widths],
        compiler_params=pltpu.CompilerParams(dimension_semantics=("arbitrary",),
                                             vmem_limit_bytes=VMEM_LIMIT),
        name="inproj",
    )(xf, g, wb16)


def _outproj_kernel(x_ref, mab_ref, mc_ref, md_ref, wo_ref, o_ref):
    acc = x_ref[...]
    acc = acc + jnp.dot(mab_ref[...], wo_ref[0:2 * GROUP_W, :], preferred_element_type=F32)
    acc = acc + jnp.dot(mc_ref[...], wo_ref[2 * GROUP_W:3 * GROUP_W, :], preferred_element_type=F32)
    acc = acc + jnp.dot(md_ref[...], wo_ref[3 * GROUP_W:4 * GROUP_W, :], preferred_element_type=F32)
    o_ref[...] = acc


def _outproj(xf, mab, mc, md, wo, layer):
    m, d = xf.shape
    return pl.pallas_call(
        _outproj_kernel,
        grid=(m // TM,),
        in_specs=[pl.BlockSpec((TM, d), lambda i: (i, 0)),
                  pl.BlockSpec((TM, 2 * GROUP_W), lambda i: (i, 0)),
                  pl.BlockSpec((TM, GROUP_W), lambda i: (i, 0)),
                  pl.BlockSpec((TM, GROUP_W), lambda i: (i, 0)),
                  pl.BlockSpec((None,) + wo.shape[1:], lambda i: (layer, 0, 0))],
        out_specs=pl.BlockSpec((TM, d), lambda i: (i, 0)),
        out_shape=jax.ShapeDtypeStruct((m, d), F32),
        compiler_params=pltpu.CompilerParams(dimension_semantics=("arbitrary",),
                                             vmem_limit_bytes=VMEM_LIMIT),
        name="outproj",
    )(xf, mab, mc, md, wo)


def _mixab_kernel(ya_ref, halo_ref, yb_ref, cw_ref, cb_ref, gg_ref, ws_ref, bfull_ref, p64_ref,
                  o_ref, u_scr):
    i = pl.program_id(1)
    W = GROUP_W
    h = ya_ref[:, 0:W].astype(F32)
    bg = ya_ref[:, W:2 * W].astype(F32)
    cg = ya_ref[:, 2 * W:3 * W].astype(F32)
    za = ya_ref[:, 3 * W:4 * W].astype(F32)
    uh = halo_ref[:, 2 * W:3 * W].astype(F32) * halo_ref[:, 0:W].astype(F32)
    u_scr[0:HALO, :] = jnp.where(i > 0, uh, 0.0)
    u_scr[HALO:HALO + TQ, :] = cg * h
    y = u_scr[HALO - 2:HALO - 2 + TQ, :] * cw_ref[0:1, :]
    y = y + u_scr[HALO - 1:HALO - 1 + TQ, :] * cw_ref[1:2, :]
    y = y + u_scr[HALO:HALO + TQ, :] * cw_ref[2:3, :]
    out_a = bg * (y + cb_ref[...]) * _silu(za)
    o_ref[:, 0:W] = out_a.astype(o_ref.dtype)

    u = _gelu(yb_ref[:, 0:W].astype(F32))
    v = _gelu(yb_ref[:, W:2 * W].astype(F32))
    zb = yb_ref[:, 2 * W:3 * W].astype(F32)
    vn = v * lax.rsqrt(_seg_mean(v * v, p64_ref[...]) + EPS) * gg_ref[...]
    t_idx = lax.broadcasted_iota(I32, (GMLP_BLOCK, GMLP_BLOCK), 0)
    s_idx = lax.broadcasted_iota(I32, (GMLP_BLOCK, GMLP_BLOCK), 1)
    causal = (t_idx >> 6) >= (s_idx >> 6)
    lane = lax.broadcasted_iota(I32, (1, W), 1)
    s = bfull_ref[...]
    for hd in range(4):
        wm = jnp.where(causal, ws_ref[hd], 0.0).astype(BF16)
        vh = jnp.where((lane >> 6) == hd, vn, 0.0).astype(BF16)
        s = s + jnp.dot(wm, vh, preferred_element_type=F32)
    out_b = u * s * _silu(zb)
    o_ref[:, W:2 * W] = out_b.astype(o_ref.dtype)


def _mixab(ya, yb, cw, cb, gg, ws, bfull, p64):
    b, t, _ = ya.shape
    nq = t // TQ
    return pl.pallas_call(
        _mixab_kernel,
        grid=(b, nq),
        in_specs=[pl.BlockSpec((None, TQ, 4 * GROUP_W), lambda bi, i: (bi, i, 0)),
                  pl.BlockSpec((None, HALO, 4 * GROUP_W),
                               lambda bi, i: (bi, jnp.maximum(i * (TQ // HALO) - 1, 0), 0)),
                  pl.BlockSpec((None, TQ, 3 * GROUP_W), lambda bi, i: (bi, i, 0)),
                  pl.BlockSpec(cw.shape, lambda bi, i: (0, 0)),
                  pl.BlockSpec(cb.shape, lambda bi, i: (0, 0)),
                  pl.BlockSpec(gg.shape, lambda bi, i: (0, 0)),
                  pl.BlockSpec(ws.shape, lambda bi, i: (0, 0, 0)),
                  pl.BlockSpec(bfull.shape, lambda bi, i: (0, 0)),
                  pl.BlockSpec(p64.shape, lambda bi, i: (0, 0))],
        out_specs=pl.BlockSpec((None, TQ, 2 * GROUP_W), lambda bi, i: (bi, i, 0)),
        out_shape=jax.ShapeDtypeStruct((b, t, 2 * GROUP_W), BF16),
        scratch_shapes=[pltpu.VMEM((HALO + TQ, GROUP_W), F32)],
        compiler_params=pltpu.CompilerParams(dimension_semantics=("arbitrary", "arbitrary"),
                                             vmem_limit_bytes=VMEM_LIMIT),
        name="mixab",
    )(ya, ya, yb, cw, cb, gg, ws, bfull, p64)


def _diff_kernel(q_ref, k_ref, v_ref, z_ref, qg_ref, kg_ref, lam_ref, subg_ref, p32_ref, p64_ref,
                 o_ref, kn_scr, qs_scr, s_scr, m_scr, l_scr, p_scr, acc_scr, *, lam_init, seq):
    TQ = TQ_DIFF
    i = pl.program_id(1)
    q0 = i * TQ
    nchunk = (q0 + TQ - 1) // KC + 1
    p32 = p32_ref[...]

    @pl.when(i == 0)
    def _():
        for c in range(seq // KC):
            kk = k_ref[c * KC:(c + 1) * KC, :].astype(F32)
            ms = _seg_mean(kk * kk, p32)
            kn_scr[c * KC:(c + 1) * KC, :] = (kk * lax.rsqrt(ms + EPS) * kg_ref[...]).astype(BF16)

    q = q_ref[...].astype(F32)
    qn = q * lax.rsqrt(_seg_mean(q * q, p32) + EPS) * (qg_ref[...] * (DIFF_QD ** -0.5 * LOG2E))
    lane = lax.broadcasted_iota(I32, (1, GROUP_W), 1)
    for hc in range(NHC):
        qs_scr[hc * TQ:(hc + 1) * TQ, :] = jnp.where((lane >> 5) == hc, qn, 0.0).astype(BF16)

    lp = lam_ref[...]
    lam = (jnp.exp(jnp.sum(lp[0:1] * lp[1:2], axis=-1, keepdims=True))
           - jnp.exp(jnp.sum(lp[2:3] * lp[3:4], axis=-1, keepdims=True)) + lam_init)

    m_scr[...] = jnp.full(m_scr.shape, -jnp.inf, F32)
    l_scr[...] = jnp.zeros(l_scr.shape, F32)
    acc_scr[...] = jnp.zeros(acc_scr.shape, F32)
    row = q0 + lax.broadcasted_iota(I32, (TQ, KC), 0)

    def s_body(c, carry):
        col = c * KC + lax.broadcasted_iota(I32, (TQ, KC), 1)
        dist = jnp.abs(row - col).astype(F32)
        dm = jnp.where((col >> 6) <= (row >> 6), dist, jnp.inf)
        kc = kn_scr[pl.ds(pl.multiple_of(c * KC, KC), KC), :]
        s_all = lax.dot_general(qs_scr[...], kc, (((1,), (1,)), ((), ())), preferred_element_type=F32)
        for h in range(NHC // 2):
            bias = (SLOPES_C[h] * LOG2E) * dm
            for j in range(2):
                rows = slice((2 * h + j) * TQ, (2 * h + j + 1) * TQ)
                s = s_all[rows, :] - bias
                s_scr[c, rows, :] = s
                m_scr[rows, :] = jnp.maximum(m_scr[rows, :], _fold_lanes(s, jnp.maximum))
        return carry

    lax.fori_loop(0, nchunk, s_body, 0)

    for hc in range(NHC):
        m = jnp.max(m_scr[hc * TQ:(hc + 1) * TQ, :], axis=1, keepdims=True)
        m_scr[hc * TQ:(hc + 1) * TQ, :] = jnp.broadcast_to(m, (TQ, LANES))

    def e_body(c, carry):
        for hc in range(NHC):
            rows = slice(hc * TQ, (hc + 1) * TQ)
            m = m_scr[rows, :]
            s = s_scr[c, rows, :]
            ps = [jnp.exp2(s[:, k * LANES:(k + 1) * LANES] - m) for k in range(KC // LANES)]
            l_scr[rows, :] += (ps[0] + ps[1]) + (ps[2] + ps[3])
            p_scr[rows, :] = jnp.concatenate(ps, axis=1).astype(BF16)
        vc = v_ref[pl.ds(pl.multiple_of(c * KC, KC), KC), :]
        acc_scr[...] += jnp.dot(p_scr[...], vc, preferred_element_type=F32)
        return carry

    lax.fori_loop(0, nchunk, e_body, 0)

    o = jnp.zeros((TQ, GROUP_W), F32)
    for h in range(NHC // 2):
        r1 = slice(2 * h * TQ, (2 * h + 1) * TQ)
        r2 = slice((2 * h + 1) * TQ, (2 * h + 2) * TQ)
        l1 = jnp.sum(l_scr[r1, :], axis=1, keepdims=True)
        l2 = jnp.sum(l_scr[r2, :], axis=1, keepdims=True)
        o_h = acc_scr[r1, :] * (1.0 / l1) - acc_scr[r2, :] * (lam / l2)
        o = jnp.where((lane >> 6) == h, o_h, o)

    ms = _seg_mean(o * o, p64_ref[...])
    o = o * lax.rsqrt(ms + EPS) * (subg_ref[...] * (1.0 - lam_init))
    z = z_ref[...].astype(F32)
    o_ref[...] = (o * _silu(z)).astype(o_ref.dtype)


def _diff(yc, qg, kg, lam_p, subg, p32, p64, lam_init):
    TQ = TQ_DIFF
    b, t, _ = yc.shape
    nq = t // TQ
    nkc = t // KC
    W = GROUP_W
    kern = functools.partial(_diff_kernel, lam_init=lam_init, seq=t)
    small = lambda a: pl.BlockSpec(a.shape, lambda bi, i: (0,) * a.ndim)
    return pl.pallas_call(
        kern,
        grid=(b, nq),
        in_specs=[pl.BlockSpec((None, TQ, W), lambda bi, i: (bi, i, 0)),
                  pl.BlockSpec((None, t, W), lambda bi, i: (bi, 0, 1)),
                  pl.BlockSpec((None, t, W), lambda bi, i: (bi, 0, 2)),
                  pl.BlockSpec((None, TQ, W), lambda bi, i: (bi, i, 3)),
                  small(qg), small(kg), small(lam_p), small(subg), small(p32), small(p64)],
        out_specs=pl.BlockSpec((None, TQ, W), lambda bi, i: (bi, i, 0)),
        out_shape=jax.ShapeDtypeStruct((b, t, W), BF16),
        scratch_shapes=[pltpu.VMEM((t, W), BF16),
                        pltpu.VMEM((NHC * TQ, W), BF16),
                        pltpu.VMEM((nkc, NHC * TQ, KC), F32),
                        pltpu.VMEM((NHC * TQ, LANES), F32),
                        pltpu.VMEM((NHC * TQ, LANES), F32),
                        pltpu.VMEM((NHC * TQ, KC), BF16),
                        pltpu.VMEM((NHC * TQ, W), F32)],
        compiler_params=pltpu.CompilerParams(dimension_semantics=("arbitrary", "arbitrary"),
                                             vmem_limit_bytes=VMEM_LIMIT),
        name="diffattn",
    )(yc, yc, yc, yc, qg, kg, lam_p, subg, p32, p64)


def _dsa_kernel(q_ref, z_ref, iq_ref, ikwq_ref, kv_ref, ikw_ref, qg_ref, kg_ref, p64_ref, p64h_ref,
                o_ref, knv_scr, vt_scr, key_scr, tk_scr, dm_scr, *, seq, topk):
    i = pl.program_id(1)
    q0 = i * TQ
    nchunk = q0 // KC + 1

    @pl.when(i == 0)
    def _():
        p64h = p64h_ref[...]
        for c in range(seq // 128):
            blk = kv_ref[c * 128:(c + 1) * 128, :].astype(F32)
            ms = _seg_mean(blk * blk, p64h)
            knv_scr[c * 128:(c + 1) * 128, :] = (blk * lax.rsqrt(ms + EPS) * kg_ref[...]).astype(BF16)
            vt = blk.T
            cc, off = divmod(c * 128, KC)
            vt_scr[cc, :, off:off + 128] = vt[DSA_HD:2 * DSA_HD, :].astype(BF16)

    iq_t = iq_ref[...].astype(F32).T
    iw_t = ikwq_ref[...].astype(F32).T[IDX_HD:IDX_HD + 8, :]
    wq = iw_t * (IDX_HEADS ** -0.5 * IDX_HD ** -0.5)
    zpad_i = jnp.zeros((128 - IDX_HD, TQ), F32)
    rhs_idx = jnp.concatenate(
        [jnp.concatenate([iq_t[IDX_HD * h:IDX_HD * (h + 1), :], zpad_i], axis=0) for h in range(IDX_HEADS)],
        axis=1).astype(BF16)

    q = q_ref[...].astype(F32)
    qn = q * lax.rsqrt(_seg_mean(q * q, p64_ref[...]) + EPS) * (qg_ref[...] * (DSA_HD ** -0.5 * LOG2E))
    qn_t = qn.T
    zpad_q = jnp.zeros((128 - DSA_HD, TQ), F32)
    rhs_main = jnp.concatenate(
        [jnp.concatenate([qn_t[DSA_HD * h:DSA_HD * (h + 1), :], zpad_q], axis=0) for h in range(4)],
        axis=1).astype(BF16)

    _dsa_tile(nchunk, q0, rhs_idx, wq, rhs_main, z_ref, ikw_ref, o_ref, knv_scr, vt_scr, key_scr,
              tk_scr, dm_scr, seq, topk)


def _dsa_tile(nchunk, q0, rhs_idx, wq, rhs_main, z_ref, ikw_ref, o_ref, knv_scr, vt_scr, key_scr,
              tk_scr, dm_scr, seq, topk):
    SB = 256
    n_sb = KC // SB

    def over_chunks(body, init):
        return lax.fori_loop(0, nchunk, body, init)

    qpos = q0 + lax.broadcasted_iota(I32, (SB, TQ), 1)

    def idx_body(c, carry):
        for sb in range(n_sb):
            r0 = pl.multiple_of(c * KC + sb * SB, SB)
            logit = jnp.dot(ikw_ref[pl.ds(r0, SB), :], rhs_idx, preferred_element_type=F32)
            sc = ((jnp.maximum(logit[:, 0:TQ], 0.0) * wq[0:1, :]
                   + jnp.maximum(logit[:, TQ:2 * TQ], 0.0) * wq[1:2, :])
                  + (jnp.maximum(logit[:, 2 * TQ:3 * TQ], 0.0) * wq[2:3, :]
                     + jnp.maximum(logit[:, 3 * TQ:4 * TQ], 0.0) * wq[3:4, :]))
            bits = lax.bitcast_convert_type(sc, I32)
            key = bits ^ ((bits >> 31) & jnp.int32(0x7FFFFFFF))
            key = jnp.where(key == -1, 0, key)
            krow = r0 + lax.broadcasted_iota(I32, (SB, TQ), 0)
            allowed = (krow >> 6) <= (qpos >> 6)
            key_scr[pl.ds(r0, SB), :] = jnp.where(allowed, key, jnp.int32(INT_MIN))
        return carry

    over_chunks(idx_body, 0)

    kf = jnp.float32(topk)
    far = jnp.int32(2 ** 30)

    def count(n, scr, pred):
        acc = jnp.zeros((SUBLANES, TQ), F32)
        for c in range(n):
            acc = acc + _fold_rows(jnp.where(pred(scr[c * KC:(c + 1) * KC, :]), 1.0, 0.0), jnp.add)
        return jnp.sum(acc, axis=0, keepdims=True)

    def bit_search(nbits, step):
        def variant(n):
            def run():
                return lax.fori_loop(0, nbits, lambda it, u: step(n, it, u), jnp.zeros((1, TQ), I32))
            return run
        return lax.switch(nchunk - 1, [variant(n) for n in range(1, seq // KC + 1)])

    def tau_step(n, it, u):
        cand_u = u | jnp.left_shift(jnp.int32(1), 31 - it)
        cand = cand_u ^ jnp.int32(INT_MIN)
        cnt = count(n, key_scr, lambda blk: blk >= cand)
        return jnp.where(cnt >= kf, cand_u, u)

    tau = bit_search(32, tau_step) ^ jnp.int32(INT_MIN)

    def tie_body(c, acc):
        r0 = pl.multiple_of(c * KC, KC)
        key = key_scr[pl.ds(r0, KC), :]
        krow = r0 + lax.broadcasted_iota(I32, (KC, TQ), 0)
        tk_scr[pl.ds(r0, KC), :] = jnp.where(key == tau, krow, far)
        return acc + _fold_rows(jnp.where(key > tau, 1.0, 0.0), jnp.add)

    n_gt = over_chunks(tie_body, jnp.zeros((SUBLANES, TQ), F32))
    need = kf - jnp.sum(n_gt, axis=0, keepdims=True)
    nbits = int(seq).bit_length()

    def jbit_step(n, it, jj):
        cand = jj | jnp.left_shift(jnp.int32(1), nbits - 1 - it)
        cnt = count(n, tk_scr, lambda blk: blk < cand)
        return jnp.where(cnt <= need, cand, jj)

    jsel = bit_search(nbits, jbit_step)
    jsel = jnp.where(tau == jnp.int32(INT_MIN), 0, jsel)

    def dm_body(c, carry):
        r0 = pl.multiple_of(c * KC, KC)
        krow = r0 + lax.broadcasted_iota(I32, (KC, TQ), 0)
        qp = q0 + lax.broadcasted_iota(I32, (KC, TQ), 1)
        dist = jnp.abs(qp - krow).astype(F32)
        inner = jnp.where(tk_scr[pl.ds(r0, KC), :] < jsel, dist, jnp.inf)
        dm_scr[pl.ds(r0, KC), :] = jnp.where(key_scr[pl.ds(r0, KC), :] > tau, dist, inner)
        return carry

    over_chunks(dm_body, 0)

    slopes = [s * LOG2E for s in SLOPES_D]

    def attn_body(c, carry):
        ms, ls, acc = list(carry[:4]), list(carry[4:8]), carry[8]
        atts = [jnp.dot(knv_scr[pl.ds(pl.multiple_of(c * KC + sb * SB, SB), SB), :], rhs_main,
                        preferred_element_type=F32) for sb in range(n_sb)]
        for sb in range(n_sb):
            r0 = pl.multiple_of(c * KC + sb * SB, SB)
            att = atts[sb]
            dm = dm_scr[pl.ds(r0, SB), :]
            alphas, probs = [], []
            for h in range(4):
                a = att[:, h * TQ:(h + 1) * TQ] - slopes[h] * dm
                m_new = jnp.maximum(ms[h], jnp.max(_fold_rows(a, jnp.maximum), axis=0, keepdims=True))
                m_use = jnp.where(m_new == -jnp.inf, 0.0, m_new)
                alpha = jnp.exp2(ms[h] - m_use)
                p = jnp.exp2(a - m_use)
                ls[h] = ls[h] * alpha + jnp.sum(_fold_rows(p, jnp.add), axis=0, keepdims=True)
                ms[h] = m_new
                alphas.append(alpha)
                probs.append(p.astype(BF16))
            pv = jnp.dot(vt_scr[c, :, sb * SB:(sb + 1) * SB], jnp.concatenate(probs, axis=1),
                         preferred_element_type=F32)
            acc = acc * jnp.concatenate(alphas, axis=1) + pv
        return (*ms, *ls, acc)

    neg = jnp.full((1, TQ), -jnp.inf, F32)
    zero = jnp.zeros((1, TQ), F32)
    res = over_chunks(attn_body, (neg,) * 4 + (zero,) * 4 + (jnp.zeros((DSA_HD, 4 * TQ), F32),))
    ls = res[4:8]
    out_t = res[8]
    o_t = jnp.concatenate([out_t[:, h * TQ:(h + 1) * TQ] * (1.0 / ls[h]) for h in range(4)], axis=0)
    o = o_t.T
    z = z_ref[...].astype(F32)
    o_ref[...] = (o * _silu(z)).astype(o_ref.dtype)


def _dsa(yd, qg, kg, p64, p64h):
    b, t, _ = yd.shape
    nq = t // TQ
    nkc = t // KC
    W = GROUP_W
    topk = min(DSA_TOPK_MAX, t // 4)
    kern = functools.partial(_dsa_kernel, seq=t, topk=topk)
    small = lambda a: pl.BlockSpec(a.shape, lambda bi, i: (0,) * a.ndim)
    return pl.pallas_call(
        kern,
        grid=(b, nq),
        in_specs=[pl.BlockSpec((None, TQ, W), lambda bi, i: (bi, i, 0)),
                  pl.BlockSpec((None, TQ, W), lambda bi, i: (bi, i, 1)),
                  pl.BlockSpec((None, TQ, 128), lambda bi, i: (bi, i, 5)),
                  pl.BlockSpec((None, TQ, 128), lambda bi, i: (bi, i, 6)),
                  pl.BlockSpec((None, t, 128), lambda bi, i: (bi, 0, 4)),
                  pl.BlockSpec((None, t, 128), lambda bi, i: (bi, 0, 6)),
                  small(qg), small(kg), small(p64), small(p64h)],
        out_specs=pl.BlockSpec((None, TQ, W), lambda bi, i: (bi, i, 0)),
        out_shape=jax.ShapeDtypeStruct((b, t, W), BF16),
        scratch_shapes=[pltpu.VMEM((t, 128), BF16),
                        pltpu.VMEM((nkc, DSA_HD, KC), BF16),
                        pltpu.VMEM((t, TQ), I32),
                        pltpu.VMEM((t, TQ), I32),
                        pltpu.VMEM((t, TQ), F32)],
        compiler_params=pltpu.CompilerParams(dimension_semantics=("arbitrary", "arbitrary"),
                                             vmem_limit_bytes=VMEM_LIMIT),
        name="dsa",
    )(yd, yd, yd, yd, yd, yd, qg, kg, p64, p64h)


def kernel(x, norm_g, w_in, conv_w, conv_b, gmlp_g, gmlp_ws, gmlp_b, diff_qg, diff_kg, diff_lam,
           diff_subg, dsa_qg, dsa_kg, w_out):
    b, t, d = x.shape
    depth = w_in.shape[0]
    p32 = _block_diag_mean(GROUP_W, 32)
    p64 = _block_diag_mean(GROUP_W, 64)
    p64h = _block_diag_mean(128, 64)
    w_in16 = _wprep(w_in)
    w_out16 = _wprep(w_out)
    xf = x.reshape(b * t, d)
    for l in range(depth):
        ya, yb, yc, yd = _inproj(xf, norm_g[l].reshape(1, d), w_in16, l)
        ya, yb, yc, yd = (a.reshape(b, t, a.shape[-1]) for a in (ya, yb, yc, yd))
        bfull = jnp.repeat(gmlp_b[l].T, GROUP_W // 4, axis=1)
        mab = _mixab(ya, yb, conv_w[l], conv_b[l].reshape(1, -1), gmlp_g[l].reshape(1, -1),
                     gmlp_ws[l], bfull, p64)
        lam_init = 0.8 - 0.6 * math.exp(-0.3 * l)
        mc = _diff(yc, jnp.tile(diff_qg[l], 8).reshape(1, -1), jnp.tile(diff_kg[l], 8).reshape(1, -1),
                   diff_lam[l], diff_subg[l].reshape(1, -1), p32, p64, lam_init)
        md = _dsa(yd, jnp.tile(dsa_qg[l], 4).reshape(1, -1), jnp.tile(dsa_kg[l], 2).reshape(1, -1),
                  p64, p64h)
        xf = _outproj(xf, mab.reshape(b * t, -1), mc.reshape(b * t, -1), md.reshape(b * t, -1),
                      w_out16, l)
    return xf.reshape(b, t, d)
```

```python
import functools
import math

import numpy as np
import jax
import jax.numpy as jnp
from jax import lax
from jax.experimental import pallas as pl
from jax.experimental.pallas import tpu as pltpu

F32 = jnp.float32
BF16 = jnp.bfloat16
I32 = jnp.int32

GROUP_W = 256
CHUNK = 64
CONV_W = 3
GMLP_BLOCK = 128
DIFF_QD = 32
DSA_HD = 64
IDX_HD = 32
IDX_HEADS = 4
DSA_TOPK_MAX = 256
EPS = 1e-6
LOG2E = math.log2(math.e)
INT_MIN = -2 ** 31
LANES = 128
SUBLANES = 8

TQ = 128
TQ_DIFF = 256
DSA_SUBTILES = 4
MIX_SUBTILES = 4
KC = 512
TM = 512
HALO = 16
NHC = 8

_SLOPES = 2.0 ** (-8.0 * np.arange(1, 9) / 8.0)
SLOPES_C = [float(s) for s in _SLOPES[0::2]]
SLOPES_D = [float(s) for s in _SLOPES[1::2]]

VMEM_LIMIT = 56 * 1024 * 1024


def _block_diag_mean(width, seg):
    idx = np.arange(width) // seg
    return jnp.asarray((idx[:, None] == idx[None, :]).astype(np.float32) / seg, dtype=BF16)


def _seg_mean(x2, p):
    hi = x2.astype(BF16)
    lo = (x2 - hi.astype(F32)).astype(BF16)
    return (jnp.dot(hi, p, preferred_element_type=F32)
            + jnp.dot(lo, p, preferred_element_type=F32))


def _silu(z):
    return z * jax.nn.sigmoid(z)


def _gelu(x):
    return 0.5 * x * (1.0 + lax.erf(x * (2.0 ** -0.5)))


def _fold_rows(x, op, stop=SUBLANES):
    r = x.shape[0]
    while r > stop:
        r //= 2
        x = op(x[:r], x[r:])
    return x


def _fold_lanes(x, op):
    c = x.shape[1]
    while c > LANES:
        c //= 2
        x = op(x[:, :c], x[:, c:])
    return x


def _wprep_kernel(w_ref, o_ref, *, n_valid):
    col = pl.program_id(1) * LANES + lax.broadcasted_iota(I32, (1, LANES), 1)
    o_ref[...] = jnp.where(col < n_valid, w_ref[...], 0.0).astype(BF16)


def _wprep(w):
    depth, d, n = w.shape
    nt = pl.cdiv(n, LANES)
    return pl.pallas_call(
        functools.partial(_wprep_kernel, n_valid=n),
        grid=(depth, nt),
        in_specs=[pl.BlockSpec((None, d, LANES), lambda l, j: (l, 0, j))],
        out_specs=pl.BlockSpec((None, d, LANES), lambda l, j: (l, 0, j)),
        out_shape=jax.ShapeDtypeStruct((depth, d, nt * LANES), BF16),
        compiler_params=pltpu.CompilerParams(dimension_semantics=("arbitrary", "arbitrary")),
        name="wprep",
    )(w)


_A0, _B0, _C0, _D0 = 0, 4 * GROUP_W, 7 * GROUP_W, 11 * GROUP_W
_DQ, _DKV, _DZ, _DIQ, _DEND = _D0, _D0 + 256, _D0 + 384, _D0 + 640, _D0 + 896
YD_W = 896


def _inproj_kernel(x_ref, g_ref, w_ref, ya_ref, yb_ref, yc_ref, yd_ref):
    x = x_ref[...]
    ms = jnp.mean(x * x, axis=-1, keepdims=True)
    xn = (x * lax.rsqrt(ms + EPS) * g_ref[...]).astype(BF16)

    def proj(lo, hi):
        return jnp.dot(xn, w_ref[:, lo:hi], preferred_element_type=F32).astype(BF16)

    ya_ref[...] = proj(_A0, _B0)
    yb_ref[...] = proj(_B0, _C0)
    yc_ref[...] = proj(_C0, _D0)
    yd_ref[:, 0:256] = proj(_DQ, _DKV)
    yd_ref[:, 256:512] = proj(_DZ, _DIQ)
    yd_ref[:, 512:640] = proj(_DKV, _DZ)
    yd_ref[:, 640:896] = proj(_DIQ, _DEND)


def _inproj(xf, g, wb16, layer):
    m, d = xf.shape
    widths = (_B0 - _A0, _C0 - _B0, _D0 - _C0, YD_W)
    return pl.pallas_call(
        _inproj_kernel,
        grid=(m // TM,),
        in_specs=[pl.BlockSpec((TM, d), lambda i: (i, 0)),
                  pl.BlockSpec((1, d), lambda i: (0, 0)),
                  pl.BlockSpec((None,) + wb16.shape[1:], lambda i: (layer, 0, 0))],
        out_specs=[pl.BlockSpec((TM, w), lambda i: (i, 0)) for w in widths],
        out_shape=[jax.ShapeDtypeStruct((m, w), BF16) for w in widths],
        compiler_params=pltpu.CompilerParams(dimension_semantics=("arbitrary",),
                                             vmem_limit_bytes=VMEM_LIMIT),
        name="inproj",
    )(xf, g, wb16)


def _outproj_kernel(x_ref, mab_ref, mc_ref, md_ref, wo_ref, o_ref):
    acc = x_ref[...]
    acc = acc + jnp.dot(mab_ref[...], wo_ref[0:2 * GROUP_W, :], preferred_element_type=F32)
    acc = acc + jnp.dot(mc_ref[...], wo_ref[2 * GROUP_W:3 * GROUP_W, :], preferred_element_type=F32)
    acc = acc + jnp.dot(md_ref[...], wo_ref[3 * GROUP_W:4 * GROUP_W, :], preferred_element_type=F32)
    o_ref[...] = acc


def _outproj(xf, mab, mc, md, wo, layer):
    m, d = xf.shape
    return pl.pallas_call(
        _outproj_kernel,
        grid=(m // TM,),
        in_specs=[pl.BlockSpec((TM, d), lambda i: (i, 0)),
                  pl.BlockSpec((TM, 2 * GROUP_W), lambda i: (i, 0)),
                  pl.BlockSpec((TM, GROUP_W), lambda i: (i, 0)),
                  pl.BlockSpec((TM, GROUP_W), lambda i: (i, 0)),
                  pl.BlockSpec((None,) + wo.shape[1:], lambda i: (layer, 0, 0))],
        out_specs=pl.BlockSpec((TM, d), lambda i: (i, 0)),
        out_shape=jax.ShapeDtypeStruct((m, d), F32),
        compiler_params=pltpu.CompilerParams(dimension_semantics=("arbitrary",),
                                             vmem_limit_bytes=VMEM_LIMIT),
        name="outproj",
    )(xf, mab, mc, md, wo)


def _mixab_kernel(ya_ref, halo_ref, yb_ref, cw_ref, cb_ref, gg_ref, ws_ref, bfull_ref, p64_ref,
                  o_ref, u_scr):
    i = pl.program_id(1)
    W = GROUP_W
    TB = TQ * MIX_SUBTILES
    uh = halo_ref[:, 2 * W:3 * W].astype(F32) * halo_ref[:, 0:W].astype(F32)
    u_scr[0:HALO, :] = jnp.where(i > 0, uh, 0.0)
    u_scr[HALO:HALO + TB, :] = ya_ref[:, 2 * W:3 * W].astype(F32) * ya_ref[:, 0:W].astype(F32)
    for sb in range(MIX_SUBTILES):
        r0 = HALO + sb * TQ
        y = u_scr[r0 - 2:r0 - 2 + TQ, :] * cw_ref[0:1, :]
        y = y + u_scr[r0 - 1:r0 - 1 + TQ, :] * cw_ref[1:2, :]
        y = y + u_scr[r0:r0 + TQ, :] * cw_ref[2:3, :]
        rows = slice(sb * TQ, (sb + 1) * TQ)
        bg = ya_ref[rows, W:2 * W].astype(F32)
        za = ya_ref[rows, 3 * W:4 * W].astype(F32)
        o_ref[rows, 0:W] = (bg * (y + cb_ref[...]) * _silu(za)).astype(o_ref.dtype)

    t_idx = lax.broadcasted_iota(I32, (GMLP_BLOCK, GMLP_BLOCK), 0)
    s_idx = lax.broadcasted_iota(I32, (GMLP_BLOCK, GMLP_BLOCK), 1)
    causal = (t_idx >> 6) >= (s_idx >> 6)
    lane = lax.broadcasted_iota(I32, (1, W), 1)
    wms = [jnp.where(causal, ws_ref[hd], 0.0).astype(BF16) for hd in range(4)]

    def block(sb, carry):
        rows = pl.ds(pl.multiple_of(sb * TQ, TQ), TQ)
        u = _gelu(yb_ref[rows, 0:W].astype(F32))
        v = _gelu(yb_ref[rows, W:2 * W].astype(F32))
        zb = yb_ref[rows, 2 * W:3 * W].astype(F32)
        vn = v * lax.rsqrt(_seg_mean(v * v, p64_ref[...]) + EPS) * gg_ref[...]
        s = bfull_ref[...]
        for hd in range(4):
            vh = jnp.where((lane >> 6) == hd, vn, 0.0).astype(BF16)
            s = s + jnp.dot(wms[hd], vh, preferred_element_type=F32)
        o_ref[rows, W:2 * W] = (u * s * _silu(zb)).astype(o_ref.dtype)
        return carry

    lax.fori_loop(0, MIX_SUBTILES, block, 0)


def _mixab(ya, yb, cw, cb, gg, ws, bfull, p64):
    b, t, _ = ya.shape
    TB = TQ * MIX_SUBTILES
    nq = t // TB
    return pl.pallas_call(
        _mixab_kernel,
        grid=(b, nq),
        in_specs=[pl.BlockSpec((None, TB, 4 * GROUP_W), lambda bi, i: (bi, i, 0)),
                  pl.BlockSpec((None, HALO, 4 * GROUP_W),
                               lambda bi, i: (bi, jnp.maximum(i * (TB // HALO) - 1, 0), 0)),
                  pl.BlockSpec((None, TB, 3 * GROUP_W), lambda bi, i: (bi, i, 0)),
                  pl.BlockSpec(cw.shape, lambda bi, i: (0, 0)),
                  pl.BlockSpec(cb.shape, lambda bi, i: (0, 0)),
                  pl.BlockSpec(gg.shape, lambda bi, i: (0, 0)),
                  pl.BlockSpec(ws.shape, lambda bi, i: (0, 0, 0)),
                  pl.BlockSpec(bfull.shape, lambda bi, i: (0, 0)),
                  pl.BlockSpec(p64.shape, lambda bi, i: (0, 0))],
        out_specs=pl.BlockSpec((None, TB, 2 * GROUP_W), lambda bi, i: (bi, i, 0)),
        out_shape=jax.ShapeDtypeStruct((b, t, 2 * GROUP_W), BF16),
        scratch_shapes=[pltpu.VMEM((HALO + TB, GROUP_W), F32)],
        compiler_params=pltpu.CompilerParams(dimension_semantics=("arbitrary", "arbitrary"),
                                             vmem_limit_bytes=VMEM_LIMIT),
        name="mixab",
    )(ya, ya, yb, cw, cb, gg, ws, bfull, p64)


def _diff_kernel(q_ref, k_ref, v_ref, z_ref, qg_ref, kg_ref, lam_ref, subg_ref, p32_ref, p64_ref,
                 o_ref, kn_scr, qs_scr, s_scr, m_scr, l_scr, p_scr, acc_scr, *, lam_init, seq):
    TQ = TQ_DIFF
    i = pl.program_id(1)
    q0 = i * TQ
    nchunk = (q0 + TQ - 1) // KC + 1
    p32 = p32_ref[...]

    @pl.when(i == 0)
    def _():
        for c in range(seq // KC):
            kk = k_ref[c * KC:(c + 1) * KC, :].astype(F32)
            ms = _seg_mean(kk * kk, p32)
            kn_scr[c * KC:(c + 1) * KC, :] = (kk * lax.rsqrt(ms + EPS) * kg_ref[...]).astype(BF16)

    q = q_ref[...].astype(F32)
    qn = q * lax.rsqrt(_seg_mean(q * q, p32) + EPS) * (qg_ref[...] * (DIFF_QD ** -0.5 * LOG2E))
    lane = lax.broadcasted_iota(I32, (1, GROUP_W), 1)
    for hc in range(NHC):
        qs_scr[hc * TQ:(hc + 1) * TQ, :] = jnp.where((lane >> 5) == hc, qn, 0.0).astype(BF16)

    lp = lam_ref[...]
    lam = (jnp.exp(jnp.sum(lp[0:1] * lp[1:2], axis=-1, keepdims=True))
           - jnp.exp(jnp.sum(lp[2:3] * lp[3:4], axis=-1, keepdims=True)) + lam_init)

    m_scr[...] = jnp.full(m_scr.shape, -jnp.inf, F32)
    l_scr[...] = jnp.zeros(l_scr.shape, F32)
    acc_scr[...] = jnp.zeros(acc_scr.shape, F32)
    row = q0 + lax.broadcasted_iota(I32, (TQ, KC), 0)

    def s_body(c, carry):
        col = c * KC + lax.broadcasted_iota(I32, (TQ, KC), 1)
        dist = jnp.abs(row - col).astype(F32)
        dm = jnp.where((col >> 6) <= (row >> 6), dist, jnp.inf)
        kc = kn_scr[pl.ds(pl.multiple_of(c * KC, KC), KC), :]
        s_all = lax.dot_general(qs_scr[...], kc, (((1,), (1,)), ((), ())), preferred_element_type=F32)
        for h in range(NHC // 2):
            bias = (SLOPES_C[h] * LOG2E) * dm
            for j in range(2):
                rows = slice((2 * h + j) * TQ, (2 * h + j + 1) * TQ)
                s = s_all[rows, :] - bias
                s_scr[c, rows, :] = s
                m_scr[rows, :] = jnp.maximum(m_scr[rows, :], _fold_lanes(s, jnp.maximum))
        return carry

    lax.fori_loop(0, nchunk, s_body, 0)

    for hc in range(NHC):
        m = jnp.max(m_scr[hc * TQ:(hc + 1) * TQ, :], axis=1, keepdims=True)
        m_scr[hc * TQ:(hc + 1) * TQ, :] = jnp.broadcast_to(m, (TQ, LANES))

    def e_body(c, carry):
        for hc in range(NHC):
            rows = slice(hc * TQ, (hc + 1) * TQ)
            m = m_scr[rows, :]
            s = s_scr[c, rows, :]
            ps = [jnp.exp2(s[:, k * LANES:(k + 1) * LANES] - m) for k in range(KC // LANES)]
            l_scr[rows, :] += (ps[0] + ps[1]) + (ps[2] + ps[3])
            p_scr[rows, :] = jnp.concatenate(ps, axis=1).astype(BF16)
        vc = v_ref[pl.ds(pl.multiple_of(c * KC, KC), KC), :]
        acc_scr[...] += jnp.dot(p_scr[...], vc, preferred_element_type=F32)
        return carry

    lax.fori_loop(0, nchunk, e_body, 0)

    o = jnp.zeros((TQ, GROUP_W), F32)
    for h in range(NHC // 2):
        r1 = slice(2 * h * TQ, (2 * h + 1) * TQ)
        r2 = slice((2 * h + 1) * TQ, (2 * h + 2) * TQ)
        l1 = jnp.sum(l_scr[r1, :], axis=1, keepdims=True)
        l2 = jnp.sum(l_scr[r2, :], axis=1, keepdims=True)
        o_h = acc_scr[r1, :] * (1.0 / l1) - acc_scr[r2, :] * (lam / l2)
        o = jnp.where((lane >> 6) == h, o_h, o)

    ms = _seg_mean(o * o, p64_ref[...])
    o = o * lax.rsqrt(ms + EPS) * (subg_ref[...] * (1.0 - lam_init))
    z = z_ref[...].astype(F32)
    o_ref[...] = (o * _silu(z)).astype(o_ref.dtype)


def _diff(yc, qg, kg, lam_p, subg, p32, p64, lam_init):
    TQ = TQ_DIFF
    b, t, _ = yc.shape
    nq = t // TQ
    nkc = t // KC
    W = GROUP_W
    kern = functools.partial(_diff_kernel, lam_init=lam_init, seq=t)
    small = lambda a: pl.BlockSpec(a.shape, lambda bi, i: (0,) * a.ndim)
    return pl.pallas_call(
        kern,
        grid=(b, nq),
        in_specs=[pl.BlockSpec((None, TQ, W), lambda bi, i: (bi, i, 0)),
                  pl.BlockSpec((None, t, W), lambda bi, i: (bi, 0, 1)),
                  pl.BlockSpec((None, t, W), lambda bi, i: (bi, 0, 2)),
                  pl.BlockSpec((None, TQ, W), lambda bi, i: (bi, i, 3)),
                  small(qg), small(kg), small(lam_p), small(subg), small(p32), small(p64)],
        out_specs=pl.BlockSpec((None, TQ, W), lambda bi, i: (bi, i, 0)),
        out_shape=jax.ShapeDtypeStruct((b, t, W), BF16),
        scratch_shapes=[pltpu.VMEM((t, W), BF16),
                        pltpu.VMEM((NHC * TQ, W), BF16),
                        pltpu.VMEM((nkc, NHC * TQ, KC), F32),
                        pltpu.VMEM((NHC * TQ, LANES), F32),
                        pltpu.VMEM((NHC * TQ, LANES), F32),
                        pltpu.VMEM((NHC * TQ, KC), BF16),
                        pltpu.VMEM((NHC * TQ, W), F32)],
        compiler_params=pltpu.CompilerParams(dimension_semantics=("arbitrary", "arbitrary"),
                                             vmem_limit_bytes=VMEM_LIMIT),
        name="diffattn",
    )(yc, yc, yc, yc, qg, kg, lam_p, subg, p32, p64)


def _dsa_kernel(q_ref, z_ref, iq_ref, ikwq_ref, kv_ref, ikw_ref, qg_ref, kg_ref, p64_ref, p64h_ref,
                o_ref, knv_scr, vt_scr, key_scr, tk_scr, dm_scr, *, seq, topk):
    i = pl.program_id(1)

    @pl.when(i == 0)
    def _():
        p64h = p64h_ref[...]
        for c in range(seq // 128):
            blk = kv_ref[c * 128:(c + 1) * 128, :].astype(F32)
            ms = _seg_mean(blk * blk, p64h)
            knv_scr[c * 128:(c + 1) * 128, :] = (blk * lax.rsqrt(ms + EPS) * kg_ref[...]).astype(BF16)
            vt = blk.T
            cc, off = divmod(c * 128, KC)
            vt_scr[cc, :, off:off + 128] = vt[DSA_HD:2 * DSA_HD, :].astype(BF16)

    def sub_tile(sub, carry):
        rows = pl.ds(pl.multiple_of(sub * TQ, TQ), TQ)
        q0 = (i * DSA_SUBTILES + sub) * TQ
        nchunk = q0 // KC + 1

        iq_t = iq_ref[rows, :].astype(F32).T
        iw_t = ikwq_ref[rows, :].astype(F32).T[IDX_HD:IDX_HD + 8, :]
        wq = iw_t * (IDX_HEADS ** -0.5 * IDX_HD ** -0.5)
        zpad_i = jnp.zeros((128 - IDX_HD, TQ), F32)
        rhs_idx = jnp.concatenate(
            [jnp.concatenate([iq_t[IDX_HD * h:IDX_HD * (h + 1), :], zpad_i], axis=0)
             for h in range(IDX_HEADS)], axis=1).astype(BF16)

        q = q_ref[rows, :].astype(F32)
        qn = q * lax.rsqrt(_seg_mean(q * q, p64_ref[...]) + EPS) * (qg_ref[...] * (DSA_HD ** -0.5 * LOG2E))
        qn_t = qn.T
        zpad_q = jnp.zeros((128 - DSA_HD, TQ), F32)
        rhs_main = jnp.concatenate(
            [jnp.concatenate([qn_t[DSA_HD * h:DSA_HD * (h + 1), :], zpad_q], axis=0) for h in range(4)],
            axis=1).astype(BF16)

        o = _dsa_tile(nchunk, q0, rhs_idx, wq, rhs_main, ikw_ref, knv_scr, vt_scr, key_scr,
                      tk_scr, dm_scr, seq, topk)
        z = z_ref[rows, :].astype(F32)
        o_ref[rows, :] = (o * _silu(z)).astype(o_ref.dtype)
        return carry

    lax.fori_loop(0, DSA_SUBTILES, sub_tile, 0)


def _dsa_tile(nchunk, q0, rhs_idx, wq, rhs_main, ikw_ref, knv_scr, vt_scr, key_scr,
              tk_scr, dm_scr, seq, topk):
    SB = 256
    n_sb = KC // SB

    def over_chunks(body, init):
        return lax.fori_loop(0, nchunk, body, init)

    qpos = q0 + lax.broadcasted_iota(I32, (SB, TQ), 1)

    def idx_body(c, carry):
        for sb in range(n_sb):
            r0 = pl.multiple_of(c * KC + sb * SB, SB)
            logit = jnp.dot(ikw_ref[pl.ds(r0, SB), :], rhs_idx, preferred_element_type=F32)
            sc = ((jnp.maximum(logit[:, 0:TQ], 0.0) * wq[0:1, :]
                   + jnp.maximum(logit[:, TQ:2 * TQ], 0.0) * wq[1:2, :])
                  + (jnp.maximum(logit[:, 2 * TQ:3 * TQ], 0.0) * wq[2:3, :]
                     + jnp.maximum(logit[:, 3 * TQ:4 * TQ], 0.0) * wq[3:4, :]))
            bits = lax.bitcast_convert_type(sc, I32)
            key = bits ^ ((bits >> 31) & jnp.int32(0x7FFFFFFF))
            key = jnp.where(key == -1, 0, key)
            krow = r0 + lax.broadcasted_iota(I32, (SB, TQ), 0)
            allowed = (krow >> 6) <= (qpos >> 6)
            key_scr[pl.ds(r0, SB), :] = jnp.where(allowed, key, jnp.int32(INT_MIN))
        return carry

    over_chunks(idx_body, 0)

    kf = jnp.float32(topk)
    far = jnp.int32(2 ** 30)

    def count(n, scr, pred):
        acc = jnp.zeros((SUBLANES, TQ), F32)
        for c in range(n):
            acc = acc + _fold_rows(jnp.where(pred(scr[c * KC:(c + 1) * KC, :]), 1.0, 0.0), jnp.add)
        return jnp.sum(acc, axis=0, keepdims=True)

    def bit_search(nbits, step):
        def variant(n):
            def run():
                return lax.fori_loop(0, nbits, lambda it, u: step(n, it, u), jnp.zeros((1, TQ), I32))
            return run
        return lax.switch(nchunk - 1, [variant(n) for n in range(1, seq // KC + 1)])

    def tau_step(n, it, u):
        cand_u = u | jnp.left_shift(jnp.int32(1), 31 - it)
        cand = cand_u ^ jnp.int32(INT_MIN)
        cnt = count(n, key_scr, lambda blk: blk >= cand)
        return jnp.where(cnt >= kf, cand_u, u)

    tau = bit_search(32, tau_step) ^ jnp.int32(INT_MIN)

    def tie_body(c, acc):
        r0 = pl.multiple_of(c * KC, KC)
        key = key_scr[pl.ds(r0, KC), :]
        krow = r0 + lax.broadcasted_iota(I32, (KC, TQ), 0)
        tk_scr[pl.ds(r0, KC), :] = jnp.where(key == tau, krow, far)
        return acc + _fold_rows(jnp.where(key > tau, 1.0, 0.0), jnp.add)

    n_gt = over_chunks(tie_body, jnp.zeros((SUBLANES, TQ), F32))
    need = kf - jnp.sum(n_gt, axis=0, keepdims=True)
    nbits = int(seq).bit_length()

    def jbit_step(n, it, jj):
        cand = jj | jnp.left_shift(jnp.int32(1), nbits - 1 - it)
        cnt = count(n, tk_scr, lambda blk: blk < cand)
        return jnp.where(cnt <= need, cand, jj)

    jsel = bit_search(nbits, jbit_step)
    jsel = jnp.where(tau == jnp.int32(INT_MIN), 0, jsel)

    def dm_body(c, carry):
        r0 = pl.multiple_of(c * KC, KC)
        krow = r0 + lax.broadcasted_iota(I32, (KC, TQ), 0)
        qp = q0 + lax.broadcasted_iota(I32, (KC, TQ), 1)
        dist = jnp.abs(qp - krow).astype(F32)
        inner = jnp.where(tk_scr[pl.ds(r0, KC), :] < jsel, dist, jnp.inf)
        dm_scr[pl.ds(r0, KC), :] = jnp.where(key_scr[pl.ds(r0, KC), :] > tau, dist, inner)
        return carry

    over_chunks(dm_body, 0)

    slopes = [s * LOG2E for s in SLOPES_D]

    def attn_body(c, carry):
        ms, ls, acc = list(carry[:4]), list(carry[4:8]), carry[8]
        atts = [jnp.dot(knv_scr[pl.ds(pl.multiple_of(c * KC + sb * SB, SB), SB), :], rhs_main,
                        preferred_element_type=F32) for sb in range(n_sb)]
        for sb in range(n_sb):
            r0 = pl.multiple_of(c * KC + sb * SB, SB)
            att = atts[sb]
            dm = dm_scr[pl.ds(r0, SB), :]
            alphas, probs = [], []
            for h in range(4):
                a = att[:, h * TQ:(h + 1) * TQ] - slopes[h] * dm
                m_new = jnp.maximum(ms[h], jnp.max(_fold_rows(a, jnp.maximum), axis=0, keepdims=True))
                m_use = jnp.where(m_new == -jnp.inf, 0.0, m_new)
                alpha = jnp.exp2(ms[h] - m_use)
                p = jnp.exp2(a - m_use)
                ls[h] = ls[h] * alpha + jnp.sum(_fold_rows(p, jnp.add), axis=0, keepdims=True)
                ms[h] = m_new
                alphas.append(alpha)
                probs.append(p.astype(BF16))
            pv = jnp.dot(vt_scr[c, :, sb * SB:(sb + 1) * SB], jnp.concatenate(probs, axis=1),
                         preferred_element_type=F32)
            acc = acc * jnp.concatenate(alphas, axis=1) + pv
        return (*ms, *ls, acc)

    neg = jnp.full((1, TQ), -jnp.inf, F32)
    zero = jnp.zeros((1, TQ), F32)
    res = over_chunks(attn_body, (neg,) * 4 + (zero,) * 4 + (jnp.zeros((DSA_HD, 4 * TQ), F32),))
    ls = res[4:8]
    out_t = res[8]
    o_t = jnp.concatenate([out_t[:, h * TQ:(h + 1) * TQ] * (1.0 / ls[h]) for h in range(4)], axis=0)
    return o_t.T


def _dsa(yd, qg, kg, p64, p64h):
    b, t, _ = yd.shape
    TB = TQ * DSA_SUBTILES
    nq = t // TB
    nkc = t // KC
    W = GROUP_W
    topk = min(DSA_TOPK_MAX, t // 4)
    kern = functools.partial(_dsa_kernel, seq=t, topk=topk)
    small = lambda a: pl.BlockSpec(a.shape, lambda bi, i: (0,) * a.ndim)
    return pl.pallas_call(
        kern,
        grid=(b, nq),
        in_specs=[pl.BlockSpec((None, TB, W), lambda bi, i: (bi, i, 0)),
                  pl.BlockSpec((None, TB, W), lambda bi, i: (bi, i, 1)),
                  pl.BlockSpec((None, TB, 128), lambda bi, i: (bi, i, 5)),
                  pl.BlockSpec((None, TB, 128), lambda bi, i: (bi, i, 6)),
                  pl.BlockSpec((None, t, 128), lambda bi, i: (bi, 0, 4)),
                  pl.BlockSpec((None, t, 128), lambda bi, i: (bi, 0, 6)),
                  small(qg), small(kg), small(p64), small(p64h)],
        out_specs=pl.BlockSpec((None, TB, W), lambda bi, i: (bi, i, 0)),
        out_shape=jax.ShapeDtypeStruct((b, t, W), BF16),
        scratch_shapes=[pltpu.VMEM((t, 128), BF16),
                        pltpu.VMEM((nkc, DSA_HD, KC), BF16),
                        pltpu.VMEM((t, TQ), I32),
                        pltpu.VMEM((t, TQ), I32),
                        pltpu.VMEM((t, TQ), F32)],
        compiler_params=pltpu.CompilerParams(dimension_semantics=("arbitrary", "arbitrary"),
                                             vmem_limit_bytes=VMEM_LIMIT),
        name="dsa",
    )(yd, yd, yd, yd, yd, yd, qg, kg, p64, p64h)


def kernel(x, norm_g, w_in, conv_w, conv_b, gmlp_g, gmlp_ws, gmlp_b, diff_qg, diff_kg, diff_lam,
           diff_subg, dsa_qg, dsa_kg, w_out):
    b, t, d = x.shape
    depth = w_in.shape[0]
    p32 = _block_diag_mean(GROUP_W, 32)
    p64 = _block_diag_mean(GROUP_W, 64)
    p64h = _block_diag_mean(128, 64)
    w_in16 = _wprep(w_in)
    w_out16 = _wprep(w_out)
    xf = x.reshape(b * t, d)
    for l in range(depth):
        ya, yb, yc, yd = _inproj(xf, norm_g[l].reshape(1, d), w_in16, l)
        ya, yb, yc, yd = (a.reshape(b, t, a.shape[-1]) for a in (ya, yb, yc, yd))
        bfull = jnp.repeat(gmlp_b[l].T, GROUP_W // 4, axis=1)
        mab = _mixab(ya, yb, conv_w[l], conv_b[l].reshape(1, -1), gmlp_g[l].reshape(1, -1),
                     gmlp_ws[l], bfull, p64)
        lam_init = 0.8 - 0.6 * math.exp(-0.3 * l)
        mc = _diff(yc, jnp.tile(diff_qg[l], 8).reshape(1, -1), jnp.tile(diff_kg[l], 8).reshape(1, -1),
                   diff_lam[l], diff_subg[l].reshape(1, -1), p32, p64, lam_init)
        md = _dsa(yd, jnp.tile(dsa_qg[l], 4).reshape(1, -1), jnp.tile(dsa_kg[l], 2).reshape(1, -1),
                  p64, p64h)
        xf = _outproj(xf, mab.reshape(b * t, -1), mc.reshape(b * t, -1), md.reshape(b * t, -1),
                      w_out16, l)
    return xf.reshape(b, t, d)
```

```python
import functools
import math

import numpy as np
import jax
import jax.numpy as jnp
from jax import lax
from jax.experimental import pallas as pl
from jax.experimental.pallas import tpu as pltpu

F32 = jnp.float32
BF16 = jnp.bfloat16
I32 = jnp.int32

GROUP_W = 256
CHUNK = 64
CONV_W = 3
GMLP_BLOCK = 128
DIFF_QD = 32
DSA_HD = 64
IDX_HD = 32
IDX_HEADS = 4
DSA_TOPK_MAX = 256
EPS = 1e-6
LOG2E = math.log2(math.e)
INT_MIN = -2 ** 31
LANES = 128
SUBLANES = 8

TQ = 128
TQ_DIFF = 256
DSA_SUBTILES = 4
MIX_SUBTILES = 4
KC = 512
TM = 512
HALO = 16
NHC = 8

_SLOPES = 2.0 ** (-8.0 * np.arange(1, 9) / 8.0)
SLOPES_C = [float(s) for s in _SLOPES[0::2]]
SLOPES_D = [float(s) for s in _SLOPES[1::2]]

VMEM_LIMIT = 56 * 1024 * 1024


def _block_diag_mean(width, seg):
    idx = np.arange(width) // seg
    return jnp.asarray((idx[:, None] == idx[None, :]).astype(np.float32) / seg, dtype=BF16)


def _seg_mean(x2, p):
    hi = x2.astype(BF16)
    lo = (x2 - hi.astype(F32)).astype(BF16)
    return (jnp.dot(hi, p, preferred_element_type=F32)
            + jnp.dot(lo, p, preferred_element_type=F32))


def _silu(z):
    return z * jax.nn.sigmoid(z)


def _gelu(x):
    return 0.5 * x * (1.0 + lax.erf(x * (2.0 ** -0.5)))


def _fold_rows(x, op, stop=SUBLANES):
    r = x.shape[0]
    while r > stop:
        r //= 2
        x = op(x[:r], x[r:])
    return x


def _fold_lanes(x, op):
    c = x.shape[1]
    while c > LANES:
        c //= 2
        x = op(x[:, :c], x[:, c:])
    return x


def _wprep_kernel(w_ref, o_ref, *, n_valid):
    col = pl.program_id(1) * LANES + lax.broadcasted_iota(I32, (1, LANES), 1)
    o_ref[...] = jnp.where(col < n_valid, w_ref[...], 0.0).astype(BF16)


def _wprep(w):
    depth, d, n = w.shape
    nt = pl.cdiv(n, LANES)
    return pl.pallas_call(
        functools.partial(_wprep_kernel, n_valid=n),
        grid=(depth, nt),
        in_specs=[pl.BlockSpec((None, d, LANES), lambda l, j: (l, 0, j))],
        out_specs=pl.BlockSpec((None, d, LANES), lambda l, j: (l, 0, j)),
        out_shape=jax.ShapeDtypeStruct((depth, d, nt * LANES), BF16),
        compiler_params=pltpu.CompilerParams(dimension_semantics=("arbitrary", "arbitrary")),
        name="wprep",
    )(w)


_A0, _B0, _C0, _D0 = 0, 4 * GROUP_W, 7 * GROUP_W, 11 * GROUP_W
_DQ, _DKV, _DZ, _DIQ, _DEND = _D0, _D0 + 256, _D0 + 384, _D0 + 640, _D0 + 896
YD_W = 896


def _inproj_kernel(x_ref, g_ref, w_ref, ya_ref, yb_ref, yc_ref, yd_ref):
    x = x_ref[...]
    ms = jnp.mean(x * x, axis=-1, keepdims=True)
    xn = (x * lax.rsqrt(ms + EPS) * g_ref[...]).astype(BF16)

    def proj(lo, hi):
        return jnp.dot(xn, w_ref[:, lo:hi], preferred_element_type=F32).astype(BF16)

    ya_ref[...] = proj(_A0, _B0)
    yb_ref[...] = proj(_B0, _C0)
    yc_ref[...] = proj(_C0, _D0)
    yd_ref[:, 0:256] = proj(_DQ, _DKV)
    yd_ref[:, 256:512] = proj(_DZ, _DIQ)
    yd_ref[:, 512:640] = proj(_DKV, _DZ)
    yd_ref[:, 640:896] = proj(_DIQ, _DEND)


def _inproj(xf, g, wb16, layer):
    m, d = xf.shape
    widths = (_B0 - _A0, _C0 - _B0, _D0 - _C0, YD_W)
    return pl.pallas_call(
        _inproj_kernel,
        grid=(m // TM,),
        in_specs=[pl.BlockSpec((TM, d), lambda i: (i, 0)),
                  pl.BlockSpec((1, d), lambda i: (0, 0)),
                  pl.BlockSpec((None,) + wb16.shape[1:], lambda i: (layer, 0, 0))],
        out_specs=[pl.BlockSpec((TM, w), lambda i: (i, 0)) for w in widths],
        out_shape=[jax.ShapeDtypeStruct((m, w), BF16) for w in widths],
        compiler_params=pltpu.CompilerParams(dimension_semantics=("arbitrary",),
                                             vmem_limit_bytes=VMEM_LIMIT),
        name="inproj",
    )(xf, g, wb16)


def _outproj_kernel(x_ref, mab_ref, mc_ref, md_ref, wo_ref, o_ref):
    acc = x_ref[...]
    acc = acc + jnp.dot(mab_ref[...], wo_ref[0:2 * GROUP_W, :], preferred_element_type=F32)
    acc = acc + jnp.dot(mc_ref[...], wo_ref[2 * GROUP_W:3 * GROUP_W, :], preferred_element_type=F32)
    acc = acc + jnp.dot(md_ref[...], wo_ref[3 * GROUP_W:4 * GROUP_W, :], preferred_element_type=F32)
    o_ref[...] = acc


def _outproj(xf, mab, mc, md, wo, layer):
    m, d = xf.shape
    return pl.pallas_call(
        _outproj_kernel,
        grid=(m // TM,),
        in_specs=[pl.BlockSpec((TM, d), lambda i: (i, 0)),
                  pl.BlockSpec((TM, 2 * GROUP_W), lambda i: (i, 0)),
                  pl.BlockSpec((TM, GROUP_W), lambda i: (i, 0)),
                  pl.BlockSpec((TM, GROUP_W), lambda i: (i, 0)),
                  pl.BlockSpec((None,) + wo.shape[1:], lambda i: (layer, 0, 0))],
        out_specs=pl.BlockSpec((TM, d), lambda i: (i, 0)),
        out_shape=jax.ShapeDtypeStruct((m, d), F32),
        compiler_params=pltpu.CompilerParams(dimension_semantics=("arbitrary",),
                                             vmem_limit_bytes=VMEM_LIMIT),
        name="outproj",
    )(xf, mab, mc, md, wo)


def _mixab_kernel(ya_ref, halo_ref, yb_ref, cw_ref, cb_ref, gg_ref, ws_ref, bfull_ref, p64_ref,
                  o_ref, u_scr):
    i = pl.program_id(1)
    W = GROUP_W
    TB = TQ * MIX_SUBTILES
    uh = halo_ref[:, 2 * W:3 * W].astype(F32) * halo_ref[:, 0:W].astype(F32)
    u_scr[0:HALO, :] = jnp.where(i > 0, uh, 0.0)
    u_scr[HALO:HALO + TB, :] = ya_ref[:, 2 * W:3 * W].astype(F32) * ya_ref[:, 0:W].astype(F32)
    for sb in range(MIX_SUBTILES):
        r0 = HALO + sb * TQ
        y = u_scr[r0 - 2:r0 - 2 + TQ, :] * cw_ref[0:1, :]
        y = y + u_scr[r0 - 1:r0 - 1 + TQ, :] * cw_ref[1:2, :]
        y = y + u_scr[r0:r0 + TQ, :] * cw_ref[2:3, :]
        rows = slice(sb * TQ, (sb + 1) * TQ)
        bg = ya_ref[rows, W:2 * W].astype(F32)
        za = ya_ref[rows, 3 * W:4 * W].astype(F32)
        o_ref[rows, 0:W] = (bg * (y + cb_ref[...]) * _silu(za)).astype(o_ref.dtype)

    t_idx = lax.broadcasted_iota(I32, (GMLP_BLOCK, GMLP_BLOCK), 0)
    s_idx = lax.broadcasted_iota(I32, (GMLP_BLOCK, GMLP_BLOCK), 1)
    causal = (t_idx >> 6) >= (s_idx >> 6)
    lane = lax.broadcasted_iota(I32, (1, W), 1)
    wms = [jnp.where(causal, ws_ref[hd], 0.0).astype(BF16) for hd in range(4)]

    def block(sb, carry):
        rows = pl.ds(pl.multiple_of(sb * TQ, TQ), TQ)
        u = _gelu(yb_ref[rows, 0:W].astype(F32))
        v = _gelu(yb_ref[rows, W:2 * W].astype(F32))
        zb = yb_ref[rows, 2 * W:3 * W].astype(F32)
        vn = v * lax.rsqrt(_seg_mean(v * v, p64_ref[...]) + EPS) * gg_ref[...]
        s = bfull_ref[...]
        for hd in range(4):
            vh = jnp.where((lane >> 6) == hd, vn, 0.0).astype(BF16)
            s = s + jnp.dot(wms[hd], vh, preferred_element_type=F32)
        o_ref[rows, W:2 * W] = (u * s * _silu(zb)).astype(o_ref.dtype)
        return carry

    lax.fori_loop(0, MIX_SUBTILES, block, 0)


def _mixab(ya, yb, cw, cb, gg, ws, bfull, p64):
    b, t, _ = ya.shape
    TB = TQ * MIX_SUBTILES
    nq = t // TB
    return pl.pallas_call(
        _mixab_kernel,
        grid=(b, nq),
        in_specs=[pl.BlockSpec((None, TB, 4 * GROUP_W), lambda bi, i: (bi, i, 0)),
                  pl.BlockSpec((None, HALO, 4 * GROUP_W),
                               lambda bi, i: (bi, jnp.maximum(i * (TB // HALO) - 1, 0), 0)),
                  pl.BlockSpec((None, TB, 3 * GROUP_W), lambda bi, i: (bi, i, 0)),
                  pl.BlockSpec(cw.shape, lambda bi, i: (0, 0)),
                  pl.BlockSpec(cb.shape, lambda bi, i: (0, 0)),
                  pl.BlockSpec(gg.shape, lambda bi, i: (0, 0)),
                  pl.BlockSpec(ws.shape, lambda bi, i: (0, 0, 0)),
                  pl.BlockSpec(bfull.shape, lambda bi, i: (0, 0)),
                  pl.BlockSpec(p64.shape, lambda bi, i: (0, 0))],
        out_specs=pl.BlockSpec((None, TB, 2 * GROUP_W), lambda bi, i: (bi, i, 0)),
        out_shape=jax.ShapeDtypeStruct((b, t, 2 * GROUP_W), BF16),
        scratch_shapes=[pltpu.VMEM((HALO + TB, GROUP_W), F32)],
        compiler_params=pltpu.CompilerParams(dimension_semantics=("arbitrary", "arbitrary"),
                                             vmem_limit_bytes=VMEM_LIMIT),
        name="mixab",
    )(ya, ya, yb, cw, cb, gg, ws, bfull, p64)


def _diff_kernel(q_ref, k_ref, v_ref, z_ref, qg_ref, kg_ref, lam_ref, subg_ref, p32_ref, p64_ref,
                 o_ref, kn_scr, qs_scr, s_scr, m_scr, l_scr, p_scr, acc_scr, *, lam_init, seq):
    TQ = TQ_DIFF
    i = pl.program_id(1)
    q0 = i * TQ
    nchunk = (q0 + TQ - 1) // KC + 1
    p32 = p32_ref[...]

    @pl.when(i == 0)
    def _():
        for c in range(seq // KC):
            kk = k_ref[c * KC:(c + 1) * KC, :].astype(F32)
            ms = _seg_mean(kk * kk, p32)
            kn_scr[c * KC:(c + 1) * KC, :] = (kk * lax.rsqrt(ms + EPS) * kg_ref[...]).astype(BF16)

    q = q_ref[...].astype(F32)
    qn = q * lax.rsqrt(_seg_mean(q * q, p32) + EPS) * (qg_ref[...] * (DIFF_QD ** -0.5 * LOG2E))
    lane = lax.broadcasted_iota(I32, (1, GROUP_W), 1)
    for hc in range(NHC):
        qs_scr[hc * TQ:(hc + 1) * TQ, :] = jnp.where((lane >> 5) == hc, qn, 0.0).astype(BF16)

    lp = lam_ref[...]
    lam = (jnp.exp(jnp.sum(lp[0:1] * lp[1:2], axis=-1, keepdims=True))
           - jnp.exp(jnp.sum(lp[2:3] * lp[3:4], axis=-1, keepdims=True)) + lam_init)

    m_scr[...] = jnp.full(m_scr.shape, -jnp.inf, F32)
    l_scr[...] = jnp.zeros(l_scr.shape, F32)
    acc_scr[...] = jnp.zeros(acc_scr.shape, F32)
    row = q0 + lax.broadcasted_iota(I32, (TQ, KC), 0)

    def s_body(c, carry):
        col = c * KC + lax.broadcasted_iota(I32, (TQ, KC), 1)
        dist = jnp.abs(row - col).astype(F32)
        dm = jnp.where((col >> 6) <= (row >> 6), dist, jnp.inf)
        kc = kn_scr[pl.ds(pl.multiple_of(c * KC, KC), KC), :]
        s_all = lax.dot_general(qs_scr[...], kc, (((1,), (1,)), ((), ())), preferred_element_type=F32)
        for h in range(NHC // 2):
            bias = (SLOPES_C[h] * LOG2E) * dm
            for j in range(2):
                rows = slice((2 * h + j) * TQ, (2 * h + j + 1) * TQ)
                s = s_all[rows, :] - bias
                s_scr[c, rows, :] = s
                m_scr[rows, :] = jnp.maximum(m_scr[rows, :], _fold_lanes(s, jnp.maximum))
        return carry

    lax.fori_loop(0, nchunk, s_body, 0)

    for hc in range(NHC):
        m = jnp.max(m_scr[hc * TQ:(hc + 1) * TQ, :], axis=1, keepdims=True)
        m_scr[hc * TQ:(hc + 1) * TQ, :] = jnp.broadcast_to(m, (TQ, LANES))

    def e_body(c, carry):
        for hc in range(NHC):
            rows = slice(hc * TQ, (hc + 1) * TQ)
            m = m_scr[rows, :]
            s = s_scr[c, rows, :]
            ps = [jnp.exp2(s[:, k * LANES:(k + 1) * LANES] - m) for k in range(KC // LANES)]
            l_scr[rows, :] += (ps[0] + ps[1]) + (ps[2] + ps[3])
            p_scr[rows, :] = jnp.concatenate(ps, axis=1).astype(BF16)
        vc = v_ref[pl.ds(pl.multiple_of(c * KC, KC), KC), :]
        acc_scr[...] += jnp.dot(p_scr[...], vc, preferred_element_type=F32)
        return carry

    lax.fori_loop(0, nchunk, e_body, 0)

    o = jnp.zeros((TQ, GROUP_W), F32)
    for h in range(NHC // 2):
        r1 = slice(2 * h * TQ, (2 * h + 1) * TQ)
        r2 = slice((2 * h + 1) * TQ, (2 * h + 2) * TQ)
        l1 = jnp.sum(l_scr[r1, :], axis=1, keepdims=True)
        l2 = jnp.sum(l_scr[r2, :], axis=1, keepdims=True)
        o_h = acc_scr[r1, :] * (1.0 / l1) - acc_scr[r2, :] * (lam / l2)
        o = jnp.where((lane >> 6) == h, o_h, o)

    ms = _seg_mean(o * o, p64_ref[...])
    o = o * lax.rsqrt(ms + EPS) * (subg_ref[...] * (1.0 - lam_init))
    z = z_ref[...].astype(F32)
    o_ref[...] = (o * _silu(z)).astype(o_ref.dtype)


def _diff(yc, qg, kg, lam_p, subg, p32, p64, lam_init):
    TQ = TQ_DIFF
    b, t, _ = yc.shape
    nq = t // TQ
    nkc = t // KC
    W = GROUP_W
    kern = functools.partial(_diff_kernel, lam_init=lam_init, seq=t)
    small = lambda a: pl.BlockSpec(a.shape, lambda bi, i: (0,) * a.ndim)
    return pl.pallas_call(
        kern,
        grid=(b, nq),
        in_specs=[pl.BlockSpec((None, TQ, W), lambda bi, i: (bi, i, 0)),
                  pl.BlockSpec((None, t, W), lambda bi, i: (bi, 0, 1)),
                  pl.BlockSpec((None, t, W), lambda bi, i: (bi, 0, 2)),
                  pl.BlockSpec((None, TQ, W), lambda bi, i: (bi, i, 3)),
                  small(qg), small(kg), small(lam_p), small(subg), small(p32), small(p64)],
        out_specs=pl.BlockSpec((None, TQ, W), lambda bi, i: (bi, i, 0)),
        out_shape=jax.ShapeDtypeStruct((b, t, W), BF16),
        scratch_shapes=[pltpu.VMEM((t, W), BF16),
                        pltpu.VMEM((NHC * TQ, W), BF16),
                        pltpu.VMEM((nkc, NHC * TQ, KC), F32),
                        pltpu.VMEM((NHC * TQ, LANES), F32),
                        pltpu.VMEM((NHC * TQ, LANES), F32),
                        pltpu.VMEM((NHC * TQ, KC), BF16),
                        pltpu.VMEM((NHC * TQ, W), F32)],
        compiler_params=pltpu.CompilerParams(dimension_semantics=("arbitrary", "arbitrary"),
                                             vmem_limit_bytes=VMEM_LIMIT),
        name="diffattn",
    )(yc, yc, yc, yc, qg, kg, lam_p, subg, p32, p64)


def _dsa_kernel(q_ref, z_ref, iq_ref, ikwq_ref, kv_ref, ikw_ref, qg_ref, kg_ref, p64_ref, p64h_ref,
                tri_ref, o_ref, knv_scr, vt_scr, key_scr, dm_scr, *, seq, topk):
    i = pl.program_id(1)

    @pl.when(i == 0)
    def _():
        p64h = p64h_ref[...]
        for c in range(seq // 128):
            blk = kv_ref[c * 128:(c + 1) * 128, :].astype(F32)
            ms = _seg_mean(blk * blk, p64h)
            knv_scr[c * 128:(c + 1) * 128, :] = (blk * lax.rsqrt(ms + EPS) * kg_ref[...]).astype(BF16)
            vt = blk.T
            cc, off = divmod(c * 128, KC)
            vt_scr[cc, :, off:off + 128] = vt[DSA_HD:2 * DSA_HD, :].astype(BF16)

    def sub_tile(sub, carry):
        rows = pl.ds(pl.multiple_of(sub * TQ, TQ), TQ)
        q0 = (i * DSA_SUBTILES + sub) * TQ
        nchunk = q0 // KC + 1

        iq_t = iq_ref[rows, :].astype(F32).T
        iw_t = ikwq_ref[rows, :].astype(F32).T[IDX_HD:IDX_HD + 8, :]
        wq = iw_t * (IDX_HEADS ** -0.5 * IDX_HD ** -0.5)
        zpad_i = jnp.zeros((128 - IDX_HD, TQ), F32)
        rhs_idx = jnp.concatenate(
            [jnp.concatenate([iq_t[IDX_HD * h:IDX_HD * (h + 1), :], zpad_i], axis=0)
             for h in range(IDX_HEADS)], axis=1).astype(BF16)

        q = q_ref[rows, :].astype(F32)
        qn = q * lax.rsqrt(_seg_mean(q * q, p64_ref[...]) + EPS) * (qg_ref[...] * (DSA_HD ** -0.5 * LOG2E))
        qn_t = qn.T
        zpad_q = jnp.zeros((128 - DSA_HD, TQ), F32)
        rhs_main = jnp.concatenate(
            [jnp.concatenate([qn_t[DSA_HD * h:DSA_HD * (h + 1), :], zpad_q], axis=0) for h in range(4)],
            axis=1).astype(BF16)

        o = _dsa_tile(nchunk, q0, rhs_idx, wq, rhs_main, ikw_ref, tri_ref, knv_scr, vt_scr, key_scr,
                      dm_scr, seq, topk)
        z = z_ref[rows, :].astype(F32)
        o_ref[rows, :] = (o * _silu(z)).astype(o_ref.dtype)
        return carry

    lax.fori_loop(0, DSA_SUBTILES, sub_tile, 0)


def _dsa_tile(nchunk, q0, rhs_idx, wq, rhs_main, ikw_ref, tri_ref, knv_scr, vt_scr, key_scr,
              dm_scr, seq, topk):
    SB = 256
    n_sb = KC // SB

    def over_chunks(body, init):
        return lax.fori_loop(0, nchunk, body, init)

    qpos = q0 + lax.broadcasted_iota(I32, (SB, TQ), 1)

    def idx_body(c, carry):
        for sb in range(n_sb):
            r0 = pl.multiple_of(c * KC + sb * SB, SB)
            logit = jnp.dot(ikw_ref[pl.ds(r0, SB), :], rhs_idx, preferred_element_type=F32)
            sc = ((jnp.maximum(logit[:, 0:TQ], 0.0) * wq[0:1, :]
                   + jnp.maximum(logit[:, TQ:2 * TQ], 0.0) * wq[1:2, :])
                  + (jnp.maximum(logit[:, 2 * TQ:3 * TQ], 0.0) * wq[2:3, :]
                     + jnp.maximum(logit[:, 3 * TQ:4 * TQ], 0.0) * wq[3:4, :]))
            bits = lax.bitcast_convert_type(sc, I32)
            key = bits ^ ((bits >> 31) & jnp.int32(0x7FFFFFFF))
            key = jnp.where(key == -1, 0, key)
            krow = r0 + lax.broadcasted_iota(I32, (SB, TQ), 0)
            allowed = (krow >> 6) <= (qpos >> 6)
            key_scr[pl.ds(r0, SB), :] = jnp.where(allowed, key, jnp.int32(INT_MIN))
        return carry

    over_chunks(idx_body, 0)

    kf = jnp.float32(topk)

    def count(n, scr, pred):
        acc = jnp.zeros((SUBLANES, TQ), F32)
        for c in range(n):
            acc = acc + _fold_rows(jnp.where(pred(scr[c * KC:(c + 1) * KC, :]), 1.0, 0.0), jnp.add)
        return jnp.sum(acc, axis=0, keepdims=True)

    def bit_search(nbits, step):
        def variant(n):
            def run():
                return lax.fori_loop(0, nbits, lambda it, u: step(n, it, u), jnp.zeros((1, TQ), I32))
            return run
        return lax.switch(nchunk - 1, [variant(n) for n in range(1, seq // KC + 1)])

    def tau_step(n, it, u):
        cand_u = u | jnp.left_shift(jnp.int32(1), 31 - it)
        cand = cand_u ^ jnp.int32(INT_MIN)
        cnt = count(n, key_scr, lambda blk: blk >= cand)
        return jnp.where(cnt >= kf, cand_u, u)

    tau = bit_search(32, tau_step) ^ jnp.int32(INT_MIN)

    def gt_body(c, acc):
        key = key_scr[pl.ds(pl.multiple_of(c * KC, KC), KC), :]
        return acc + _fold_rows(jnp.where(key > tau, 1.0, 0.0), jnp.add)

    n_gt = over_chunks(gt_body, jnp.zeros((SUBLANES, TQ), F32))
    need = kf - jnp.sum(n_gt, axis=0, keepdims=True)
    need = jnp.where(tau == jnp.int32(INT_MIN), 0.0, need)

    def dm_body(c, ties_before):
        r0 = pl.multiple_of(c * KC, KC)
        krow = r0 + lax.broadcasted_iota(I32, (KC, TQ), 0)
        qp = q0 + lax.broadcasted_iota(I32, (KC, TQ), 1)
        dist = jnp.abs(qp - krow).astype(F32)
        key = key_scr[pl.ds(r0, KC), :]
        is_tie = key == tau
        rank = ties_before + jnp.dot(tri_ref[...], jnp.where(is_tie, 1.0, 0.0).astype(BF16),
                                     preferred_element_type=F32)
        inner = jnp.where(is_tie, jnp.where(rank <= need, dist, jnp.inf), jnp.inf)
        dm_scr[pl.ds(r0, KC), :] = jnp.where(key > tau, dist, inner)
        return rank[KC - 1:KC, :]

    over_chunks(dm_body, jnp.zeros((1, TQ), F32))

    slopes = [s * LOG2E for s in SLOPES_D]

    def attn_body(c, carry):
        ms, ls, acc = list(carry[:4]), list(carry[4:8]), carry[8]
        atts = [jnp.dot(knv_scr[pl.ds(pl.multiple_of(c * KC + sb * SB, SB), SB), :], rhs_main,
                        preferred_element_type=F32) for sb in range(n_sb)]
        for sb in range(n_sb):
            r0 = pl.multiple_of(c * KC + sb * SB, SB)
            att = atts[sb]
            dm = dm_scr[pl.ds(r0, SB), :]
            alphas, probs = [], []
            for h in range(4):
                a = att[:, h * TQ:(h + 1) * TQ] - slopes[h] * dm
                m_new = jnp.maximum(ms[h], jnp.max(_fold_rows(a, jnp.maximum), axis=0, keepdims=True))
                m_use = jnp.where(m_new == -jnp.inf, 0.0, m_new)
                alpha = jnp.exp2(ms[h] - m_use)
                p = jnp.exp2(a - m_use)
                ls[h] = ls[h] * alpha + jnp.sum(_fold_rows(p, jnp.add), axis=0, keepdims=True)
                ms[h] = m_new
                alphas.append(alpha)
                probs.append(p.astype(BF16))
            pv = jnp.dot(vt_scr[c, :, sb * SB:(sb + 1) * SB], jnp.concatenate(probs, axis=1),
                         preferred_element_type=F32)
            acc = acc * jnp.concatenate(alphas, axis=1) + pv
        return (*ms, *ls, acc)

    neg = jnp.full((1, TQ), -jnp.inf, F32)
    zero = jnp.zeros((1, TQ), F32)
    res = over_chunks(attn_body, (neg,) * 4 + (zero,) * 4 + (jnp.zeros((DSA_HD, 4 * TQ), F32),))
    ls = res[4:8]
    out_t = res[8]
    o_t = jnp.concatenate([out_t[:, h * TQ:(h + 1) * TQ] * (1.0 / ls[h]) for h in range(4)], axis=0)
    return o_t.T


def _dsa(yd, qg, kg, p64, p64h):
    b, t, _ = yd.shape
    TB = TQ * DSA_SUBTILES
    nq = t // TB
    nkc = t // KC
    W = GROUP_W
    topk = min(DSA_TOPK_MAX, t // 4)
    kern = functools.partial(_dsa_kernel, seq=t, topk=topk)
    small = lambda a: pl.BlockSpec(a.shape, lambda bi, i: (0,) * a.ndim)
    tri = jnp.asarray(np.tril(np.ones((KC, KC), np.float32)), dtype=BF16)
    return pl.pallas_call(
        kern,
        grid=(b, nq),
        in_specs=[pl.BlockSpec((None, TB, W), lambda bi, i: (bi, i, 0)),
                  pl.BlockSpec((None, TB, W), lambda bi, i: (bi, i, 1)),
                  pl.BlockSpec((None, TB, 128), lambda bi, i: (bi, i, 5)),
                  pl.BlockSpec((None, TB, 128), lambda bi, i: (bi, i, 6)),
                  pl.BlockSpec((None, t, 128), lambda bi, i: (bi, 0, 4)),
                  pl.BlockSpec((None, t, 128), lambda bi, i: (bi, 0, 6)),
                  small(qg), small(kg), small(p64), small(p64h), small(tri)],
        out_specs=pl.BlockSpec((None, TB, W), lambda bi, i: (bi, i, 0)),
        out_shape=jax.ShapeDtypeStruct((b, t, W), BF16),
        scratch_shapes=[pltpu.VMEM((t, 128), BF16),
                        pltpu.VMEM((nkc, DSA_HD, KC), BF16),
                        pltpu.VMEM((t, TQ), I32),
                        pltpu.VMEM((t, TQ), F32)],
        compiler_params=pltpu.CompilerParams(dimension_semantics=("arbitrary", "arbitrary"),
                                             vmem_limit_bytes=VMEM_LIMIT),
        name="dsa",
    )(yd, yd, yd, yd, yd, yd, qg, kg, p64, p64h, tri)


def kernel(x, norm_g, w_in, conv_w, conv_b, gmlp_g, gmlp_ws, gmlp_b, diff_qg, diff_kg, diff_lam,
           diff_subg, dsa_qg, dsa_kg, w_out):
    b, t, d = x.shape
    depth = w_in.shape[0]
    p32 = _block_diag_mean(GROUP_W, 32)
    p64 = _block_diag_mean(GROUP_W, 64)
    p64h = _block_diag_mean(128, 64)
    w_in16 = _wprep(w_in)
    w_out16 = _wprep(w_out)
    xf = x.reshape(b * t, d)
    for l in range(depth):
        ya, yb, yc, yd = _inproj(xf, norm_g[l].reshape(1, d), w_in16, l)
        ya, yb, yc, yd = (a.reshape(b, t, a.shape[-1]) for a in (ya, yb, yc, yd))
        bfull = jnp.repeat(gmlp_b[l].T, GROUP_W // 4, axis=1)
        mab = _mixab(ya, yb, conv_w[l], conv_b[l].reshape(1, -1), gmlp_g[l].reshape(1, -1),
                     gmlp_ws[l], bfull, p64)
        lam_init = 0.8 - 0.6 * math.exp(-0.3 * l)
        mc = _diff(yc, jnp.tile(diff_qg[l], 8).reshape(1, -1), jnp.tile(diff_kg[l], 8).reshape(1, -1),
                   diff_lam[l], diff_subg[l].reshape(1, -1), p32, p64, lam_init)
        md = _dsa(yd, jnp.tile(dsa_qg[l], 4).reshape(1, -1), jnp.tile(dsa_kg[l], 2).reshape(1, -1),
                  p64, p64h)
        xf = _outproj(xf, mab.reshape(b * t, -1), mc.reshape(b * t, -1), md.reshape(b * t, -1),
                      w_out16, l)
    return xf.reshape(b, t, d)
```

```python
import functools
import math

import numpy as np
import jax
import jax.numpy as jnp
from jax import lax
from jax.experimental import pallas as pl
from jax.experimental.pallas import tpu as pltpu

F32 = jnp.float32
BF16 = jnp.bfloat16
I32 = jnp.int32

GROUP_W = 256
CHUNK = 64
CONV_W = 3
GMLP_BLOCK = 128
DIFF_QD = 32
DSA_HD = 64
IDX_HD = 32
IDX_HEADS = 4
DSA_TOPK_MAX = 256
EPS = 1e-6
LOG2E = math.log2(math.e)
INT_MIN = -2 ** 31
LANES = 128
SUBLANES = 8

TQ = 128
TQ_DIFF = 256
DIFF_SUBTILES = 2
DSA_SUBTILES = 4
MIX_SUBTILES = 4
KC = 512
TM = 512
HALO = 16
NHC = 8

_SLOPES = 2.0 ** (-8.0 * np.arange(1, 9) / 8.0)
SLOPES_C = [float(s) for s in _SLOPES[0::2]]
SLOPES_D = [float(s) for s in _SLOPES[1::2]]

VMEM_LIMIT = 56 * 1024 * 1024


def _block_diag_mean(width, seg):
    idx = np.arange(width) // seg
    return jnp.asarray((idx[:, None] == idx[None, :]).astype(np.float32) / seg, dtype=BF16)


def _seg_mean(x2, p):
    hi = x2.astype(BF16)
    lo = (x2 - hi.astype(F32)).astype(BF16)
    return (jnp.dot(hi, p, preferred_element_type=F32)
            + jnp.dot(lo, p, preferred_element_type=F32))


def _silu(z):
    return z * jax.nn.sigmoid(z)


def _gelu(x):
    return 0.5 * x * (1.0 + lax.erf(x * (2.0 ** -0.5)))


def _fold_rows(x, op, stop=SUBLANES):
    r = x.shape[0]
    while r > stop:
        r //= 2
        x = op(x[:r], x[r:])
    return x


def _fold_lanes(x, op):
    c = x.shape[1]
    while c > LANES:
        c //= 2
        x = op(x[:, :c], x[:, c:])
    return x


def _wprep_kernel(w_ref, o_ref, *, n_valid):
    col = pl.program_id(1) * LANES + lax.broadcasted_iota(I32, (1, LANES), 1)
    o_ref[...] = jnp.where(col < n_valid, w_ref[...], 0.0).astype(BF16)


def _wprep(w):
    depth, d, n = w.shape
    nt = pl.cdiv(n, LANES)
    return pl.pallas_call(
        functools.partial(_wprep_kernel, n_valid=n),
        grid=(depth, nt),
        in_specs=[pl.BlockSpec((None, d, LANES), lambda l, j: (l, 0, j))],
        out_specs=pl.BlockSpec((None, d, LANES), lambda l, j: (l, 0, j)),
        out_shape=jax.ShapeDtypeStruct((depth, d, nt * LANES), BF16),
        compiler_params=pltpu.CompilerParams(dimension_semantics=("arbitrary", "arbitrary")),
        name="wprep",
    )(w)


_A0, _B0, _C0, _D0 = 0, 4 * GROUP_W, 7 * GROUP_W, 11 * GROUP_W
_DQ, _DKV, _DZ, _DIQ, _DEND = _D0, _D0 + 256, _D0 + 384, _D0 + 640, _D0 + 896
YD_W = 896


_Y_WIDTHS = (_B0 - _A0, _C0 - _B0, _D0 - _C0, YD_W)


def _norm_project(x, g_ref, w_ref, ya_ref, yb_ref, yc_ref, yd_ref):
    ms = jnp.mean(x * x, axis=-1, keepdims=True)
    xn = (x * lax.rsqrt(ms + EPS) * g_ref[...]).astype(BF16)

    def proj(lo, hi):
        return jnp.dot(xn, w_ref[:, lo:hi], preferred_element_type=F32).astype(BF16)

    ya_ref[...] = proj(_A0, _B0)
    yb_ref[...] = proj(_B0, _C0)
    yc_ref[...] = proj(_C0, _D0)
    yd_ref[:, 0:256] = proj(_DQ, _DKV)
    yd_ref[:, 256:512] = proj(_DZ, _DIQ)
    yd_ref[:, 512:640] = proj(_DKV, _DZ)
    yd_ref[:, 640:896] = proj(_DIQ, _DEND)


def _mix_project(x_ref, mab_ref, mc_ref, md_ref, wo_ref):
    acc = x_ref[...]
    acc = acc + jnp.dot(mab_ref[...], wo_ref[0:2 * GROUP_W, :], preferred_element_type=F32)
    acc = acc + jnp.dot(mc_ref[...], wo_ref[2 * GROUP_W:3 * GROUP_W, :], preferred_element_type=F32)
    return acc + jnp.dot(md_ref[...], wo_ref[3 * GROUP_W:4 * GROUP_W, :], preferred_element_type=F32)


def _inproj_kernel(x_ref, g_ref, w_ref, ya_ref, yb_ref, yc_ref, yd_ref):
    _norm_project(x_ref[...], g_ref, w_ref, ya_ref, yb_ref, yc_ref, yd_ref)


def _outproj_kernel(x_ref, mab_ref, mc_ref, md_ref, wo_ref, o_ref):
    o_ref[...] = _mix_project(x_ref, mab_ref, mc_ref, md_ref, wo_ref)


def _outin_kernel(x_ref, mab_ref, mc_ref, md_ref, wo_ref, g_ref, w_ref,
                  o_ref, ya_ref, yb_ref, yc_ref, yd_ref):
    x_new = _mix_project(x_ref, mab_ref, mc_ref, md_ref, wo_ref)
    o_ref[...] = x_new
    _norm_project(x_new, g_ref, w_ref, ya_ref, yb_ref, yc_ref, yd_ref)


def _row_spec(width):
    return pl.BlockSpec((TM, width), lambda i: (i, 0))


def _layer_spec(w, layer):
    return pl.BlockSpec((None,) + w.shape[1:], lambda i: (layer, 0, 0))


_PROJ_PARAMS = pltpu.CompilerParams(dimension_semantics=("arbitrary",), vmem_limit_bytes=VMEM_LIMIT)


def _inproj(xf, g, wb16, layer):
    m, d = xf.shape
    return pl.pallas_call(
        _inproj_kernel,
        grid=(m // TM,),
        in_specs=[_row_spec(d), pl.BlockSpec((1, d), lambda i: (0, 0)), _layer_spec(wb16, layer)],
        out_specs=[_row_spec(w) for w in _Y_WIDTHS],
        out_shape=[jax.ShapeDtypeStruct((m, w), BF16) for w in _Y_WIDTHS],
        compiler_params=_PROJ_PARAMS,
        name="inproj",
    )(xf, g, wb16)


def _outproj(xf, mab, mc, md, wo, layer):
    m, d = xf.shape
    return pl.pallas_call(
        _outproj_kernel,
        grid=(m // TM,),
        in_specs=[_row_spec(d), _row_spec(2 * GROUP_W), _row_spec(GROUP_W), _row_spec(GROUP_W),
                  _layer_spec(wo, layer)],
        out_specs=_row_spec(d),
        out_shape=jax.ShapeDtypeStruct((m, d), F32),
        compiler_params=_PROJ_PARAMS,
        name="outproj",
    )(xf, mab, mc, md, wo)


def _outin(xf, mab, mc, md, wo, layer, g_next, wb16):
    m, d = xf.shape
    return pl.pallas_call(
        _outin_kernel,
        grid=(m // TM,),
        in_specs=[_row_spec(d), _row_spec(2 * GROUP_W), _row_spec(GROUP_W), _row_spec(GROUP_W),
                  _layer_spec(wo, layer), pl.BlockSpec((1, d), lambda i: (0, 0)),
                  _layer_spec(wb16, layer + 1)],
        out_specs=[_row_spec(d)] + [_row_spec(w) for w in _Y_WIDTHS],
        out_shape=[jax.ShapeDtypeStruct((m, d), F32)]
                  + [jax.ShapeDtypeStruct((m, w), BF16) for w in _Y_WIDTHS],
        compiler_params=_PROJ_PARAMS,
        name="outin",
    )(xf, mab, mc, md, wo, g_next, wb16)


def _mixab_kernel(ya_ref, halo_ref, yb_ref, cw_ref, cb_ref, gg_ref, ws_ref, bfull_ref, p64_ref,
                  o_ref, u_scr):
    i = pl.program_id(1)
    W = GROUP_W
    TB = TQ * MIX_SUBTILES
    uh = halo_ref[:, 2 * W:3 * W].astype(F32) * halo_ref[:, 0:W].astype(F32)
    u_scr[0:HALO, :] = jnp.where(i > 0, uh, 0.0)
    u_scr[HALO:HALO + TB, :] = ya_ref[:, 2 * W:3 * W].astype(F32) * ya_ref[:, 0:W].astype(F32)
    for sb in range(MIX_SUBTILES):
        r0 = HALO + sb * TQ
        y = u_scr[r0 - 2:r0 - 2 + TQ, :] * cw_ref[0:1, :]
        y = y + u_scr[r0 - 1:r0 - 1 + TQ, :] * cw_ref[1:2, :]
        y = y + u_scr[r0:r0 + TQ, :] * cw_ref[2:3, :]
        rows = slice(sb * TQ, (sb + 1) * TQ)
        bg = ya_ref[rows, W:2 * W].astype(F32)
        za = ya_ref[rows, 3 * W:4 * W].astype(F32)
        o_ref[rows, 0:W] = (bg * (y + cb_ref[...]) * _silu(za)).astype(o_ref.dtype)

    t_idx = lax.broadcasted_iota(I32, (GMLP_BLOCK, GMLP_BLOCK), 0)
    s_idx = lax.broadcasted_iota(I32, (GMLP_BLOCK, GMLP_BLOCK), 1)
    causal = (t_idx >> 6) >= (s_idx >> 6)
    lane = lax.broadcasted_iota(I32, (1, W), 1)
    wms = [jnp.where(causal, ws_ref[hd], 0.0).astype(BF16) for hd in range(4)]

    def block(sb, carry):
        rows = pl.ds(pl.multiple_of(sb * TQ, TQ), TQ)
        u = _gelu(yb_ref[rows, 0:W].astype(F32))
        v = _gelu(yb_ref[rows, W:2 * W].astype(F32))
        zb = yb_ref[rows, 2 * W:3 * W].astype(F32)
        vn = v * lax.rsqrt(_seg_mean(v * v, p64_ref[...]) + EPS) * gg_ref[...]
        s = bfull_ref[...]
        for hd in range(4):
            vh = jnp.where((lane >> 6) == hd, vn, 0.0).astype(BF16)
            s = s + jnp.dot(wms[hd], vh, preferred_element_type=F32)
        o_ref[rows, W:2 * W] = (u * s * _silu(zb)).astype(o_ref.dtype)
        return carry

    lax.fori_loop(0, MIX_SUBTILES, block, 0)


def _mixab(ya, yb, cw, cb, gg, ws, bfull, p64):
    b, t, _ = ya.shape
    TB = TQ * MIX_SUBTILES
    nq = t // TB
    return pl.pallas_call(
        _mixab_kernel,
        grid=(b, nq),
        in_specs=[pl.BlockSpec((None, TB, 4 * GROUP_W), lambda bi, i: (bi, i, 0)),
                  pl.BlockSpec((None, HALO, 4 * GROUP_W),
                               lambda bi, i: (bi, jnp.maximum(i * (TB // HALO) - 1, 0), 0)),
                  pl.BlockSpec((None, TB, 3 * GROUP_W), lambda bi, i: (bi, i, 0)),
                  pl.BlockSpec(cw.shape, lambda bi, i: (0, 0)),
                  pl.BlockSpec(cb.shape, lambda bi, i: (0, 0)),
                  pl.BlockSpec(gg.shape, lambda bi, i: (0, 0)),
                  pl.BlockSpec(ws.shape, lambda bi, i: (0, 0, 0)),
                  pl.BlockSpec(bfull.shape, lambda bi, i: (0, 0)),
                  pl.BlockSpec(p64.shape, lambda bi, i: (0, 0))],
        out_specs=pl.BlockSpec((None, TB, 2 * GROUP_W), lambda bi, i: (bi, i, 0)),
        out_shape=jax.ShapeDtypeStruct((b, t, 2 * GROUP_W), BF16),
        scratch_shapes=[pltpu.VMEM((HALO + TB, GROUP_W), F32)],
        compiler_params=pltpu.CompilerParams(dimension_semantics=("arbitrary", "arbitrary"),
                                             vmem_limit_bytes=VMEM_LIMIT),
        name="mixab",
    )(ya, ya, yb, cw, cb, gg, ws, bfull, p64)


def _diff_kernel(q_ref, k_ref, v_ref, z_ref, qg_ref, kg_ref, lam_ref, subg_ref, p32_ref, p64_ref,
                 o_ref, kn_scr, qs_scr, s_scr, m_scr, l_scr, p_scr, acc_scr, *, lam_init, seq):
    TQ = TQ_DIFF
    i = pl.program_id(1)

    @pl.when(i == 0)
    def _():
        for c in range(seq // KC):
            kk = k_ref[c * KC:(c + 1) * KC, :].astype(F32)
            ms = _seg_mean(kk * kk, p32_ref[...])
            kn_scr[c * KC:(c + 1) * KC, :] = (kk * lax.rsqrt(ms + EPS) * kg_ref[...]).astype(BF16)

    lp = lam_ref[...]
    lam = (jnp.exp(jnp.sum(lp[0:1] * lp[1:2], axis=-1, keepdims=True))
           - jnp.exp(jnp.sum(lp[2:3] * lp[3:4], axis=-1, keepdims=True)) + lam_init)

    def sub_tile(sub, carry):
        rows = pl.ds(pl.multiple_of(sub * TQ, TQ), TQ)
        q0 = (i * DIFF_SUBTILES + sub) * TQ
        o = _diff_tile(q0, q_ref[rows, :].astype(F32), lam, v_ref, qg_ref, subg_ref, p32_ref, p64_ref,
                       kn_scr, qs_scr, s_scr, m_scr, l_scr, p_scr, acc_scr, lam_init)
        z = z_ref[rows, :].astype(F32)
        o_ref[rows, :] = (o * _silu(z)).astype(o_ref.dtype)
        return carry

    lax.fori_loop(0, DIFF_SUBTILES, sub_tile, 0)


def _diff_tile(q0, q, lam, v_ref, qg_ref, subg_ref, p32_ref, p64_ref,
               kn_scr, qs_scr, s_scr, m_scr, l_scr, p_scr, acc_scr, lam_init):
    TQ = TQ_DIFF
    nchunk = (q0 + TQ - 1) // KC + 1
    p32 = p32_ref[...]
    qn = q * lax.rsqrt(_seg_mean(q * q, p32) + EPS) * (qg_ref[...] * (DIFF_QD ** -0.5 * LOG2E))
    lane = lax.broadcasted_iota(I32, (1, GROUP_W), 1)
    for hc in range(NHC):
        qs_scr[hc * TQ:(hc + 1) * TQ, :] = jnp.where((lane >> 5) == hc, qn, 0.0).astype(BF16)

    m_scr[...] = jnp.full(m_scr.shape, -jnp.inf, F32)
    l_scr[...] = jnp.zeros(l_scr.shape, F32)
    acc_scr[...] = jnp.zeros(acc_scr.shape, F32)
    row = q0 + lax.broadcasted_iota(I32, (TQ, KC), 0)

    def s_body(c, carry):
        col = c * KC + lax.broadcasted_iota(I32, (TQ, KC), 1)
        dist = jnp.abs(row - col).astype(F32)
        dm = jnp.where((col >> 6) <= (row >> 6), dist, jnp.inf)
        kc = kn_scr[pl.ds(pl.multiple_of(c * KC, KC), KC), :]
        s_all = lax.dot_general(qs_scr[...], kc, (((1,), (1,)), ((), ())), preferred_element_type=F32)
        for h in range(NHC // 2):
            bias = (SLOPES_C[h] * LOG2E) * dm
            for j in range(2):
                rows = slice((2 * h + j) * TQ, (2 * h + j + 1) * TQ)
                s = s_all[rows, :] - bias
                s_scr[c, rows, :] = s
                m_scr[rows, :] = jnp.maximum(m_scr[rows, :], _fold_lanes(s, jnp.maximum))
        return carry

    lax.fori_loop(0, nchunk, s_body, 0)

    for hc in range(NHC):
        m = jnp.max(m_scr[hc * TQ:(hc + 1) * TQ, :], axis=1, keepdims=True)
        m_scr[hc * TQ:(hc + 1) * TQ, :] = jnp.broadcast_to(m, (TQ, LANES))

    def e_body(c, carry):
        for hc in range(NHC):
            rows = slice(hc * TQ, (hc + 1) * TQ)
            m = m_scr[rows, :]
            s = s_scr[c, rows, :]
            ps = [jnp.exp2(s[:, k * LANES:(k + 1) * LANES] - m) for k in range(KC // LANES)]
            l_scr[rows, :] += (ps[0] + ps[1]) + (ps[2] + ps[3])
            p_scr[rows, :] = jnp.concatenate(ps, axis=1).astype(BF16)
        vc = v_ref[pl.ds(pl.multiple_of(c * KC, KC), KC), :]
        acc_scr[...] += jnp.dot(p_scr[...], vc, preferred_element_type=F32)
        return carry

    lax.fori_loop(0, nchunk, e_body, 0)

    o = jnp.zeros((TQ, GROUP_W), F32)
    for h in range(NHC // 2):
        r1 = slice(2 * h * TQ, (2 * h + 1) * TQ)
        r2 = slice((2 * h + 1) * TQ, (2 * h + 2) * TQ)
        l1 = jnp.sum(l_scr[r1, :], axis=1, keepdims=True)
        l2 = jnp.sum(l_scr[r2, :], axis=1, keepdims=True)
        o_h = acc_scr[r1, :] * (1.0 / l1) - acc_scr[r2, :] * (lam / l2)
        o = jnp.where((lane >> 6) == h, o_h, o)

    ms = _seg_mean(o * o, p64_ref[...])
    return o * lax.rsqrt(ms + EPS) * (subg_ref[...] * (1.0 - lam_init))


def _diff(yc, qg, kg, lam_p, subg, p32, p64, lam_init):
    TQ = TQ_DIFF
    TB = TQ * DIFF_SUBTILES
    b, t, _ = yc.shape
    nq = t // TB
    nkc = t // KC
    W = GROUP_W
    kern = functools.partial(_diff_kernel, lam_init=lam_init, seq=t)
    small = lambda a: pl.BlockSpec(a.shape, lambda bi, i: (0,) * a.ndim)
    return pl.pallas_call(
        kern,
        grid=(b, nq),
        in_specs=[pl.BlockSpec((None, TB, W), lambda bi, i: (bi, i, 0)),
                  pl.BlockSpec((None, t, W), lambda bi, i: (bi, 0, 1)),
                  pl.BlockSpec((None, t, W), lambda bi, i: (bi, 0, 2)),
                  pl.BlockSpec((None, TB, W), lambda bi, i: (bi, i, 3)),
                  small(qg), small(kg), small(lam_p), small(subg), small(p32), small(p64)],
        out_specs=pl.BlockSpec((None, TB, W), lambda bi, i: (bi, i, 0)),
        out_shape=jax.ShapeDtypeStruct((b, t, W), BF16),
        scratch_shapes=[pltpu.VMEM((t, W), BF16),
                        pltpu.VMEM((NHC * TQ, W), BF16),
                        pltpu.VMEM((nkc, NHC * TQ, KC), F32),
                        pltpu.VMEM((NHC * TQ, LANES), F32),
                        pltpu.VMEM((NHC * TQ, LANES), F32),
                        pltpu.VMEM((NHC * TQ, KC), BF16),
                        pltpu.VMEM((NHC * TQ, W), F32)],
        compiler_params=pltpu.CompilerParams(dimension_semantics=("arbitrary", "arbitrary"),
                                             vmem_limit_bytes=VMEM_LIMIT),
        name="diffattn",
    )(yc, yc, yc, yc, qg, kg, lam_p, subg, p32, p64)


def _dsa_kernel(q_ref, z_ref, iq_ref, ikwq_ref, kv_ref, ikw_ref, qg_ref, kg_ref, p64_ref, p64h_ref,
                tri_ref, o_ref, knv_scr, vt_scr, key_scr, dm_scr, *, seq, topk):
    i = pl.program_id(1)

    @pl.when(i == 0)
    def _():
        p64h = p64h_ref[...]
        for c in range(seq // 128):
            blk = kv_ref[c * 128:(c + 1) * 128, :].astype(F32)
            ms = _seg_mean(blk * blk, p64h)
            knv_scr[c * 128:(c + 1) * 128, :] = (blk * lax.rsqrt(ms + EPS) * kg_ref[...]).astype(BF16)
            vt = blk.T
            cc, off = divmod(c * 128, KC)
            vt_scr[cc, :, off:off + 128] = vt[DSA_HD:2 * DSA_HD, :].astype(BF16)

    def sub_tile(sub, carry):
        rows = pl.ds(pl.multiple_of(sub * TQ, TQ), TQ)
        q0 = (i * DSA_SUBTILES + sub) * TQ
        nchunk = q0 // KC + 1

        iq_t = iq_ref[rows, :].astype(F32).T
        iw_t = ikwq_ref[rows, :].astype(F32).T[IDX_HD:IDX_HD + 8, :]
        wq = iw_t * (IDX_HEADS ** -0.5 * IDX_HD ** -0.5)
        zpad_i = jnp.zeros((128 - IDX_HD, TQ), F32)
        rhs_idx = jnp.concatenate(
            [jnp.concatenate([iq_t[IDX_HD * h:IDX_HD * (h + 1), :], zpad_i], axis=0)
             for h in range(IDX_HEADS)], axis=1).astype(BF16)

        q = q_ref[rows, :].astype(F32)
        qn = q * lax.rsqrt(_seg_mean(q * q, p64_ref[...]) + EPS) * (qg_ref[...] * (DSA_HD ** -0.5 * LOG2E))
        qn_t = qn.T
        zpad_q = jnp.zeros((128 - DSA_HD, TQ), F32)
        rhs_main = jnp.concatenate(
            [jnp.concatenate([qn_t[DSA_HD * h:DSA_HD * (h + 1), :], zpad_q], axis=0) for h in range(4)],
            axis=1).astype(BF16)

        o = _dsa_tile(nchunk, q0, rhs_idx, wq, rhs_main, ikw_ref, tri_ref, knv_scr, vt_scr, key_scr,
                      dm_scr, seq, topk)
        z = z_ref[rows, :].astype(F32)
        o_ref[rows, :] = (o * _silu(z)).astype(o_ref.dtype)
        return carry

    lax.fori_loop(0, DSA_SUBTILES, sub_tile, 0)


def _dsa_tile(nchunk, q0, rhs_idx, wq, rhs_main, ikw_ref, tri_ref, knv_scr, vt_scr, key_scr,
              dm_scr, seq, topk):
    SB = 256
    n_sb = KC // SB

    def over_chunks(body, init):
        return lax.fori_loop(0, nchunk, body, init)

    qpos = q0 + lax.broadcasted_iota(I32, (SB, TQ), 1)

    def idx_body(c, carry):
        for sb in range(n_sb):
            r0 = pl.multiple_of(c * KC + sb * SB, SB)
            logit = jnp.dot(ikw_ref[pl.ds(r0, SB), :], rhs_idx, preferred_element_type=F32)
            sc = ((jnp.maximum(logit[:, 0:TQ], 0.0) * wq[0:1, :]
                   + jnp.maximum(logit[:, TQ:2 * TQ], 0.0) * wq[1:2, :])
                  + (jnp.maximum(logit[:, 2 * TQ:3 * TQ], 0.0) * wq[2:3, :]
                     + jnp.maximum(logit[:, 3 * TQ:4 * TQ], 0.0) * wq[3:4, :]))
            bits = lax.bitcast_convert_type(sc, I32)
            key = bits ^ ((bits >> 31) & jnp.int32(0x7FFFFFFF))
            key = jnp.where(key == -1, 0, key)
            krow = r0 + lax.broadcasted_iota(I32, (SB, TQ), 0)
            allowed = (krow >> 6) <= (qpos >> 6)
            key_scr[pl.ds(r0, SB), :] = jnp.where(allowed, key, jnp.int32(INT_MIN))
        return carry

    over_chunks(idx_body, 0)

    kf = jnp.float32(topk)

    def count(n, scr, pred):
        acc = jnp.zeros((SUBLANES, TQ), F32)
        for c in range(n):
            acc = acc + _fold_rows(jnp.where(pred(scr[c * KC:(c + 1) * KC, :]), 1.0, 0.0), jnp.add)
        return jnp.sum(acc, axis=0, keepdims=True)

    def bit_search(nbits, step):
        def variant(n):
            def run():
                return lax.fori_loop(0, nbits, lambda it, u: step(n, it, u), jnp.zeros((1, TQ), I32))
            return run
        return lax.switch(nchunk - 1, [variant(n) for n in range(1, seq // KC + 1)])

    def tau_step(n, it, u):
        cand_u = u | jnp.left_shift(jnp.int32(1), 31 - it)
        cand = cand_u ^ jnp.int32(INT_MIN)
        cnt = count(n, key_scr, lambda blk: blk >= cand)
        return jnp.where(cnt >= kf, cand_u, u)

    tau = bit_search(32, tau_step) ^ jnp.int32(INT_MIN)

    def gt_body(c, acc):
        key = key_scr[pl.ds(pl.multiple_of(c * KC, KC), KC), :]
        return acc + _fold_rows(jnp.where(key > tau, 1.0, 0.0), jnp.add)

    n_gt = over_chunks(gt_body, jnp.zeros((SUBLANES, TQ), F32))
    need = kf - jnp.sum(n_gt, axis=0, keepdims=True)
    need = jnp.where(tau == jnp.int32(INT_MIN), 0.0, need)

    def dm_body(c, ties_before):
        r0 = pl.multiple_of(c * KC, KC)
        krow = r0 + lax.broadcasted_iota(I32, (KC, TQ), 0)
        qp = q0 + lax.broadcasted_iota(I32, (KC, TQ), 1)
        dist = jnp.abs(qp - krow).astype(F32)
        key = key_scr[pl.ds(r0, KC), :]
        is_tie = key == tau
        rank = ties_before + jnp.dot(tri_ref[...], jnp.where(is_tie, 1.0, 0.0).astype(BF16),
                                     preferred_element_type=F32)
        inner = jnp.where(is_tie, jnp.where(rank <= need, dist, jnp.inf), jnp.inf)
        dm_scr[pl.ds(r0, KC), :] = jnp.where(key > tau, dist, inner)
        return rank[KC - 1:KC, :]

    over_chunks(dm_body, jnp.zeros((1, TQ), F32))

    slopes = [s * LOG2E for s in SLOPES_D]

    def attn_body(c, carry):
        ms, ls, acc = list(carry[:4]), list(carry[4:8]), carry[8]
        atts = [jnp.dot(knv_scr[pl.ds(pl.multiple_of(c * KC + sb * SB, SB), SB), :], rhs_main,
                        preferred_element_type=F32) for sb in range(n_sb)]
        for sb in range(n_sb):
            r0 = pl.multiple_of(c * KC + sb * SB, SB)
            att = atts[sb]
            dm = dm_scr[pl.ds(r0, SB), :]
            alphas, probs = [], []
            for h in range(4):
                a = att[:, h * TQ:(h + 1) * TQ] - slopes[h] * dm
                m_new = jnp.maximum(ms[h], jnp.max(_fold_rows(a, jnp.maximum), axis=0, keepdims=True))
                m_use = jnp.where(m_new == -jnp.inf, 0.0, m_new)
                alpha = jnp.exp2(ms[h] - m_use)
                p = jnp.exp2(a - m_use)
                ls[h] = ls[h] * alpha + jnp.sum(_fold_rows(p, jnp.add), axis=0, keepdims=True)
                ms[h] = m_new
                alphas.append(alpha)
                probs.append(p.astype(BF16))
            pv = jnp.dot(vt_scr[c, :, sb * SB:(sb + 1) * SB], jnp.concatenate(probs, axis=1),
                         preferred_element_type=F32)
            acc = acc * jnp.concatenate(alphas, axis=1) + pv
        return (*ms, *ls, acc)

    neg = jnp.full((1, TQ), -jnp.inf, F32)
    zero = jnp.zeros((1, TQ), F32)
    res = over_chunks(attn_body, (neg,) * 4 + (zero,) * 4 + (jnp.zeros((DSA_HD, 4 * TQ), F32),))
    ls = res[4:8]
    out_t = res[8]
    o_t = jnp.concatenate([out_t[:, h * TQ:(h + 1) * TQ] * (1.0 / ls[h]) for h in range(4)], axis=0)
    return o_t.T


def _dsa(yd, qg, kg, p64, p64h):
    b, t, _ = yd.shape
    TB = TQ * DSA_SUBTILES
    nq = t // TB
    nkc = t // KC
    W = GROUP_W
    topk = min(DSA_TOPK_MAX, t // 4)
    kern = functools.partial(_dsa_kernel, seq=t, topk=topk)
    small = lambda a: pl.BlockSpec(a.shape, lambda bi, i: (0,) * a.ndim)
    tri = jnp.asarray(np.tril(np.ones((KC, KC), np.float32)), dtype=BF16)
    return pl.pallas_call(
        kern,
        grid=(b, nq),
        in_specs=[pl.BlockSpec((None, TB, W), lambda bi, i: (bi, i, 0)),
                  pl.BlockSpec((None, TB, W), lambda bi, i: (bi, i, 1)),
                  pl.BlockSpec((None, TB, 128), lambda bi, i: (bi, i, 5)),
                  pl.BlockSpec((None, TB, 128), lambda bi, i: (bi, i, 6)),
                  pl.BlockSpec((None, t, 128), lambda bi, i: (bi, 0, 4)),
                  pl.BlockSpec((None, t, 128), lambda bi, i: (bi, 0, 6)),
                  small(qg), small(kg), small(p64), small(p64h), small(tri)],
        out_specs=pl.BlockSpec((None, TB, W), lambda bi, i: (bi, i, 0)),
        out_shape=jax.ShapeDtypeStruct((b, t, W), BF16),
        scratch_shapes=[pltpu.VMEM((t, 128), BF16),
                        pltpu.VMEM((nkc, DSA_HD, KC), BF16),
                        pltpu.VMEM((t, TQ), I32),
                        pltpu.VMEM((t, TQ), F32)],
        compiler_params=pltpu.CompilerParams(dimension_semantics=("arbitrary", "arbitrary"),
                                             vmem_limit_bytes=VMEM_LIMIT),
        name="dsa",
    )(yd, yd, yd, yd, yd, yd, qg, kg, p64, p64h, tri)


def kernel(x, norm_g, w_in, conv_w, conv_b, gmlp_g, gmlp_ws, gmlp_b, diff_qg, diff_kg, diff_lam,
           diff_subg, dsa_qg, dsa_kg, w_out):
    b, t, d = x.shape
    depth = w_in.shape[0]
    p32 = _block_diag_mean(GROUP_W, 32)
    p64 = _block_diag_mean(GROUP_W, 64)
    p64h = _block_diag_mean(128, 64)
    w_in16 = _wprep(w_in)
    w_out16 = _wprep(w_out)
    xf = x.reshape(b * t, d)
    ys = _inproj(xf, norm_g[0].reshape(1, d), w_in16, 0)
    for l in range(depth):
        ya, yb, yc, yd = (a.reshape(b, t, a.shape[-1]) for a in ys)
        bfull = jnp.repeat(gmlp_b[l].T, GROUP_W // 4, axis=1)
        mab = _mixab(ya, yb, conv_w[l], conv_b[l].reshape(1, -1), gmlp_g[l].reshape(1, -1),
                     gmlp_ws[l], bfull, p64)
        lam_init = 0.8 - 0.6 * math.exp(-0.3 * l)
        mc = _diff(yc, jnp.tile(diff_qg[l], 8).reshape(1, -1), jnp.tile(diff_kg[l], 8).reshape(1, -1),
                   diff_lam[l], diff_subg[l].reshape(1, -1), p32, p64, lam_init)
        md = _dsa(yd, jnp.tile(dsa_qg[l], 4).reshape(1, -1), jnp.tile(dsa_kg[l], 2).reshape(1, -1),
                  p64, p64h)
        mixes = (mab.reshape(b * t, -1), mc.reshape(b * t, -1), md.reshape(b * t, -1))
        if l + 1 < depth:
            xf, *ys = _outin(xf, *mixes, w_out16, l, norm_g[l + 1].reshape(1, d), w_in16)
        else:
            xf = _outproj(xf, *mixes, w_out16, l)
    return xf.reshape(b, t, d)
```

```python
import functools
import math

import numpy as np
import jax
import jax.numpy as jnp
from jax import lax
from jax.experimental import pallas as pl
from jax.experimental.pallas import tpu as pltpu

F32 = jnp.float32
BF16 = jnp.bfloat16
I32 = jnp.int32

GROUP_W = 256
CHUNK = 64
CONV_W = 3
GMLP_BLOCK = 128
DIFF_QD = 32
DSA_HD = 64
IDX_HD = 32
IDX_HEADS = 4
DSA_TOPK_MAX = 256
EPS = 1e-6
LOG2E = math.log2(math.e)
INT_MIN = -2 ** 31
LANES = 128
SUBLANES = 8

TQ = 128
TQ_DIFF = 256
DIFF_SUBTILES = 2
DSA_SUBTILES = 4
DSA_SB = 256
MIX_SUBTILES = 4
KC = 512
TM = 512
HALO = 16
NHC = 8

_SLOPES = 2.0 ** (-8.0 * np.arange(1, 9) / 8.0)
SLOPES_C = [float(s) for s in _SLOPES[0::2]]
SLOPES_D = [float(s) for s in _SLOPES[1::2]]

VMEM_LIMIT = 56 * 1024 * 1024


def _block_diag_mean(width, seg):
    idx = np.arange(width) // seg
    return jnp.asarray((idx[:, None] == idx[None, :]).astype(np.float32) / seg, dtype=BF16)


def _seg_mean(x2, p):
    hi = x2.astype(BF16)
    lo = (x2 - hi.astype(F32)).astype(BF16)
    return (jnp.dot(hi, p, preferred_element_type=F32)
            + jnp.dot(lo, p, preferred_element_type=F32))


def _silu(z):
    return z * jax.nn.sigmoid(z)


def _gelu(x):
    return 0.5 * x * (1.0 + lax.erf(x * (2.0 ** -0.5)))


def _fold_rows(x, op, stop=SUBLANES):
    r = x.shape[0]
    while r > stop:
        r //= 2
        x = op(x[:r], x[r:])
    return x


def _fold_lanes(x, op):
    c = x.shape[1]
    while c > LANES:
        c //= 2
        x = op(x[:, :c], x[:, c:])
    return x


def _wprep_kernel(w_ref, o_ref, *, n_valid):
    col = pl.program_id(1) * LANES + lax.broadcasted_iota(I32, (1, LANES), 1)
    o_ref[...] = jnp.where(col < n_valid, w_ref[...], 0.0).astype(BF16)


def _wprep(w):
    depth, d, n = w.shape
    nt = pl.cdiv(n, LANES)
    return pl.pallas_call(
        functools.partial(_wprep_kernel, n_valid=n),
        grid=(depth, nt),
        in_specs=[pl.BlockSpec((None, d, LANES), lambda l, j: (l, 0, j))],
        out_specs=pl.BlockSpec((None, d, LANES), lambda l, j: (l, 0, j)),
        out_shape=jax.ShapeDtypeStruct((depth, d, nt * LANES), BF16),
        compiler_params=pltpu.CompilerParams(dimension_semantics=("arbitrary", "arbitrary")),
        name="wprep",
    )(w)


_A0, _B0, _C0, _D0 = 0, 4 * GROUP_W, 7 * GROUP_W, 11 * GROUP_W
_DQ, _DKV, _DZ, _DIQ, _DEND = _D0, _D0 + 256, _D0 + 384, _D0 + 640, _D0 + 896
YD_W = 896


_Y_WIDTHS = (_B0 - _A0, _C0 - _B0, _D0 - _C0, YD_W)


def _norm_project(x, g_ref, w_ref, ya_ref, yb_ref, yc_ref, yd_ref):
    ms = jnp.mean(x * x, axis=-1, keepdims=True)
    xn = (x * lax.rsqrt(ms + EPS) * g_ref[...]).astype(BF16)

    def proj(lo, hi):
        return jnp.dot(xn, w_ref[:, lo:hi], preferred_element_type=F32).astype(BF16)

    ya_ref[...] = proj(_A0, _B0)
    yb_ref[...] = proj(_B0, _C0)
    yc_ref[...] = proj(_C0, _D0)
    yd_ref[:, 0:256] = proj(_DQ, _DKV)
    yd_ref[:, 256:512] = proj(_DZ, _DIQ)
    yd_ref[:, 512:640] = proj(_DKV, _DZ)
    yd_ref[:, 640:896] = proj(_DIQ, _DEND)


def _mix_project(x_ref, mab_ref, mc_ref, md_ref, wo_ref):
    acc = x_ref[...]
    acc = acc + jnp.dot(mab_ref[...], wo_ref[0:2 * GROUP_W, :], preferred_element_type=F32)
    acc = acc + jnp.dot(mc_ref[...], wo_ref[2 * GROUP_W:3 * GROUP_W, :], preferred_element_type=F32)
    return acc + jnp.dot(md_ref[...], wo_ref[3 * GROUP_W:4 * GROUP_W, :], preferred_element_type=F32)


def _inproj_kernel(x_ref, g_ref, w_ref, ya_ref, yb_ref, yc_ref, yd_ref):
    _norm_project(x_ref[...], g_ref, w_ref, ya_ref, yb_ref, yc_ref, yd_ref)


def _outproj_kernel(x_ref, mab_ref, mc_ref, md_ref, wo_ref, o_ref):
    o_ref[...] = _mix_project(x_ref, mab_ref, mc_ref, md_ref, wo_ref)


def _outin_kernel(x_ref, mab_ref, mc_ref, md_ref, wo_ref, g_ref, w_ref,
                  o_ref, ya_ref, yb_ref, yc_ref, yd_ref):
    x_new = _mix_project(x_ref, mab_ref, mc_ref, md_ref, wo_ref)
    o_ref[...] = x_new
    _norm_project(x_new, g_ref, w_ref, ya_ref, yb_ref, yc_ref, yd_ref)


def _row_spec(width):
    return pl.BlockSpec((TM, width), lambda i: (i, 0))


def _layer_spec(w, layer):
    return pl.BlockSpec((None,) + w.shape[1:], lambda i: (layer, 0, 0))


_PROJ_PARAMS = pltpu.CompilerParams(dimension_semantics=("arbitrary",), vmem_limit_bytes=VMEM_LIMIT)


def _inproj(xf, g, wb16, layer):
    m, d = xf.shape
    return pl.pallas_call(
        _inproj_kernel,
        grid=(m // TM,),
        in_specs=[_row_spec(d), pl.BlockSpec((1, d), lambda i: (0, 0)), _layer_spec(wb16, layer)],
        out_specs=[_row_spec(w) for w in _Y_WIDTHS],
        out_shape=[jax.ShapeDtypeStruct((m, w), BF16) for w in _Y_WIDTHS],
        compiler_params=_PROJ_PARAMS,
        name="inproj",
    )(xf, g, wb16)


def _outproj(xf, mab, mc, md, wo, layer):
    m, d = xf.shape
    return pl.pallas_call(
        _outproj_kernel,
        grid=(m // TM,),
        in_specs=[_row_spec(d), _row_spec(2 * GROUP_W), _row_spec(GROUP_W), _row_spec(GROUP_W),
                  _layer_spec(wo, layer)],
        out_specs=_row_spec(d),
        out_shape=jax.ShapeDtypeStruct((m, d), F32),
        compiler_params=_PROJ_PARAMS,
        name="outproj",
    )(xf, mab, mc, md, wo)


def _outin(xf, mab, mc, md, wo, layer, g_next, wb16):
    m, d = xf.shape
    return pl.pallas_call(
        _outin_kernel,
        grid=(m // TM,),
        in_specs=[_row_spec(d), _row_spec(2 * GROUP_W), _row_spec(GROUP_W), _row_spec(GROUP_W),
                  _layer_spec(wo, layer), pl.BlockSpec((1, d), lambda i: (0, 0)),
                  _layer_spec(wb16, layer + 1)],
        out_specs=[_row_spec(d)] + [_row_spec(w) for w in _Y_WIDTHS],
        out_shape=[jax.ShapeDtypeStruct((m, d), F32)]
                  + [jax.ShapeDtypeStruct((m, w), BF16) for w in _Y_WIDTHS],
        compiler_params=_PROJ_PARAMS,
        name="outin",
    )(xf, mab, mc, md, wo, g_next, wb16)


def _mixab_kernel(ya_ref, halo_ref, yb_ref, cw_ref, cb_ref, gg_ref, ws_ref, bfull_ref, p64_ref,
                  o_ref, u_scr):
    i = pl.program_id(1)
    W = GROUP_W
    TB = TQ * MIX_SUBTILES
    uh = halo_ref[:, 2 * W:3 * W].astype(F32) * halo_ref[:, 0:W].astype(F32)
    u_scr[0:HALO, :] = jnp.where(i > 0, uh, 0.0)
    u_scr[HALO:HALO + TB, :] = ya_ref[:, 2 * W:3 * W].astype(F32) * ya_ref[:, 0:W].astype(F32)
    for sb in range(MIX_SUBTILES):
        r0 = HALO + sb * TQ
        y = u_scr[r0 - 2:r0 - 2 + TQ, :] * cw_ref[0:1, :]
        y = y + u_scr[r0 - 1:r0 - 1 + TQ, :] * cw_ref[1:2, :]
        y = y + u_scr[r0:r0 + TQ, :] * cw_ref[2:3, :]
        rows = slice(sb * TQ, (sb + 1) * TQ)
        bg = ya_ref[rows, W:2 * W].astype(F32)
        za = ya_ref[rows, 3 * W:4 * W].astype(F32)
        o_ref[rows, 0:W] = (bg * (y + cb_ref[...]) * _silu(za)).astype(o_ref.dtype)

    t_idx = lax.broadcasted_iota(I32, (GMLP_BLOCK, GMLP_BLOCK), 0)
    s_idx = lax.broadcasted_iota(I32, (GMLP_BLOCK, GMLP_BLOCK), 1)
    causal = (t_idx >> 6) >= (s_idx >> 6)
    lane = lax.broadcasted_iota(I32, (1, W), 1)
    wms = [jnp.where(causal, ws_ref[hd], 0.0).astype(BF16) for hd in range(4)]

    def block(sb, carry):
        rows = pl.ds(pl.multiple_of(sb * TQ, TQ), TQ)
        u = _gelu(yb_ref[rows, 0:W].astype(F32))
        v = _gelu(yb_ref[rows, W:2 * W].astype(F32))
        zb = yb_ref[rows, 2 * W:3 * W].astype(F32)
        vn = v * lax.rsqrt(_seg_mean(v * v, p64_ref[...]) + EPS) * gg_ref[...]
        s = bfull_ref[...]
        for hd in range(4):
            vh = jnp.where((lane >> 6) == hd, vn, 0.0).astype(BF16)
            s = s + jnp.dot(wms[hd], vh, preferred_element_type=F32)
        o_ref[rows, W:2 * W] = (u * s * _silu(zb)).astype(o_ref.dtype)
        return carry

    lax.fori_loop(0, MIX_SUBTILES, block, 0)


def _mixab(ya, yb, cw, cb, gg, ws, bfull, p64):
    b, t, _ = ya.shape
    TB = TQ * MIX_SUBTILES
    nq = t // TB
    return pl.pallas_call(
        _mixab_kernel,
        grid=(b, nq),
        in_specs=[pl.BlockSpec((None, TB, 4 * GROUP_W), lambda bi, i: (bi, i, 0)),
                  pl.BlockSpec((None, HALO, 4 * GROUP_W),
                               lambda bi, i: (bi, jnp.maximum(i * (TB // HALO) - 1, 0), 0)),
                  pl.BlockSpec((None, TB, 3 * GROUP_W), lambda bi, i: (bi, i, 0)),
                  pl.BlockSpec(cw.shape, lambda bi, i: (0, 0)),
                  pl.BlockSpec(cb.shape, lambda bi, i: (0, 0)),
                  pl.BlockSpec(gg.shape, lambda bi, i: (0, 0)),
                  pl.BlockSpec(ws.shape, lambda bi, i: (0, 0, 0)),
                  pl.BlockSpec(bfull.shape, lambda bi, i: (0, 0)),
                  pl.BlockSpec(p64.shape, lambda bi, i: (0, 0))],
        out_specs=pl.BlockSpec((None, TB, 2 * GROUP_W), lambda bi, i: (bi, i, 0)),
        out_shape=jax.ShapeDtypeStruct((b, t, 2 * GROUP_W), BF16),
        scratch_shapes=[pltpu.VMEM((HALO + TB, GROUP_W), F32)],
        compiler_params=pltpu.CompilerParams(dimension_semantics=("arbitrary", "arbitrary"),
                                             vmem_limit_bytes=VMEM_LIMIT),
        name="mixab",
    )(ya, ya, yb, cw, cb, gg, ws, bfull, p64)


def _diff_kernel(q_ref, k_ref, v_ref, z_ref, qg_ref, kg_ref, lam_ref, subg_ref, p32_ref, p64_ref,
                 o_ref, kn_scr, qs_scr, s_scr, m_scr, l_scr, p_scr, acc_scr, *, lam_init, seq):
    TQ = TQ_DIFF
    i = pl.program_id(1)

    @pl.when(i == 0)
    def _():
        for c in range(seq // KC):
            kk = k_ref[c * KC:(c + 1) * KC, :].astype(F32)
            ms = _seg_mean(kk * kk, p32_ref[...])
            kn_scr[c * KC:(c + 1) * KC, :] = (kk * lax.rsqrt(ms + EPS) * kg_ref[...]).astype(BF16)

    lp = lam_ref[...]
    lam = (jnp.exp(jnp.sum(lp[0:1] * lp[1:2], axis=-1, keepdims=True))
           - jnp.exp(jnp.sum(lp[2:3] * lp[3:4], axis=-1, keepdims=True)) + lam_init)

    def sub_tile(sub, carry):
        rows = pl.ds(pl.multiple_of(sub * TQ, TQ), TQ)
        q0 = (i * DIFF_SUBTILES + sub) * TQ
        o = _diff_tile(q0, q_ref[rows, :].astype(F32), lam, v_ref, qg_ref, subg_ref, p32_ref, p64_ref,
                       kn_scr, qs_scr, s_scr, m_scr, l_scr, p_scr, acc_scr, lam_init)
        z = z_ref[rows, :].astype(F32)
        o_ref[rows, :] = (o * _silu(z)).astype(o_ref.dtype)
        return carry

    lax.fori_loop(0, DIFF_SUBTILES, sub_tile, 0)


def _diff_tile(q0, q, lam, v_ref, qg_ref, subg_ref, p32_ref, p64_ref,
               kn_scr, qs_scr, s_scr, m_scr, l_scr, p_scr, acc_scr, lam_init):
    TQ = TQ_DIFF
    nchunk = (q0 + TQ - 1) // KC + 1
    p32 = p32_ref[...]
    qn = q * lax.rsqrt(_seg_mean(q * q, p32) + EPS) * (qg_ref[...] * (DIFF_QD ** -0.5 * LOG2E))
    lane = lax.broadcasted_iota(I32, (1, GROUP_W), 1)
    for hc in range(NHC):
        qs_scr[hc * TQ:(hc + 1) * TQ, :] = jnp.where((lane >> 5) == hc, qn, 0.0).astype(BF16)

    m_scr[...] = jnp.full(m_scr.shape, -jnp.inf, F32)
    l_scr[...] = jnp.zeros(l_scr.shape, F32)
    acc_scr[...] = jnp.zeros(acc_scr.shape, F32)
    row = q0 + lax.broadcasted_iota(I32, (TQ, KC), 0)

    def s_body(c, carry):
        col = c * KC + lax.broadcasted_iota(I32, (TQ, KC), 1)
        dist = jnp.abs(row - col).astype(F32)
        dm = jnp.where((col >> 6) <= (row >> 6), dist, jnp.inf)
        kc = kn_scr[pl.ds(pl.multiple_of(c * KC, KC), KC), :]
        s_all = lax.dot_general(qs_scr[...], kc, (((1,), (1,)), ((), ())), preferred_element_type=F32)
        for h in range(NHC // 2):
            bias = (SLOPES_C[h] * LOG2E) * dm
            for j in range(2):
                rows = slice((2 * h + j) * TQ, (2 * h + j + 1) * TQ)
                s = s_all[rows, :] - bias
                s_scr[c, rows, :] = s
                m_scr[rows, :] = jnp.maximum(m_scr[rows, :], _fold_lanes(s, jnp.maximum))
        return carry

    lax.fori_loop(0, nchunk, s_body, 0)

    for hc in range(NHC):
        m = jnp.max(m_scr[hc * TQ:(hc + 1) * TQ, :], axis=1, keepdims=True)
        m_scr[hc * TQ:(hc + 1) * TQ, :] = jnp.broadcast_to(m, (TQ, LANES))

    def e_body(c, carry):
        for hc in range(NHC):
            rows = slice(hc * TQ, (hc + 1) * TQ)
            m = m_scr[rows, :]
            s = s_scr[c, rows, :]
            ps = [jnp.exp2(s[:, k * LANES:(k + 1) * LANES] - m) for k in range(KC // LANES)]
            l_scr[rows, :] += (ps[0] + ps[1]) + (ps[2] + ps[3])
            p_scr[rows, :] = jnp.concatenate(ps, axis=1).astype(BF16)
        vc = v_ref[pl.ds(pl.multiple_of(c * KC, KC), KC), :]
        acc_scr[...] += jnp.dot(p_scr[...], vc, preferred_element_type=F32)
        return carry

    lax.fori_loop(0, nchunk, e_body, 0)

    o = jnp.zeros((TQ, GROUP_W), F32)
    for h in range(NHC // 2):
        r1 = slice(2 * h * TQ, (2 * h + 1) * TQ)
        r2 = slice((2 * h + 1) * TQ, (2 * h + 2) * TQ)
        l1 = jnp.sum(l_scr[r1, :], axis=1, keepdims=True)
        l2 = jnp.sum(l_scr[r2, :], axis=1, keepdims=True)
        o_h = acc_scr[r1, :] * (1.0 / l1) - acc_scr[r2, :] * (lam / l2)
        o = jnp.where((lane >> 6) == h, o_h, o)

    ms = _seg_mean(o * o, p64_ref[...])
    return o * lax.rsqrt(ms + EPS) * (subg_ref[...] * (1.0 - lam_init))


def _diff(yc, qg, kg, lam_p, subg, p32, p64, lam_init):
    TQ = TQ_DIFF
    TB = TQ * DIFF_SUBTILES
    b, t, _ = yc.shape
    nq = t // TB
    nkc = t // KC
    W = GROUP_W
    kern = functools.partial(_diff_kernel, lam_init=lam_init, seq=t)
    small = lambda a: pl.BlockSpec(a.shape, lambda bi, i: (0,) * a.ndim)
    return pl.pallas_call(
        kern,
        grid=(b, nq),
        in_specs=[pl.BlockSpec((None, TB, W), lambda bi, i: (bi, i, 0)),
                  pl.BlockSpec((None, t, W), lambda bi, i: (bi, 0, 1)),
                  pl.BlockSpec((None, t, W), lambda bi, i: (bi, 0, 2)),
                  pl.BlockSpec((None, TB, W), lambda bi, i: (bi, i, 3)),
                  small(qg), small(kg), small(lam_p), small(subg), small(p32), small(p64)],
        out_specs=pl.BlockSpec((None, TB, W), lambda bi, i: (bi, i, 0)),
        out_shape=jax.ShapeDtypeStruct((b, t, W), BF16),
        scratch_shapes=[pltpu.VMEM((t, W), BF16),
                        pltpu.VMEM((NHC * TQ, W), BF16),
                        pltpu.VMEM((nkc, NHC * TQ, KC), F32),
                        pltpu.VMEM((NHC * TQ, LANES), F32),
                        pltpu.VMEM((NHC * TQ, LANES), F32),
                        pltpu.VMEM((NHC * TQ, KC), BF16),
                        pltpu.VMEM((NHC * TQ, W), F32)],
        compiler_params=pltpu.CompilerParams(dimension_semantics=("arbitrary", "arbitrary"),
                                             vmem_limit_bytes=VMEM_LIMIT),
        name="diffattn",
    )(yc, yc, yc, yc, qg, kg, lam_p, subg, p32, p64)


def _dsa_kernel(q_ref, z_ref, iq_ref, ikwq_ref, kv_ref, ikw_ref, qg_ref, kg_ref, p64_ref, p64h_ref,
                tri_ref, o_ref, knv_scr, vt_scr, key_scr, dm_scr, *, seq, topk):
    i = pl.program_id(1)

    @pl.when(i == 0)
    def _():
        p64h = p64h_ref[...]
        for c in range(seq // 128):
            blk = kv_ref[c * 128:(c + 1) * 128, :].astype(F32)
            ms = _seg_mean(blk * blk, p64h)
            knv_scr[c * 128:(c + 1) * 128, :] = (blk * lax.rsqrt(ms + EPS) * kg_ref[...]).astype(BF16)
            vt = blk.T
            cc, off = divmod(c * 128, KC)
            vt_scr[cc, :, off:off + 128] = vt[DSA_HD:2 * DSA_HD, :].astype(BF16)

    def sub_tile(sub, carry):
        rows = pl.ds(pl.multiple_of(sub * TQ, TQ), TQ)
        q0 = (i * DSA_SUBTILES + sub) * TQ
        nchunk = q0 // KC + 1

        iq_t = iq_ref[rows, :].astype(F32).T
        iw_t = ikwq_ref[rows, :].astype(F32).T[IDX_HD:IDX_HD + 8, :]
        wq = iw_t * (IDX_HEADS ** -0.5 * IDX_HD ** -0.5)
        zpad_i = jnp.zeros((128 - IDX_HD, TQ), F32)
        rhs_idx = jnp.concatenate(
            [jnp.concatenate([iq_t[IDX_HD * h:IDX_HD * (h + 1), :], zpad_i], axis=0)
             for h in range(IDX_HEADS)], axis=1).astype(BF16)

        q = q_ref[rows, :].astype(F32)
        qn = q * lax.rsqrt(_seg_mean(q * q, p64_ref[...]) + EPS) * (qg_ref[...] * (DSA_HD ** -0.5 * LOG2E))
        qn_t = qn.T
        zpad_q = jnp.zeros((128 - DSA_HD, TQ), F32)
        rhs_main = jnp.concatenate(
            [jnp.concatenate([qn_t[DSA_HD * h:DSA_HD * (h + 1), :], zpad_q], axis=0) for h in range(4)],
            axis=1).astype(BF16)

        o = _dsa_tile(nchunk, q0, rhs_idx, wq, rhs_main, ikw_ref, tri_ref, knv_scr, vt_scr, key_scr,
                      dm_scr, seq, topk)
        z = z_ref[rows, :].astype(F32)
        o_ref[rows, :] = (o * _silu(z)).astype(o_ref.dtype)
        return carry

    lax.fori_loop(0, DSA_SUBTILES, sub_tile, 0)


def _dsa_tile(nchunk, q0, rhs_idx, wq, rhs_main, ikw_ref, tri_ref, knv_scr, vt_scr, key_scr,
              dm_scr, seq, topk):
    SB = DSA_SB
    n_sb = KC // SB

    def over_chunks(body, init):
        return lax.fori_loop(0, nchunk, body, init)

    qpos = q0 + lax.broadcasted_iota(I32, (SB, TQ), 1)

    def idx_body(c, carry):
        for sb in range(n_sb):
            r0 = pl.multiple_of(c * KC + sb * SB, SB)
            logit = jnp.dot(ikw_ref[pl.ds(r0, SB), :], rhs_idx, preferred_element_type=F32)
            sc = ((jnp.maximum(logit[:, 0:TQ], 0.0) * wq[0:1, :]
                   + jnp.maximum(logit[:, TQ:2 * TQ], 0.0) * wq[1:2, :])
                  + (jnp.maximum(logit[:, 2 * TQ:3 * TQ], 0.0) * wq[2:3, :]
                     + jnp.maximum(logit[:, 3 * TQ:4 * TQ], 0.0) * wq[3:4, :]))
            bits = lax.bitcast_convert_type(sc, I32)
            key = bits ^ ((bits >> 31) & jnp.int32(0x7FFFFFFF))
            key = jnp.where(key == -1, 0, key)
            krow = r0 + lax.broadcasted_iota(I32, (SB, TQ), 0)
            allowed = (krow >> 6) <= (qpos >> 6)
            key_scr[pl.ds(r0, SB), :] = jnp.where(allowed, key, jnp.int32(INT_MIN))
        return carry

    over_chunks(idx_body, 0)

    kf = jnp.float32(topk)

    def count(n, scr, pred):
        acc = jnp.zeros((SUBLANES, TQ), F32)
        for c in range(n):
            acc = acc + _fold_rows(jnp.where(pred(scr[c * KC:(c + 1) * KC, :]), 1.0, 0.0), jnp.add)
        return jnp.sum(acc, axis=0, keepdims=True)

    def bit_search(nbits, step):
        def variant(n):
            def run():
                return lax.fori_loop(0, nbits, lambda it, u: step(n, it, u), jnp.zeros((1, TQ), I32))
            return run
        return lax.switch(nchunk - 1, [variant(n) for n in range(1, seq // KC + 1)])

    def tau_step(n, it, u):
        cand_u = u | jnp.left_shift(jnp.int32(1), 31 - it)
        cand = cand_u ^ jnp.int32(INT_MIN)
        cnt = count(n, key_scr, lambda blk: blk >= cand)
        return jnp.where(cnt >= kf, cand_u, u)

    tau = bit_search(32, tau_step) ^ jnp.int32(INT_MIN)

    def gt_body(c, acc):
        key = key_scr[pl.ds(pl.multiple_of(c * KC, KC), KC), :]
        return acc + _fold_rows(jnp.where(key > tau, 1.0, 0.0), jnp.add)

    n_gt = over_chunks(gt_body, jnp.zeros((SUBLANES, TQ), F32))
    need = kf - jnp.sum(n_gt, axis=0, keepdims=True)
    need = jnp.where(tau == jnp.int32(INT_MIN), 0.0, need)

    def dm_body(c, ties_before):
        for sb in range(n_sb):
            r0 = pl.multiple_of(c * KC + sb * SB, SB)
            krow = r0 + lax.broadcasted_iota(I32, (SB, TQ), 0)
            dist = jnp.abs(qpos - krow).astype(F32)
            key = key_scr[pl.ds(r0, SB), :]
            is_tie = key == tau
            tie01 = jnp.where(is_tie, 1.0, 0.0)
            rank = ties_before + jnp.dot(tri_ref[...], tie01.astype(BF16), preferred_element_type=F32)
            inner = jnp.where(is_tie, jnp.where(rank <= need, dist, jnp.inf), jnp.inf)
            dm_scr[pl.ds(r0, SB), :] = jnp.where(key > tau, dist, inner)
            ties_before = ties_before + jnp.sum(_fold_rows(tie01, jnp.add), axis=0, keepdims=True)
        return ties_before

    over_chunks(dm_body, jnp.zeros((1, TQ), F32))

    slopes = [s * LOG2E for s in SLOPES_D]

    def attn_body(c, carry):
        ms, ls, acc = list(carry[:4]), list(carry[4:8]), carry[8]
        atts = [jnp.dot(knv_scr[pl.ds(pl.multiple_of(c * KC + sb * SB, SB), SB), :], rhs_main,
                        preferred_element_type=F32) for sb in range(n_sb)]
        for sb in range(n_sb):
            r0 = pl.multiple_of(c * KC + sb * SB, SB)
            att = atts[sb]
            dm = dm_scr[pl.ds(r0, SB), :]
            alphas, probs = [], []
            for h in range(4):
                a = att[:, h * TQ:(h + 1) * TQ] - slopes[h] * dm
                m_new = jnp.maximum(ms[h], jnp.max(_fold_rows(a, jnp.maximum), axis=0, keepdims=True))
                m_use = jnp.where(m_new == -jnp.inf, 0.0, m_new)
                alpha = jnp.exp2(ms[h] - m_use)
                p = jnp.exp2(a - m_use)
                ls[h] = ls[h] * alpha + jnp.sum(_fold_rows(p, jnp.add), axis=0, keepdims=True)
                ms[h] = m_new
                alphas.append(alpha)
                probs.append(p.astype(BF16))
            pv = jnp.dot(vt_scr[c, :, sb * SB:(sb + 1) * SB], jnp.concatenate(probs, axis=1),
                         preferred_element_type=F32)
            acc = acc * jnp.concatenate(alphas, axis=1) + pv
        return (*ms, *ls, acc)

    neg = jnp.full((1, TQ), -jnp.inf, F32)
    zero = jnp.zeros((1, TQ), F32)
    res = over_chunks(attn_body, (neg,) * 4 + (zero,) * 4 + (jnp.zeros((DSA_HD, 4 * TQ), F32),))
    ls = res[4:8]
    out_t = res[8]
    o_t = jnp.concatenate([out_t[:, h * TQ:(h + 1) * TQ] * (1.0 / ls[h]) for h in range(4)], axis=0)
    return o_t.T


def _dsa(yd, qg, kg, p64, p64h):
    b, t, _ = yd.shape
    TB = TQ * DSA_SUBTILES
    nq = t // TB
    nkc = t // KC
    W = GROUP_W
    topk = min(DSA_TOPK_MAX, t // 4)
    kern = functools.partial(_dsa_kernel, seq=t, topk=topk)
    small = lambda a: pl.BlockSpec(a.shape, lambda bi, i: (0,) * a.ndim)
    tri = jnp.asarray(np.tril(np.ones((DSA_SB, DSA_SB), np.float32)), dtype=BF16)
    return pl.pallas_call(
        kern,
        grid=(b, nq),
        in_specs=[pl.BlockSpec((None, TB, W), lambda bi, i: (bi, i, 0)),
                  pl.BlockSpec((None, TB, W), lambda bi, i: (bi, i, 1)),
                  pl.BlockSpec((None, TB, 128), lambda bi, i: (bi, i, 5)),
                  pl.BlockSpec((None, TB, 128), lambda bi, i: (bi, i, 6)),
                  pl.BlockSpec((None, t, 128), lambda bi, i: (bi, 0, 4)),
                  pl.BlockSpec((None, t, 128), lambda bi, i: (bi, 0, 6)),
                  small(qg), small(kg), small(p64), small(p64h), small(tri)],
        out_specs=pl.BlockSpec((None, TB, W), lambda bi, i: (bi, i, 0)),
        out_shape=jax.ShapeDtypeStruct((b, t, W), BF16),
        scratch_shapes=[pltpu.VMEM((t, 128), BF16),
                        pltpu.VMEM((nkc, DSA_HD, KC), BF16),
                        pltpu.VMEM((t, TQ), I32),
                        pltpu.VMEM((t, TQ), F32)],
        compiler_params=pltpu.CompilerParams(dimension_semantics=("arbitrary", "arbitrary"),
                                             vmem_limit_bytes=VMEM_LIMIT),
        name="dsa",
    )(yd, yd, yd, yd, yd, yd, qg, kg, p64, p64h, tri)


def kernel(x, norm_g, w_in, conv_w, conv_b, gmlp_g, gmlp_ws, gmlp_b, diff_qg, diff_kg, diff_lam,
           diff_subg, dsa_qg, dsa_kg, w_out):
    b, t, d = x.shape
    depth = w_in.shape[0]
    p32 = _block_diag_mean(GROUP_W, 32)
    p64 = _block_diag_mean(GROUP_W, 64)
    p64h = _block_diag_mean(128, 64)
    w_in16 = _wprep(w_in)
    w_out16 = _wprep(w_out)
    xf = x.reshape(b * t, d)
    ys = _inproj(xf, norm_g[0].reshape(1, d), w_in16, 0)
    for l in range(depth):
        ya, yb, yc, yd = (a.reshape(b, t, a.shape[-1]) for a in ys)
        bfull = jnp.repeat(gmlp_b[l].T, GROUP_W // 4, axis=1)
        mab = _mixab(ya, yb, conv_w[l], conv_b[l].reshape(1, -1), gmlp_g[l].reshape(1, -1),
                     gmlp_ws[l], bfull, p64)
        lam_init = 0.8 - 0.6 * math.exp(-0.3 * l)
        mc = _diff(yc, jnp.tile(diff_qg[l], 8).reshape(1, -1), jnp.tile(diff_kg[l], 8).reshape(1, -1),
                   diff_lam[l], diff_subg[l].reshape(1, -1), p32, p64, lam_init)
        md = _dsa(yd, jnp.tile(dsa_qg[l], 4).reshape(1, -1), jnp.tile(dsa_kg[l], 2).reshape(1, -1),
                  p64, p64h)
        mixes = (mab.reshape(b * t, -1), mc.reshape(b * t, -1), md.reshape(b * t, -1))
        if l + 1 < depth:
            xf, *ys = _outin(xf, *mixes, w_out16, l, norm_g[l + 1].reshape(1, d), w_in16)
        else:
            xf = _outproj(xf, *mixes, w_out16, l)
    return xf.reshape(b, t, d)
```

```python
import functools
import math

import numpy as np
import jax
import jax.numpy as jnp
from jax import lax
from jax.experimental import pallas as pl
from jax.experimental.pallas import tpu as pltpu

F32 = jnp.float32
BF16 = jnp.bfloat16
I32 = jnp.int32

GROUP_W = 256
CHUNK = 64
CONV_W = 3
GMLP_BLOCK = 128
DIFF_QD = 32
DSA_HD = 64
IDX_HD = 32
IDX_HEADS = 4
DSA_TOPK_MAX = 256
EPS = 1e-6
LOG2E = math.log2(math.e)
INT_MIN = -2 ** 31
LANES = 128
SUBLANES = 8

TQ = 128
TQ_DIFF = 256
DIFF_SUBTILES = 2
DSA_SUBTILES = 4
DSA_SB = 256
MIX_SUBTILES = 4
KC = 512
TM = 512
HALO = 16
NHC = 8

_SLOPES = 2.0 ** (-8.0 * np.arange(1, 9) / 8.0)
SLOPES_C = [float(s) for s in _SLOPES[0::2]]
SLOPES_D = [float(s) for s in _SLOPES[1::2]]

VMEM_LIMIT = 56 * 1024 * 1024


def _block_diag_mean(width, seg):
    idx = np.arange(width) // seg
    return jnp.asarray((idx[:, None] == idx[None, :]).astype(np.float32) / seg, dtype=BF16)


def _seg_mean(x2, p):
    hi = x2.astype(BF16)
    lo = (x2 - hi.astype(F32)).astype(BF16)
    return (jnp.dot(hi, p, preferred_element_type=F32)
            + jnp.dot(lo, p, preferred_element_type=F32))


def _silu(z):
    return z * jax.nn.sigmoid(z)


def _gelu(x):
    return 0.5 * x * (1.0 + lax.erf(x * (2.0 ** -0.5)))


def _fold_rows(x, op, stop=SUBLANES):
    r = x.shape[0]
    while r > stop:
        r //= 2
        x = op(x[:r], x[r:])
    return x


def _fold_lanes(x, op):
    c = x.shape[1]
    while c > LANES:
        c //= 2
        x = op(x[:, :c], x[:, c:])
    return x


def _wprep_kernel(w_ref, o_ref, *, n_valid):
    col = pl.program_id(1) * LANES + lax.broadcasted_iota(I32, (1, LANES), 1)
    o_ref[...] = jnp.where(col < n_valid, w_ref[...], 0.0).astype(BF16)


def _wprep(w):
    depth, d, n = w.shape
    nt = pl.cdiv(n, LANES)
    return pl.pallas_call(
        functools.partial(_wprep_kernel, n_valid=n),
        grid=(depth, nt),
        in_specs=[pl.BlockSpec((None, d, LANES), lambda l, j: (l, 0, j))],
        out_specs=pl.BlockSpec((None, d, LANES), lambda l, j: (l, 0, j)),
        out_shape=jax.ShapeDtypeStruct((depth, d, nt * LANES), BF16),
        compiler_params=pltpu.CompilerParams(dimension_semantics=("arbitrary", "arbitrary")),
        name="wprep",
    )(w)


_A0, _B0, _C0, _D0 = 0, 4 * GROUP_W, 7 * GROUP_W, 11 * GROUP_W
_DQ, _DKV, _DZ, _DIQ, _DEND = _D0, _D0 + 256, _D0 + 384, _D0 + 640, _D0 + 896
YD_W = 896


_Y_WIDTHS = (_B0 - _A0, _C0 - _B0, _D0 - _C0, YD_W)


def _norm_project(x, g_ref, w_ref, ya_ref, yb_ref, yc_ref, yd_ref):
    ms = jnp.mean(x * x, axis=-1, keepdims=True)
    xn = (x * lax.rsqrt(ms + EPS) * g_ref[...]).astype(BF16)

    def proj(lo, hi):
        return jnp.dot(xn, w_ref[:, lo:hi], preferred_element_type=F32).astype(BF16)

    ya_ref[...] = proj(_A0, _B0)
    yb_ref[...] = proj(_B0, _C0)
    yc_ref[...] = proj(_C0, _D0)
    yd_ref[:, 0:256] = proj(_DQ, _DKV)
    yd_ref[:, 256:512] = proj(_DZ, _DIQ)
    yd_ref[:, 512:640] = proj(_DKV, _DZ)
    yd_ref[:, 640:896] = proj(_DIQ, _DEND)


def _mix_project(x_ref, mab_ref, mc_ref, md_ref, wo_ref):
    acc = x_ref[...]
    acc = acc + jnp.dot(mab_ref[...], wo_ref[0:2 * GROUP_W, :], preferred_element_type=F32)
    acc = acc + jnp.dot(mc_ref[...], wo_ref[2 * GROUP_W:3 * GROUP_W, :], preferred_element_type=F32)
    return acc + jnp.dot(md_ref[...], wo_ref[3 * GROUP_W:4 * GROUP_W, :], preferred_element_type=F32)


def _inproj_kernel(x_ref, g_ref, w_ref, ya_ref, yb_ref, yc_ref, yd_ref):
    _norm_project(x_ref[...], g_ref, w_ref, ya_ref, yb_ref, yc_ref, yd_ref)


def _outproj_kernel(x_ref, mab_ref, mc_ref, md_ref, wo_ref, o_ref):
    o_ref[...] = _mix_project(x_ref, mab_ref, mc_ref, md_ref, wo_ref)


def _outin_kernel(x_ref, mab_ref, mc_ref, md_ref, wo_ref, g_ref, w_ref,
                  o_ref, ya_ref, yb_ref, yc_ref, yd_ref):
    x_new = _mix_project(x_ref, mab_ref, mc_ref, md_ref, wo_ref)
    o_ref[...] = x_new
    _norm_project(x_new, g_ref, w_ref, ya_ref, yb_ref, yc_ref, yd_ref)


def _row_spec(width):
    return pl.BlockSpec((TM, width), lambda i: (i, 0))


def _layer_spec(w, layer):
    return pl.BlockSpec((None,) + w.shape[1:], lambda i: (layer, 0, 0))


_PROJ_PARAMS = pltpu.CompilerParams(dimension_semantics=("arbitrary",), vmem_limit_bytes=VMEM_LIMIT)


def _inproj(xf, g, wb16, layer):
    m, d = xf.shape
    return pl.pallas_call(
        _inproj_kernel,
        grid=(m // TM,),
        in_specs=[_row_spec(d), pl.BlockSpec((1, d), lambda i: (0, 0)), _layer_spec(wb16, layer)],
        out_specs=[_row_spec(w) for w in _Y_WIDTHS],
        out_shape=[jax.ShapeDtypeStruct((m, w), BF16) for w in _Y_WIDTHS],
        compiler_params=_PROJ_PARAMS,
        name="inproj",
    )(xf, g, wb16)


def _outproj(xf, mab, mc, md, wo, layer):
    m, d = xf.shape
    return pl.pallas_call(
        _outproj_kernel,
        grid=(m // TM,),
        in_specs=[_row_spec(d), _row_spec(2 * GROUP_W), _row_spec(GROUP_W), _row_spec(GROUP_W),
                  _layer_spec(wo, layer)],
        out_specs=_row_spec(d),
        out_shape=jax.ShapeDtypeStruct((m, d), F32),
        compiler_params=_PROJ_PARAMS,
        name="outproj",
    )(xf, mab, mc, md, wo)


def _outin(xf, mab, mc, md, wo, layer, g_next, wb16):
    m, d = xf.shape
    return pl.pallas_call(
        _outin_kernel,
        grid=(m // TM,),
        in_specs=[_row_spec(d), _row_spec(2 * GROUP_W), _row_spec(GROUP_W), _row_spec(GROUP_W),
                  _layer_spec(wo, layer), pl.BlockSpec((1, d), lambda i: (0, 0)),
                  _layer_spec(wb16, layer + 1)],
        out_specs=[_row_spec(d)] + [_row_spec(w) for w in _Y_WIDTHS],
        out_shape=[jax.ShapeDtypeStruct((m, d), F32)]
                  + [jax.ShapeDtypeStruct((m, w), BF16) for w in _Y_WIDTHS],
        compiler_params=_PROJ_PARAMS,
        name="outin",
    )(xf, mab, mc, md, wo, g_next, wb16)


def _mixab_kernel(ya_ref, halo_ref, yb_ref, cw_ref, cb_ref, gg_ref, ws_ref, bfull_ref, p64_ref,
                  o_ref, u_scr):
    i = pl.program_id(1)
    W = GROUP_W
    TB = TQ * MIX_SUBTILES
    uh = halo_ref[:, 2 * W:3 * W].astype(F32) * halo_ref[:, 0:W].astype(F32)
    u_scr[0:HALO, :] = jnp.where(i > 0, uh, 0.0)
    u_scr[HALO:HALO + TB, :] = ya_ref[:, 2 * W:3 * W].astype(F32) * ya_ref[:, 0:W].astype(F32)
    for sb in range(MIX_SUBTILES):
        r0 = HALO + sb * TQ
        y = u_scr[r0 - 2:r0 - 2 + TQ, :] * cw_ref[0:1, :]
        y = y + u_scr[r0 - 1:r0 - 1 + TQ, :] * cw_ref[1:2, :]
        y = y + u_scr[r0:r0 + TQ, :] * cw_ref[2:3, :]
        rows = slice(sb * TQ, (sb + 1) * TQ)
        bg = ya_ref[rows, W:2 * W].astype(F32)
        za = ya_ref[rows, 3 * W:4 * W].astype(F32)
        o_ref[rows, 0:W] = (bg * (y + cb_ref[...]) * _silu(za)).astype(o_ref.dtype)

    t_idx = lax.broadcasted_iota(I32, (GMLP_BLOCK, GMLP_BLOCK), 0)
    s_idx = lax.broadcasted_iota(I32, (GMLP_BLOCK, GMLP_BLOCK), 1)
    causal = (t_idx >> 6) >= (s_idx >> 6)
    lane = lax.broadcasted_iota(I32, (1, W), 1)
    wms = [jnp.where(causal, ws_ref[hd], 0.0).astype(BF16) for hd in range(4)]

    def block(sb, carry):
        rows = pl.ds(pl.multiple_of(sb * TQ, TQ), TQ)
        u = _gelu(yb_ref[rows, 0:W].astype(F32))
        v = _gelu(yb_ref[rows, W:2 * W].astype(F32))
        zb = yb_ref[rows, 2 * W:3 * W].astype(F32)
        vn = v * lax.rsqrt(_seg_mean(v * v, p64_ref[...]) + EPS) * gg_ref[...]
        s = bfull_ref[...]
        for hd in range(4):
            vh = jnp.where((lane >> 6) == hd, vn, 0.0).astype(BF16)
            s = s + jnp.dot(wms[hd], vh, preferred_element_type=F32)
        o_ref[rows, W:2 * W] = (u * s * _silu(zb)).astype(o_ref.dtype)
        return carry

    lax.fori_loop(0, MIX_SUBTILES, block, 0)


def _mixab(ya, yb, cw, cb, gg, ws, bfull, p64):
    b, t, _ = ya.shape
    TB = TQ * MIX_SUBTILES
    nq = t // TB
    return pl.pallas_call(
        _mixab_kernel,
        grid=(b, nq),
        in_specs=[pl.BlockSpec((None, TB, 4 * GROUP_W), lambda bi, i: (bi, i, 0)),
                  pl.BlockSpec((None, HALO, 4 * GROUP_W),
                               lambda bi, i: (bi, jnp.maximum(i * (TB // HALO) - 1, 0), 0)),
                  pl.BlockSpec((None, TB, 3 * GROUP_W), lambda bi, i: (bi, i, 0)),
                  pl.BlockSpec(cw.shape, lambda bi, i: (0, 0)),
                  pl.BlockSpec(cb.shape, lambda bi, i: (0, 0)),
                  pl.BlockSpec(gg.shape, lambda bi, i: (0, 0)),
                  pl.BlockSpec(ws.shape, lambda bi, i: (0, 0, 0)),
                  pl.BlockSpec(bfull.shape, lambda bi, i: (0, 0)),
                  pl.BlockSpec(p64.shape, lambda bi, i: (0, 0))],
        out_specs=pl.BlockSpec((None, TB, 2 * GROUP_W), lambda bi, i: (bi, i, 0)),
        out_shape=jax.ShapeDtypeStruct((b, t, 2 * GROUP_W), BF16),
        scratch_shapes=[pltpu.VMEM((HALO + TB, GROUP_W), F32)],
        compiler_params=pltpu.CompilerParams(dimension_semantics=("arbitrary", "arbitrary"),
                                             vmem_limit_bytes=VMEM_LIMIT),
        name="mixab",
    )(ya, ya, yb, cw, cb, gg, ws, bfull, p64)


def _diff_kernel(q_ref, k_ref, v_ref, z_ref, qg_ref, kg_ref, lam_ref, subg_ref, p32_ref, p64_ref,
                 o_ref, kn_scr, qs_scr, s_scr, m_scr, l_scr, p_scr, acc_scr, *, lam_init, seq):
    TQ = TQ_DIFF
    i = pl.program_id(1)

    @pl.when(i == 0)
    def _():
        for c in range(seq // KC):
            kk = k_ref[c * KC:(c + 1) * KC, :].astype(F32)
            ms = _seg_mean(kk * kk, p32_ref[...])
            kn_scr[c * KC:(c + 1) * KC, :] = (kk * lax.rsqrt(ms + EPS) * kg_ref[...]).astype(BF16)

    lp = lam_ref[...]
    lam = (jnp.exp(jnp.sum(lp[0:1] * lp[1:2], axis=-1, keepdims=True))
           - jnp.exp(jnp.sum(lp[2:3] * lp[3:4], axis=-1, keepdims=True)) + lam_init)

    def sub_tile(sub, carry):
        rows = pl.ds(pl.multiple_of(sub * TQ, TQ), TQ)
        q0 = (i * DIFF_SUBTILES + sub) * TQ
        o = _diff_tile(q0, q_ref[rows, :].astype(F32), lam, v_ref, qg_ref, subg_ref, p32_ref, p64_ref,
                       kn_scr, qs_scr, s_scr, m_scr, l_scr, p_scr, acc_scr, lam_init)
        z = z_ref[rows, :].astype(F32)
        o_ref[rows, :] = (o * _silu(z)).astype(o_ref.dtype)
        return carry

    lax.fori_loop(0, DIFF_SUBTILES, sub_tile, 0)


def _diff_tile(q0, q, lam, v_ref, qg_ref, subg_ref, p32_ref, p64_ref,
               kn_scr, qs_scr, s_scr, m_scr, l_scr, p_scr, acc_scr, lam_init):
    TQ = TQ_DIFF
    nchunk = (q0 + TQ - 1) // KC + 1
    p32 = p32_ref[...]
    qn = q * lax.rsqrt(_seg_mean(q * q, p32) + EPS) * (qg_ref[...] * (DIFF_QD ** -0.5 * LOG2E))
    lane = lax.broadcasted_iota(I32, (1, GROUP_W), 1)
    for hc in range(NHC):
        qs_scr[hc * TQ:(hc + 1) * TQ, :] = jnp.where((lane >> 5) == hc, qn, 0.0).astype(BF16)

    m_scr[...] = jnp.full(m_scr.shape, -jnp.inf, F32)
    l_scr[...] = jnp.zeros(l_scr.shape, F32)
    acc_scr[...] = jnp.zeros(acc_scr.shape, F32)
    row = q0 + lax.broadcasted_iota(I32, (TQ, KC), 0)

    def s_body(c, carry):
        col = c * KC + lax.broadcasted_iota(I32, (TQ, KC), 1)
        dist = jnp.abs(row - col).astype(F32)
        dm = jnp.where((col >> 6) <= (row >> 6), dist, jnp.inf)
        kc = kn_scr[pl.ds(pl.multiple_of(c * KC, KC), KC), :]
        s_all = lax.dot_general(qs_scr[...], kc, (((1,), (1,)), ((), ())), preferred_element_type=F32)
        for h in range(NHC // 2):
            bias = (SLOPES_C[h] * LOG2E) * dm
            for j in range(2):
                rows = slice((2 * h + j) * TQ, (2 * h + j + 1) * TQ)
                s = s_all[rows, :] - bias
                s_scr[c, rows, :] = s
                m_scr[rows, :] = jnp.maximum(m_scr[rows, :], _fold_lanes(s, jnp.maximum))
        return carry

    lax.fori_loop(0, nchunk, s_body, 0)

    for hc in range(NHC):
        m = jnp.max(m_scr[hc * TQ:(hc + 1) * TQ, :], axis=1, keepdims=True)
        m_scr[hc * TQ:(hc + 1) * TQ, :] = jnp.broadcast_to(m, (TQ, LANES))

    def e_body(c, carry):
        for hc in range(NHC):
            rows = slice(hc * TQ, (hc + 1) * TQ)
            m = m_scr[rows, :]
            s = s_scr[c, rows, :]
            ps = [jnp.exp2(s[:, k * LANES:(k + 1) * LANES] - m) for k in range(KC // LANES)]
            l_scr[rows, :] += (ps[0] + ps[1]) + (ps[2] + ps[3])
            p_scr[rows, :] = jnp.concatenate(ps, axis=1).astype(BF16)
        vc = v_ref[pl.ds(pl.multiple_of(c * KC, KC), KC), :]
        acc_scr[...] += jnp.dot(p_scr[...], vc, preferred_element_type=F32)
        return carry

    lax.fori_loop(0, nchunk, e_body, 0)

    o = jnp.zeros((TQ, GROUP_W), F32)
    for h in range(NHC // 2):
        r1 = slice(2 * h * TQ, (2 * h + 1) * TQ)
        r2 = slice((2 * h + 1) * TQ, (2 * h + 2) * TQ)
        l1 = jnp.sum(l_scr[r1, :], axis=1, keepdims=True)
        l2 = jnp.sum(l_scr[r2, :], axis=1, keepdims=True)
        o_h = acc_scr[r1, :] * (1.0 / l1) - acc_scr[r2, :] * (lam / l2)
        o = jnp.where((lane >> 6) == h, o_h, o)

    ms = _seg_mean(o * o, p64_ref[...])
    return o * lax.rsqrt(ms + EPS) * (subg_ref[...] * (1.0 - lam_init))


def _diff(yc, qg, kg, lam_p, subg, p32, p64, lam_init):
    TQ = TQ_DIFF
    TB = TQ * DIFF_SUBTILES
    b, t, _ = yc.shape
    nq = t // TB
    nkc = t // KC
    W = GROUP_W
    kern = functools.partial(_diff_kernel, lam_init=lam_init, seq=t)
    small = lambda a: pl.BlockSpec(a.shape, lambda bi, i: (0,) * a.ndim)
    return pl.pallas_call(
        kern,
        grid=(b, nq),
        in_specs=[pl.BlockSpec((None, TB, W), lambda bi, i: (bi, i, 0)),
                  pl.BlockSpec((None, t, W), lambda bi, i: (bi, 0, 1)),
                  pl.BlockSpec((None, t, W), lambda bi, i: (bi, 0, 2)),
                  pl.BlockSpec((None, TB, W), lambda bi, i: (bi, i, 3)),
                  small(qg), small(kg), small(lam_p), small(subg), small(p32), small(p64)],
        out_specs=pl.BlockSpec((None, TB, W), lambda bi, i: (bi, i, 0)),
        out_shape=jax.ShapeDtypeStruct((b, t, W), BF16),
        scratch_shapes=[pltpu.VMEM((t, W), BF16),
                        pltpu.VMEM((NHC * TQ, W), BF16),
                        pltpu.VMEM((nkc, NHC * TQ, KC), F32),
                        pltpu.VMEM((NHC * TQ, LANES), F32),
                        pltpu.VMEM((NHC * TQ, LANES), F32),
                        pltpu.VMEM((NHC * TQ, KC), BF16),
                        pltpu.VMEM((NHC * TQ, W), F32)],
        compiler_params=pltpu.CompilerParams(dimension_semantics=("arbitrary", "arbitrary"),
                                             vmem_limit_bytes=VMEM_LIMIT),
        name="diffattn",
    )(yc, yc, yc, yc, qg, kg, lam_p, subg, p32, p64)


def _dsa_kernel(q_ref, z_ref, iq_ref, ikwq_ref, kv_ref, ikw_ref, qg_ref, kg_ref, p64_ref, p64h_ref,
                tri_ref, o_ref, knv_scr, vt_scr, key_scr, dm_scr, *, seq, topk):
    i = pl.program_id(1)

    @pl.when(i == 0)
    def _():
        p64h = p64h_ref[...]
        for c in range(seq // 128):
            blk = kv_ref[c * 128:(c + 1) * 128, :].astype(F32)
            ms = _seg_mean(blk * blk, p64h)
            knv_scr[c * 128:(c + 1) * 128, :] = (blk * lax.rsqrt(ms + EPS) * kg_ref[...]).astype(BF16)
            vt = blk.T
            cc, off = divmod(c * 128, KC)
            vt_scr[cc, :, off:off + 128] = vt[DSA_HD:2 * DSA_HD, :].astype(BF16)

    def sub_tile(nchunk, sub, carry):
        rows = pl.ds(pl.multiple_of(sub * TQ, TQ), TQ)
        q0 = (i * DSA_SUBTILES + sub) * TQ

        iq_t = iq_ref[rows, :].astype(F32).T
        iw_t = ikwq_ref[rows, :].astype(F32).T[IDX_HD:IDX_HD + 8, :]
        wq = iw_t * (IDX_HEADS ** -0.5 * IDX_HD ** -0.5)
        zpad_i = jnp.zeros((128 - IDX_HD, TQ), F32)
        rhs_idx = jnp.concatenate(
            [jnp.concatenate([iq_t[IDX_HD * h:IDX_HD * (h + 1), :], zpad_i], axis=0)
             for h in range(IDX_HEADS)], axis=1).astype(BF16)

        q = q_ref[rows, :].astype(F32)
        qn = q * lax.rsqrt(_seg_mean(q * q, p64_ref[...]) + EPS) * (qg_ref[...] * (DSA_HD ** -0.5 * LOG2E))
        qn_t = qn.T
        zpad_q = jnp.zeros((128 - DSA_HD, TQ), F32)
        rhs_main = jnp.concatenate(
            [jnp.concatenate([qn_t[DSA_HD * h:DSA_HD * (h + 1), :], zpad_q], axis=0) for h in range(4)],
            axis=1).astype(BF16)

        o = _dsa_tile(nchunk, q0, rhs_idx, wq, rhs_main, ikw_ref, tri_ref, knv_scr, vt_scr, key_scr,
                      dm_scr, seq, topk)
        z = z_ref[rows, :].astype(F32)
        o_ref[rows, :] = (o * _silu(z)).astype(o_ref.dtype)
        return carry

    assert TQ * DSA_SUBTILES == KC
    for n in range(1, seq // KC + 1):
        @pl.when(i == n - 1)
        def _(n=n):
            lax.fori_loop(0, DSA_SUBTILES, functools.partial(sub_tile, n), 0)


def _dsa_tile(nchunk, q0, rhs_idx, wq, rhs_main, ikw_ref, tri_ref, knv_scr, vt_scr, key_scr,
              dm_scr, seq, topk):
    SB = DSA_SB
    n_sb = KC // SB

    def over_chunks(body, init):
        acc = init
        for c in range(nchunk):
            acc = body(c, acc)
        return acc

    qpos = q0 + lax.broadcasted_iota(I32, (SB, TQ), 1)

    def idx_body(c, carry):
        for sb in range(n_sb):
            r0 = c * KC + sb * SB
            logit = jnp.dot(ikw_ref[pl.ds(r0, SB), :], rhs_idx, preferred_element_type=F32)
            sc = ((jnp.maximum(logit[:, 0:TQ], 0.0) * wq[0:1, :]
                   + jnp.maximum(logit[:, TQ:2 * TQ], 0.0) * wq[1:2, :])
                  + (jnp.maximum(logit[:, 2 * TQ:3 * TQ], 0.0) * wq[2:3, :]
                     + jnp.maximum(logit[:, 3 * TQ:4 * TQ], 0.0) * wq[3:4, :]))
            bits = lax.bitcast_convert_type(sc, I32)
            key = bits ^ ((bits >> 31) & jnp.int32(0x7FFFFFFF))
            key = jnp.where(key == -1, 0, key)
            krow = r0 + lax.broadcasted_iota(I32, (SB, TQ), 0)
            allowed = (krow >> 6) <= (qpos >> 6)
            key_scr[pl.ds(r0, SB), :] = jnp.where(allowed, key, jnp.int32(INT_MIN))
        return carry

    over_chunks(idx_body, 0)

    kf = jnp.float32(topk)

    def count(pred):
        acc = jnp.zeros((SUBLANES, TQ), F32)
        for c in range(nchunk):
            acc = acc + _fold_rows(jnp.where(pred(key_scr[c * KC:(c + 1) * KC, :]), 1.0, 0.0), jnp.add)
        return jnp.sum(acc, axis=0, keepdims=True)

    def tau_step(it, u):
        cand_u = u | jnp.left_shift(jnp.int32(1), 31 - it)
        cand = cand_u ^ jnp.int32(INT_MIN)
        return jnp.where(count(lambda blk: blk >= cand) >= kf, cand_u, u)

    tau = lax.fori_loop(0, 32, tau_step, jnp.zeros((1, TQ), I32)) ^ jnp.int32(INT_MIN)

    def gt_body(c, acc):
        key = key_scr[c * KC:(c + 1) * KC, :]
        return acc + _fold_rows(jnp.where(key > tau, 1.0, 0.0), jnp.add)

    n_gt = over_chunks(gt_body, jnp.zeros((SUBLANES, TQ), F32))
    need = kf - jnp.sum(n_gt, axis=0, keepdims=True)
    need = jnp.where(tau == jnp.int32(INT_MIN), 0.0, need)

    def dm_body(c, ties_before):
        for sb in range(n_sb):
            r0 = c * KC + sb * SB
            krow = r0 + lax.broadcasted_iota(I32, (SB, TQ), 0)
            dist = jnp.abs(qpos - krow).astype(F32)
            key = key_scr[pl.ds(r0, SB), :]
            is_tie = key == tau
            tie01 = jnp.where(is_tie, 1.0, 0.0)
            rank = ties_before + jnp.dot(tri_ref[...], tie01.astype(BF16), preferred_element_type=F32)
            inner = jnp.where(is_tie, jnp.where(rank <= need, dist, jnp.inf), jnp.inf)
            dm_scr[pl.ds(r0, SB), :] = jnp.where(key > tau, dist, inner)
            ties_before = ties_before + jnp.sum(_fold_rows(tie01, jnp.add), axis=0, keepdims=True)
        return ties_before

    over_chunks(dm_body, jnp.zeros((1, TQ), F32))

    slopes = [s * LOG2E for s in SLOPES_D]

    def attn_body(c, carry):
        ms, ls, acc = list(carry[:4]), list(carry[4:8]), carry[8]
        atts = [jnp.dot(knv_scr[pl.ds(c * KC + sb * SB, SB), :], rhs_main,
                        preferred_element_type=F32) for sb in range(n_sb)]
        for sb in range(n_sb):
            r0 = c * KC + sb * SB
            att = atts[sb]
            dm = dm_scr[pl.ds(r0, SB), :]
            alphas, probs = [], []
            for h in range(4):
                a = att[:, h * TQ:(h + 1) * TQ] - slopes[h] * dm
                m_new = jnp.maximum(ms[h], jnp.max(_fold_rows(a, jnp.maximum), axis=0, keepdims=True))
                m_use = jnp.where(m_new == -jnp.inf, 0.0, m_new)
                alpha = jnp.exp2(ms[h] - m_use)
                p = jnp.exp2(a - m_use)
                ls[h] = ls[h] * alpha + jnp.sum(_fold_rows(p, jnp.add), axis=0, keepdims=True)
                ms[h] = m_new
                alphas.append(alpha)
                probs.append(p.astype(BF16))
            pv = jnp.dot(vt_scr[c, :, sb * SB:(sb + 1) * SB], jnp.concatenate(probs, axis=1),
                         preferred_element_type=F32)
            acc = acc * jnp.concatenate(alphas, axis=1) + pv
        return (*ms, *ls, acc)

    neg = jnp.full((1, TQ), -jnp.inf, F32)
    zero = jnp.zeros((1, TQ), F32)
    res = over_chunks(attn_body, (neg,) * 4 + (zero,) * 4 + (jnp.zeros((DSA_HD, 4 * TQ), F32),))
    ls = res[4:8]
    out_t = res[8]
    o_t = jnp.concatenate([out_t[:, h * TQ:(h + 1) * TQ] * (1.0 / ls[h]) for h in range(4)], axis=0)
    return o_t.T


def _dsa(yd, qg, kg, p64, p64h):
    b, t, _ = yd.shape
    TB = TQ * DSA_SUBTILES
    nq = t // TB
    nkc = t // KC
    W = GROUP_W
    topk = min(DSA_TOPK_MAX, t // 4)
    kern = functools.partial(_dsa_kernel, seq=t, topk=topk)
    small = lambda a: pl.BlockSpec(a.shape, lambda bi, i: (0,) * a.ndim)
    tri = jnp.asarray(np.tril(np.ones((DSA_SB, DSA_SB), np.float32)), dtype=BF16)
    return pl.pallas_call(
        kern,
        grid=(b, nq),
        in_specs=[pl.BlockSpec((None, TB, W), lambda bi, i: (bi, i, 0)),
                  pl.BlockSpec((None, TB, W), lambda bi, i: (bi, i, 1)),
                  pl.BlockSpec((None, TB, 128), lambda bi, i: (bi, i, 5)),
                  pl.BlockSpec((None, TB, 128), lambda bi, i: (bi, i, 6)),
                  pl.BlockSpec((None, t, 128), lambda bi, i: (bi, 0, 4)),
                  pl.BlockSpec((None, t, 128), lambda bi, i: (bi, 0, 6)),
                  small(qg), small(kg), small(p64), small(p64h), small(tri)],
        out_specs=pl.BlockSpec((None, TB, W), lambda bi, i: (bi, i, 0)),
        out_shape=jax.ShapeDtypeStruct((b, t, W), BF16),
        scratch_shapes=[pltpu.VMEM((t, 128), BF16),
                        pltpu.VMEM((nkc, DSA_HD, KC), BF16),
                        pltpu.VMEM((t, TQ), I32),
                        pltpu.VMEM((t, TQ), F32)],
        compiler_params=pltpu.CompilerParams(dimension_semantics=("arbitrary", "arbitrary"),
                                             vmem_limit_bytes=VMEM_LIMIT),
        name="dsa",
    )(yd, yd, yd, yd, yd, yd, qg, kg, p64, p64h, tri)


def kernel(x, norm_g, w_in, conv_w, conv_b, gmlp_g, gmlp_ws, gmlp_b, diff_qg, diff_kg, diff_lam,
           diff_subg, dsa_qg, dsa_kg, w_out):
    b, t, d = x.shape
    depth = w_in.shape[0]
    p32 = _block_diag_mean(GROUP_W, 32)
    p64 = _block_diag_mean(GROUP_W, 64)
    p64h = _block_diag_mean(128, 64)
    w_in16 = _wprep(w_in)
    w_out16 = _wprep(w_out)
    xf = x.reshape(b * t, d)
    ys = _inproj(xf, norm_g[0].reshape(1, d), w_in16, 0)
    for l in range(depth):
        ya, yb, yc, yd = (a.reshape(b, t, a.shape[-1]) for a in ys)
        bfull = jnp.repeat(gmlp_b[l].T, GROUP_W // 4, axis=1)
        mab = _mixab(ya, yb, conv_w[l], conv_b[l].reshape(1, -1), gmlp_g[l].reshape(1, -1),
                     gmlp_ws[l], bfull, p64)
        lam_init = 0.8 - 0.6 * math.exp(-0.3 * l)
        mc = _diff(yc, jnp.tile(diff_qg[l], 8).reshape(1, -1), jnp.tile(diff_kg[l], 8).reshape(1, -1),
                   diff_lam[l], diff_subg[l].reshape(1, -1), p32, p64, lam_init)
        md = _dsa(yd, jnp.tile(dsa_qg[l], 4).reshape(1, -1), jnp.tile(dsa_kg[l], 2).reshape(1, -1),
                  p64, p64h)
        mixes = (mab.reshape(b * t, -1), mc.reshape(b * t, -1), md.reshape(b * t, -1))
        if l + 1 < depth:
            xf, *ys = _outin(xf, *mixes, w_out16, l, norm_g[l + 1].reshape(1, d), w_in16)
        else:
            xf = _outproj(xf, *mixes, w_out16, l)
    return xf.reshape(b, t, d)
```

```python
import functools
import math

import numpy as np
import jax
import jax.numpy as jnp
from jax import lax
from jax.experimental import pallas as pl
from jax.experimental.pallas import tpu as pltpu

F32 = jnp.float32
BF16 = jnp.bfloat16
I32 = jnp.int32

GROUP_W = 256
CHUNK = 64
CONV_W = 3
GMLP_BLOCK = 128
DIFF_QD = 32
DSA_HD = 64
IDX_HD = 32
IDX_HEADS = 4
DSA_TOPK_MAX = 256
EPS = 1e-6
LOG2E = math.log2(math.e)
INT_MIN = -2 ** 31
LANES = 128
SUBLANES = 8

TQ = 128
TQ_DIFF = 256
DIFF_SUBTILES = 2
DSA_SUBTILES = 4
DSA_SB = 256
MIX_SUBTILES = 4
KC = 512
KC_DIFF = 512
TM = 512
HALO = 16
NHC = 8

_SLOPES = 2.0 ** (-8.0 * np.arange(1, 9) / 8.0)
SLOPES_C = [float(s) for s in _SLOPES[0::2]]
SLOPES_D = [float(s) for s in _SLOPES[1::2]]

VMEM_LIMIT = 56 * 1024 * 1024


def _block_diag_mean(width, seg):
    idx = np.arange(width) // seg
    return jnp.asarray((idx[:, None] == idx[None, :]).astype(np.float32) / seg, dtype=BF16)


def _seg_mean(x2, p):
    hi = x2.astype(BF16)
    lo = (x2 - hi.astype(F32)).astype(BF16)
    return (jnp.dot(hi, p, preferred_element_type=F32)
            + jnp.dot(lo, p, preferred_element_type=F32))


def _silu(z):
    return z * jax.nn.sigmoid(z)


def _gelu(x):
    return 0.5 * x * (1.0 + lax.erf(x * (2.0 ** -0.5)))


def _fold_rows(x, op, stop=SUBLANES):
    r = x.shape[0]
    while r > stop:
        r //= 2
        x = op(x[:r], x[r:])
    return x


def _fold_lanes(x, op):
    c = x.shape[1]
    while c > LANES:
        c //= 2
        x = op(x[:, :c], x[:, c:])
    return x


def _wprep_kernel(w_ref, o_ref, *, n_valid):
    col = pl.program_id(1) * LANES + lax.broadcasted_iota(I32, (1, LANES), 1)
    o_ref[...] = jnp.where(col < n_valid, w_ref[...], 0.0).astype(BF16)


def _wprep(w):
    depth, d, n = w.shape
    nt = pl.cdiv(n, LANES)
    return pl.pallas_call(
        functools.partial(_wprep_kernel, n_valid=n),
        grid=(depth, nt),
        in_specs=[pl.BlockSpec((None, d, LANES), lambda l, j: (l, 0, j))],
        out_specs=pl.BlockSpec((None, d, LANES), lambda l, j: (l, 0, j)),
        out_shape=jax.ShapeDtypeStruct((depth, d, nt * LANES), BF16),
        compiler_params=pltpu.CompilerParams(dimension_semantics=("arbitrary", "arbitrary")),
        name="wprep",
    )(w)


_A0, _B0, _C0, _D0 = 0, 4 * GROUP_W, 7 * GROUP_W, 11 * GROUP_W
_DQ, _DKV, _DZ, _DIQ, _DEND = _D0, _D0 + 256, _D0 + 384, _D0 + 640, _D0 + 896
YD_W = 896


_Y_WIDTHS = (_B0 - _A0, _C0 - _B0, _D0 - _C0, YD_W)


def _norm_project(x, g_ref, w_ref, ya_ref, yb_ref, yc_ref, yd_ref):
    ms = jnp.mean(x * x, axis=-1, keepdims=True)
    xn = (x * lax.rsqrt(ms + EPS) * g_ref[...]).astype(BF16)

    def proj(lo, hi):
        return jnp.dot(xn, w_ref[:, lo:hi], preferred_element_type=F32).astype(BF16)

    ya_ref[...] = proj(_A0, _B0)
    yb_ref[...] = proj(_B0, _C0)
    yc_ref[...] = proj(_C0, _D0)
    yd_ref[:, 0:256] = proj(_DQ, _DKV)
    yd_ref[:, 256:512] = proj(_DZ, _DIQ)
    yd_ref[:, 512:640] = proj(_DKV, _DZ)
    yd_ref[:, 640:896] = proj(_DIQ, _DEND)


def _mix_project(x_ref, mab_ref, mc_ref, md_ref, wo_ref):
    acc = x_ref[...]
    acc = acc + jnp.dot(mab_ref[...], wo_ref[0:2 * GROUP_W, :], preferred_element_type=F32)
    acc = acc + jnp.dot(mc_ref[...], wo_ref[2 * GROUP_W:3 * GROUP_W, :], preferred_element_type=F32)
    return acc + jnp.dot(md_ref[...], wo_ref[3 * GROUP_W:4 * GROUP_W, :], preferred_element_type=F32)


def _inproj_kernel(x_ref, g_ref, w_ref, ya_ref, yb_ref, yc_ref, yd_ref):
    _norm_project(x_ref[...], g_ref, w_ref, ya_ref, yb_ref, yc_ref, yd_ref)


def _outproj_kernel(x_ref, mab_ref, mc_ref, md_ref, wo_ref, o_ref):
    o_ref[...] = _mix_project(x_ref, mab_ref, mc_ref, md_ref, wo_ref)


def _outin_kernel(x_ref, mab_ref, mc_ref, md_ref, wo_ref, g_ref, w_ref,
                  o_ref, ya_ref, yb_ref, yc_ref, yd_ref):
    x_new = _mix_project(x_ref, mab_ref, mc_ref, md_ref, wo_ref)
    o_ref[...] = x_new
    _norm_project(x_new, g_ref, w_ref, ya_ref, yb_ref, yc_ref, yd_ref)


def _row_spec(width):
    return pl.BlockSpec((TM, width), lambda i: (i, 0))


def _layer_spec(w, layer):
    return pl.BlockSpec((None,) + w.shape[1:], lambda i: (layer, 0, 0))


_PROJ_PARAMS = pltpu.CompilerParams(dimension_semantics=("arbitrary",), vmem_limit_bytes=VMEM_LIMIT)


def _inproj(xf, g, wb16, layer):
    m, d = xf.shape
    return pl.pallas_call(
        _inproj_kernel,
        grid=(m // TM,),
        in_specs=[_row_spec(d), pl.BlockSpec((1, d), lambda i: (0, 0)), _layer_spec(wb16, layer)],
        out_specs=[_row_spec(w) for w in _Y_WIDTHS],
        out_shape=[jax.ShapeDtypeStruct((m, w), BF16) for w in _Y_WIDTHS],
        compiler_params=_PROJ_PARAMS,
        name="inproj",
    )(xf, g, wb16)


def _outproj(xf, mab, mc, md, wo, layer):
    m, d = xf.shape
    return pl.pallas_call(
        _outproj_kernel,
        grid=(m // TM,),
        in_specs=[_row_spec(d), _row_spec(2 * GROUP_W), _row_spec(GROUP_W), _row_spec(GROUP_W),
                  _layer_spec(wo, layer)],
        out_specs=_row_spec(d),
        out_shape=jax.ShapeDtypeStruct((m, d), F32),
        compiler_params=_PROJ_PARAMS,
        name="outproj",
    )(xf, mab, mc, md, wo)


def _outin(xf, mab, mc, md, wo, layer, g_next, wb16):
    m, d = xf.shape
    return pl.pallas_call(
        _outin_kernel,
        grid=(m // TM,),
        in_specs=[_row_spec(d), _row_spec(2 * GROUP_W), _row_spec(GROUP_W), _row_spec(GROUP_W),
                  _layer_spec(wo, layer), pl.BlockSpec((1, d), lambda i: (0, 0)),
                  _layer_spec(wb16, layer + 1)],
        out_specs=[_row_spec(d)] + [_row_spec(w) for w in _Y_WIDTHS],
        out_shape=[jax.ShapeDtypeStruct((m, d), F32)]
                  + [jax.ShapeDtypeStruct((m, w), BF16) for w in _Y_WIDTHS],
        compiler_params=_PROJ_PARAMS,
        name="outin",
    )(xf, mab, mc, md, wo, g_next, wb16)


def _mixab_kernel(ya_ref, halo_ref, yb_ref, cw_ref, cb_ref, gg_ref, ws_ref, bfull_ref, p64_ref,
                  o_ref, u_scr):
    i = pl.program_id(1)
    W = GROUP_W
    TB = TQ * MIX_SUBTILES
    uh = halo_ref[:, 2 * W:3 * W].astype(F32) * halo_ref[:, 0:W].astype(F32)
    u_scr[0:HALO, :] = jnp.where(i > 0, uh, 0.0)
    u_scr[HALO:HALO + TB, :] = ya_ref[:, 2 * W:3 * W].astype(F32) * ya_ref[:, 0:W].astype(F32)
    for sb in range(MIX_SUBTILES):
        r0 = HALO + sb * TQ
        y = u_scr[r0 - 2:r0 - 2 + TQ, :] * cw_ref[0:1, :]
        y = y + u_scr[r0 - 1:r0 - 1 + TQ, :] * cw_ref[1:2, :]
        y = y + u_scr[r0:r0 + TQ, :] * cw_ref[2:3, :]
        rows = slice(sb * TQ, (sb + 1) * TQ)
        bg = ya_ref[rows, W:2 * W].astype(F32)
        za = ya_ref[rows, 3 * W:4 * W].astype(F32)
        o_ref[rows, 0:W] = (bg * (y + cb_ref[...]) * _silu(za)).astype(o_ref.dtype)

    t_idx = lax.broadcasted_iota(I32, (GMLP_BLOCK, GMLP_BLOCK), 0)
    s_idx = lax.broadcasted_iota(I32, (GMLP_BLOCK, GMLP_BLOCK), 1)
    causal = (t_idx >> 6) >= (s_idx >> 6)
    lane = lax.broadcasted_iota(I32, (1, W), 1)
    wms = [jnp.where(causal, ws_ref[hd], 0.0).astype(BF16) for hd in range(4)]

    def block(sb, carry):
        rows = pl.ds(pl.multiple_of(sb * TQ, TQ), TQ)
        u = _gelu(yb_ref[rows, 0:W].astype(F32))
        v = _gelu(yb_ref[rows, W:2 * W].astype(F32))
        zb = yb_ref[rows, 2 * W:3 * W].astype(F32)
        vn = v * lax.rsqrt(_seg_mean(v * v, p64_ref[...]) + EPS) * gg_ref[...]
        s = bfull_ref[...]
        for hd in range(4):
            vh = jnp.where((lane >> 6) == hd, vn, 0.0).astype(BF16)
            s = s + jnp.dot(wms[hd], vh, preferred_element_type=F32)
        o_ref[rows, W:2 * W] = (u * s * _silu(zb)).astype(o_ref.dtype)
        return carry

    lax.fori_loop(0, MIX_SUBTILES, block, 0)


def _mixab(ya, yb, cw, cb, gg, ws, bfull, p64):
    b, t, _ = ya.shape
    TB = TQ * MIX_SUBTILES
    nq = t // TB
    return pl.pallas_call(
        _mixab_kernel,
        grid=(b, nq),
        in_specs=[pl.BlockSpec((None, TB, 4 * GROUP_W), lambda bi, i: (bi, i, 0)),
                  pl.BlockSpec((None, HALO, 4 * GROUP_W),
                               lambda bi, i: (bi, jnp.maximum(i * (TB // HALO) - 1, 0), 0)),
                  pl.BlockSpec((None, TB, 3 * GROUP_W), lambda bi, i: (bi, i, 0)),
                  pl.BlockSpec(cw.shape, lambda bi, i: (0, 0)),
                  pl.BlockSpec(cb.shape, lambda bi, i: (0, 0)),
                  pl.BlockSpec(gg.shape, lambda bi, i: (0, 0)),
                  pl.BlockSpec(ws.shape, lambda bi, i: (0, 0, 0)),
                  pl.BlockSpec(bfull.shape, lambda bi, i: (0, 0)),
                  pl.BlockSpec(p64.shape, lambda bi, i: (0, 0))],
        out_specs=pl.BlockSpec((None, TB, 2 * GROUP_W), lambda bi, i: (bi, i, 0)),
        out_shape=jax.ShapeDtypeStruct((b, t, 2 * GROUP_W), BF16),
        scratch_shapes=[pltpu.VMEM((HALO + TB, GROUP_W), F32)],
        compiler_params=pltpu.CompilerParams(dimension_semantics=("arbitrary", "arbitrary"),
                                             vmem_limit_bytes=VMEM_LIMIT),
        name="mixab",
    )(ya, ya, yb, cw, cb, gg, ws, bfull, p64)


def _diff_kernel(q_ref, k_ref, v_ref, z_ref, qg_ref, kg_ref, lam_ref, subg_ref, p32_ref, p64_ref,
                 o_ref, kn_scr, qs_scr, s_scr, m_scr, l_scr, p_scr, acc_scr, *, lam_init, seq):
    TQ = TQ_DIFF
    i = pl.program_id(1)

    @pl.when(i == 0)
    def _():
        for c in range(seq // KC):
            kk = k_ref[c * KC:(c + 1) * KC, :].astype(F32)
            ms = _seg_mean(kk * kk, p32_ref[...])
            kn_scr[c * KC:(c + 1) * KC, :] = (kk * lax.rsqrt(ms + EPS) * kg_ref[...]).astype(BF16)

    lp = lam_ref[...]
    lam = (jnp.exp(jnp.sum(lp[0:1] * lp[1:2], axis=-1, keepdims=True))
           - jnp.exp(jnp.sum(lp[2:3] * lp[3:4], axis=-1, keepdims=True)) + lam_init)

    def sub_tile(sub, carry):
        rows = pl.ds(pl.multiple_of(sub * TQ, TQ), TQ)
        q0 = (i * DIFF_SUBTILES + sub) * TQ
        o = _diff_tile(q0, q_ref[rows, :].astype(F32), lam, v_ref, qg_ref, subg_ref, p32_ref, p64_ref,
                       kn_scr, qs_scr, s_scr, m_scr, l_scr, p_scr, acc_scr, lam_init)
        z = z_ref[rows, :].astype(F32)
        o_ref[rows, :] = (o * _silu(z)).astype(o_ref.dtype)
        return carry

    lax.fori_loop(0, DIFF_SUBTILES, sub_tile, 0)


def _diff_tile(q0, q, lam, v_ref, qg_ref, subg_ref, p32_ref, p64_ref,
               kn_scr, qs_scr, s_scr, m_scr, l_scr, p_scr, acc_scr, lam_init):
    TQ = TQ_DIFF
    KC = KC_DIFF
    nchunk = (q0 + TQ - 1) // KC + 1
    p32 = p32_ref[...]
    qn = q * lax.rsqrt(_seg_mean(q * q, p32) + EPS) * (qg_ref[...] * (DIFF_QD ** -0.5 * LOG2E))
    lane = lax.broadcasted_iota(I32, (1, GROUP_W), 1)
    for hc in range(NHC):
        qs_scr[hc * TQ:(hc + 1) * TQ, :] = jnp.where((lane >> 5) == hc, qn, 0.0).astype(BF16)

    m_scr[...] = jnp.full(m_scr.shape, -jnp.inf, F32)
    l_scr[...] = jnp.zeros(l_scr.shape, F32)
    acc_scr[...] = jnp.zeros(acc_scr.shape, F32)
    row = q0 + lax.broadcasted_iota(I32, (TQ, KC), 0)

    def s_body(c, carry):
        col = c * KC + lax.broadcasted_iota(I32, (TQ, KC), 1)
        dist = jnp.abs(row - col).astype(F32)
        dm = jnp.where((col >> 6) <= (row >> 6), dist, jnp.inf)
        kc = kn_scr[pl.ds(pl.multiple_of(c * KC, KC), KC), :]
        s_all = lax.dot_general(qs_scr[...], kc, (((1,), (1,)), ((), ())), preferred_element_type=F32)
        for h in range(NHC // 2):
            bias = (SLOPES_C[h] * LOG2E) * dm
            for j in range(2):
                rows = slice((2 * h + j) * TQ, (2 * h + j + 1) * TQ)
                s = s_all[rows, :] - bias
                s_scr[c, rows, :] = s
                m_scr[rows, :] = jnp.maximum(m_scr[rows, :], _fold_lanes(s, jnp.maximum))
        return carry

    lax.fori_loop(0, nchunk, s_body, 0)

    for hc in range(NHC):
        m = jnp.max(m_scr[hc * TQ:(hc + 1) * TQ, :], axis=1, keepdims=True)
        m_scr[hc * TQ:(hc + 1) * TQ, :] = jnp.broadcast_to(m, (TQ, LANES))

    def e_body(c, carry):
        for hc in range(NHC):
            rows = slice(hc * TQ, (hc + 1) * TQ)
            m = m_scr[rows, :]
            s = s_scr[c, rows, :]
            ps = [jnp.exp2(s[:, k * LANES:(k + 1) * LANES] - m) for k in range(KC // LANES)]
            l_scr[rows, :] += _fold_lanes(jnp.concatenate(ps, axis=1), jnp.add)
            p_scr[rows, :] = jnp.concatenate(ps, axis=1).astype(BF16)
        vc = v_ref[pl.ds(pl.multiple_of(c * KC, KC), KC), :]
        acc_scr[...] += jnp.dot(p_scr[...], vc, preferred_element_type=F32)
        return carry

    lax.fori_loop(0, nchunk, e_body, 0)

    o = jnp.zeros((TQ, GROUP_W), F32)
    for h in range(NHC // 2):
        r1 = slice(2 * h * TQ, (2 * h + 1) * TQ)
        r2 = slice((2 * h + 1) * TQ, (2 * h + 2) * TQ)
        l1 = jnp.sum(l_scr[r1, :], axis=1, keepdims=True)
        l2 = jnp.sum(l_scr[r2, :], axis=1, keepdims=True)
        o_h = acc_scr[r1, :] * (1.0 / l1) - acc_scr[r2, :] * (lam / l2)
        o = jnp.where((lane >> 6) == h, o_h, o)

    ms = _seg_mean(o * o, p64_ref[...])
    return o * lax.rsqrt(ms + EPS) * (subg_ref[...] * (1.0 - lam_init))


def _diff(yc, qg, kg, lam_p, subg, p32, p64, lam_init):
    TQ = TQ_DIFF
    KC = KC_DIFF
    TB = TQ * DIFF_SUBTILES
    b, t, _ = yc.shape
    nq = t // TB
    nkc = t // KC
    W = GROUP_W
    kern = functools.partial(_diff_kernel, lam_init=lam_init, seq=t)
    small = lambda a: pl.BlockSpec(a.shape, lambda bi, i: (0,) * a.ndim)
    return pl.pallas_call(
        kern,
        grid=(b, nq),
        in_specs=[pl.BlockSpec((None, TB, W), lambda bi, i: (bi, i, 0)),
                  pl.BlockSpec((None, t, W), lambda bi, i: (bi, 0, 1)),
                  pl.BlockSpec((None, t, W), lambda bi, i: (bi, 0, 2)),
                  pl.BlockSpec((None, TB, W), lambda bi, i: (bi, i, 3)),
                  small(qg), small(kg), small(lam_p), small(subg), small(p32), small(p64)],
        out_specs=pl.BlockSpec((None, TB, W), lambda bi, i: (bi, i, 0)),
        out_shape=jax.ShapeDtypeStruct((b, t, W), BF16),
        scratch_shapes=[pltpu.VMEM((t, W), BF16),
                        pltpu.VMEM((NHC * TQ, W), BF16),
                        pltpu.VMEM((nkc, NHC * TQ, KC), F32),
                        pltpu.VMEM((NHC * TQ, LANES), F32),
                        pltpu.VMEM((NHC * TQ, LANES), F32),
                        pltpu.VMEM((NHC * TQ, KC), BF16),
                        pltpu.VMEM((NHC * TQ, W), F32)],
        compiler_params=pltpu.CompilerParams(dimension_semantics=("arbitrary", "arbitrary"),
                                             vmem_limit_bytes=VMEM_LIMIT),
        name="diffattn",
    )(yc, yc, yc, yc, qg, kg, lam_p, subg, p32, p64)


def _dsa_kernel(q_ref, z_ref, iq_ref, ikwq_ref, kv_ref, ikw_ref, qg_ref, kg_ref, p64_ref, p64h_ref,
                tri_ref, o_ref, knv_scr, vt_scr, key_scr, dm_scr, *, seq, topk):
    i = pl.program_id(1)

    @pl.when(i == 0)
    def _():
        p64h = p64h_ref[...]
        for c in range(seq // 128):
            blk = kv_ref[c * 128:(c + 1) * 128, :].astype(F32)
            ms = _seg_mean(blk * blk, p64h)
            knv_scr[c * 128:(c + 1) * 128, :] = (blk * lax.rsqrt(ms + EPS) * kg_ref[...]).astype(BF16)
            vt = blk.T
            cc, off = divmod(c * 128, KC)
            vt_scr[cc, :, off:off + 128] = vt[DSA_HD:2 * DSA_HD, :].astype(BF16)

    def sub_tile(nchunk, sub, carry):
        rows = pl.ds(pl.multiple_of(sub * TQ, TQ), TQ)
        q0 = (i * DSA_SUBTILES + sub) * TQ

        iq_t = iq_ref[rows, :].astype(F32).T
        iw_t = ikwq_ref[rows, :].astype(F32).T[IDX_HD:IDX_HD + 8, :]
        wq = iw_t * (IDX_HEADS ** -0.5 * IDX_HD ** -0.5)
        zpad_i = jnp.zeros((128 - IDX_HD, TQ), F32)
        rhs_idx = jnp.concatenate(
            [jnp.concatenate([iq_t[IDX_HD * h:IDX_HD * (h + 1), :], zpad_i], axis=0)
             for h in range(IDX_HEADS)], axis=1).astype(BF16)

        q = q_ref[rows, :].astype(F32)
        qn = q * lax.rsqrt(_seg_mean(q * q, p64_ref[...]) + EPS) * (qg_ref[...] * (DSA_HD ** -0.5 * LOG2E))
        qn_t = qn.T
        zpad_q = jnp.zeros((128 - DSA_HD, TQ), F32)
        rhs_main = jnp.concatenate(
            [jnp.concatenate([qn_t[DSA_HD * h:DSA_HD * (h + 1), :], zpad_q], axis=0) for h in range(4)],
            axis=1).astype(BF16)

        o = _dsa_tile(nchunk, q0, rhs_idx, wq, rhs_main, ikw_ref, tri_ref, knv_scr, vt_scr, key_scr,
                      dm_scr, seq, topk)
        z = z_ref[rows, :].astype(F32)
        o_ref[rows, :] = (o * _silu(z)).astype(o_ref.dtype)
        return carry

    assert TQ * DSA_SUBTILES == KC
    for n in range(1, seq // KC + 1):
        @pl.when(i == n - 1)
        def _(n=n):
            lax.fori_loop(0, DSA_SUBTILES, functools.partial(sub_tile, n), 0)


def _dsa_tile(nchunk, q0, rhs_idx, wq, rhs_main, ikw_ref, tri_ref, knv_scr, vt_scr, key_scr,
              dm_scr, seq, topk):
    SB = DSA_SB
    n_sb = KC // SB

    def over_chunks(body, init):
        acc = init
        for c in range(nchunk):
            acc = body(c, acc)
        return acc

    qpos = q0 + lax.broadcasted_iota(I32, (SB, TQ), 1)

    def idx_body(c, carry):
        for sb in range(n_sb):
            r0 = c * KC + sb * SB
            logit = jnp.dot(ikw_ref[pl.ds(r0, SB), :], rhs_idx, preferred_element_type=F32)
            sc = ((jnp.maximum(logit[:, 0:TQ], 0.0) * wq[0:1, :]
                   + jnp.maximum(logit[:, TQ:2 * TQ], 0.0) * wq[1:2, :])
                  + (jnp.maximum(logit[:, 2 * TQ:3 * TQ], 0.0) * wq[2:3, :]
                     + jnp.maximum(logit[:, 3 * TQ:4 * TQ], 0.0) * wq[3:4, :]))
            bits = lax.bitcast_convert_type(sc, I32)
            key = bits ^ ((bits >> 31) & jnp.int32(0x7FFFFFFF))
            key = jnp.where(key == -1, 0, key)
            if c == nchunk - 1:
                krow = r0 + lax.broadcasted_iota(I32, (SB, TQ), 0)
                key = jnp.where((krow >> 6) <= (qpos >> 6), key, jnp.int32(INT_MIN))
            key_scr[pl.ds(r0, SB), :] = key
        return carry

    over_chunks(idx_body, 0)

    kf = jnp.float32(topk)

    def count(pred):
        acc = jnp.zeros((SUBLANES, TQ), F32)
        for c in range(nchunk):
            acc = acc + _fold_rows(jnp.where(pred(key_scr[c * KC:(c + 1) * KC, :]), 1.0, 0.0), jnp.add)
        return jnp.sum(acc, axis=0, keepdims=True)

    def tau_step(it, u):
        cand_u = u | jnp.left_shift(jnp.int32(1), 31 - it)
        cand = cand_u ^ jnp.int32(INT_MIN)
        return jnp.where(count(lambda blk: blk >= cand) >= kf, cand_u, u)

    def search():
        return lax.fori_loop(0, 32, tau_step, jnp.zeros((1, TQ), I32)) ^ jnp.int32(INT_MIN)

    if nchunk == 1:
        tau = lax.cond(q0 + TQ <= topk, lambda: jnp.full((1, TQ), INT_MIN, I32), search)
    else:
        tau = search()

    def gt_body(c, acc):
        key = key_scr[c * KC:(c + 1) * KC, :]
        return acc + _fold_rows(jnp.where(key > tau, 1.0, 0.0), jnp.add)

    n_gt = over_chunks(gt_body, jnp.zeros((SUBLANES, TQ), F32))
    need = kf - jnp.sum(n_gt, axis=0, keepdims=True)
    need = jnp.where(tau == jnp.int32(INT_MIN), 0.0, need)

    def dm_body(c, ties_before):
        for sb in range(n_sb):
            r0 = c * KC + sb * SB
            krow = r0 + lax.broadcasted_iota(I32, (SB, TQ), 0)
            dist = jnp.abs(qpos - krow).astype(F32)
            key = key_scr[pl.ds(r0, SB), :]
            is_tie = key == tau
            tie01 = jnp.where(is_tie, 1.0, 0.0)
            rank = ties_before + jnp.dot(tri_ref[...], tie01.astype(BF16), preferred_element_type=F32)
            inner = jnp.where(is_tie, jnp.where(rank <= need, dist, jnp.inf), jnp.inf)
            dm_scr[pl.ds(r0, SB), :] = jnp.where(key > tau, dist, inner)
            ties_before = ties_before + jnp.sum(_fold_rows(tie01, jnp.add), axis=0, keepdims=True)
        return ties_before

    over_chunks(dm_body, jnp.zeros((1, TQ), F32))

    slopes = [s * LOG2E for s in SLOPES_D]

    def attn_body(c, carry):
        ms, ls, acc = list(carry[:4]), list(carry[4:8]), carry[8]
        atts = [jnp.dot(knv_scr[pl.ds(c * KC + sb * SB, SB), :], rhs_main,
                        preferred_element_type=F32) for sb in range(n_sb)]
        for sb in range(n_sb):
            r0 = c * KC + sb * SB
            att = atts[sb]
            dm = dm_scr[pl.ds(r0, SB), :]
            alphas, probs = [], []
            for h in range(4):
                a = att[:, h * TQ:(h + 1) * TQ] - slopes[h] * dm
                m_new = jnp.maximum(ms[h], jnp.max(_fold_rows(a, jnp.maximum), axis=0, keepdims=True))
                m_use = jnp.where(m_new == -jnp.inf, 0.0, m_new)
                alpha = jnp.exp2(ms[h] - m_use)
                p = jnp.exp2(a - m_use)
                ls[h] = ls[h] * alpha + jnp.sum(_fold_rows(p, jnp.add), axis=0, keepdims=True)
                ms[h] = m_new
                alphas.append(alpha)
                probs.append(p.astype(BF16))
            pv = jnp.dot(vt_scr[c, :, sb * SB:(sb + 1) * SB], jnp.concatenate(probs, axis=1),
                         preferred_element_type=F32)
            acc = acc * jnp.concatenate(alphas, axis=1) + pv
        return (*ms, *ls, acc)

    neg = jnp.full((1, TQ), -jnp.inf, F32)
    zero = jnp.zeros((1, TQ), F32)
    res = over_chunks(attn_body, (neg,) * 4 + (zero,) * 4 + (jnp.zeros((DSA_HD, 4 * TQ), F32),))
    ls = res[4:8]
    out_t = res[8]
    o_t = jnp.concatenate([out_t[:, h * TQ:(h + 1) * TQ] * (1.0 / ls[h]) for h in range(4)], axis=0)
    return o_t.T


def _dsa(yd, qg, kg, p64, p64h):
    b, t, _ = yd.shape
    TB = TQ * DSA_SUBTILES
    nq = t // TB
    nkc = t // KC
    W = GROUP_W
    topk = min(DSA_TOPK_MAX, t // 4)
    kern = functools.partial(_dsa_kernel, seq=t, topk=topk)
    small = lambda a: pl.BlockSpec(a.shape, lambda bi, i: (0,) * a.ndim)
    tri = jnp.asarray(np.tril(np.ones((DSA_SB, DSA_SB), np.float32)), dtype=BF16)
    return pl.pallas_call(
        kern,
        grid=(b, nq),
        in_specs=[pl.BlockSpec((None, TB, W), lambda bi, i: (bi, i, 0)),
                  pl.BlockSpec((None, TB, W), lambda bi, i: (bi, i, 1)),
                  pl.BlockSpec((None, TB, 128), lambda bi, i: (bi, i, 5)),
                  pl.BlockSpec((None, TB, 128), lambda bi, i: (bi, i, 6)),
                  pl.BlockSpec((None, t, 128), lambda bi, i: (bi, 0, 4)),
                  pl.BlockSpec((None, t, 128), lambda bi, i: (bi, 0, 6)),
                  small(qg), small(kg), small(p64), small(p64h), small(tri)],
        out_specs=pl.BlockSpec((None, TB, W), lambda bi, i: (bi, i, 0)),
        out_shape=jax.ShapeDtypeStruct((b, t, W), BF16),
        scratch_shapes=[pltpu.VMEM((t, 128), BF16),
                        pltpu.VMEM((nkc, DSA_HD, KC), BF16),
                        pltpu.VMEM((t, TQ), I32),
                        pltpu.VMEM((t, TQ), F32)],
        compiler_params=pltpu.CompilerParams(dimension_semantics=("arbitrary", "arbitrary"),
                                             vmem_limit_bytes=VMEM_LIMIT),
        name="dsa",
    )(yd, yd, yd, yd, yd, yd, qg, kg, p64, p64h, tri)


def kernel(x, norm_g, w_in, conv_w, conv_b, gmlp_g, gmlp_ws, gmlp_b, diff_qg, diff_kg, diff_lam,
           diff_subg, dsa_qg, dsa_kg, w_out):
    b, t, d = x.shape
    depth = w_in.shape[0]
    p32 = _block_diag_mean(GROUP_W, 32)
    p64 = _block_diag_mean(GROUP_W, 64)
    p64h = _block_diag_mean(128, 64)
    w_in16 = _wprep(w_in)
    w_out16 = _wprep(w_out)
    xf = x.reshape(b * t, d)
    ys = _inproj(xf, norm_g[0].reshape(1, d), w_in16, 0)
    for l in range(depth):
        ya, yb, yc, yd = (a.reshape(b, t, a.shape[-1]) for a in ys)
        bfull = jnp.repeat(gmlp_b[l].T, GROUP_W // 4, axis=1)
        mab = _mixab(ya, yb, conv_w[l], conv_b[l].reshape(1, -1), gmlp_g[l].reshape(1, -1),
                     gmlp_ws[l], bfull, p64)
        lam_init = 0.8 - 0.6 * math.exp(-0.3 * l)
        mc = _diff(yc, jnp.tile(diff_qg[l], 8).reshape(1, -1), jnp.tile(diff_kg[l], 8).reshape(1, -1),
                   diff_lam[l], diff_subg[l].reshape(1, -1), p32, p64, lam_init)
        md = _dsa(yd, jnp.tile(dsa_qg[l], 4).reshape(1, -1), jnp.tile(dsa_kg[l], 2).reshape(1, -1),
                  p64, p64h)
        mixes = (mab.reshape(b * t, -1), mc.reshape(b * t, -1), md.reshape(b * t, -1))
        if l + 1 < depth:
            xf, *ys = _outin(xf, *mixes, w_out16, l, norm_g[l + 1].reshape(1, d), w_in16)
        else:
            xf = _outproj(xf, *mixes, w_out16, l)
    return xf.reshape(b, t, d)
```

```python
import functools
import math

import numpy as np
import jax
import jax.numpy as jnp
from jax import lax
from jax.experimental import pallas as pl
from jax.experimental.pallas import tpu as pltpu

F32 = jnp.float32
BF16 = jnp.bfloat16
I32 = jnp.int32

GROUP_W = 256
CHUNK = 64
CHUNK_SHIFT = CHUNK.bit_length() - 1
HEAD_SHIFT = 6
QD_SHIFT = 5
GMLP_BLOCK = 128
DIFF_QD = 32
DSA_HD = 64
IDX_HD = 32
IDX_HEADS = 4
DSA_TOPK_MAX = 256
EPS = 1e-6
LOG2E = math.log2(math.e)
INT_MIN = -2 ** 31
LANES = 128
SUBLANES = 8

TQ = 128
TQ_DIFF = 256
DIFF_SUBTILES = 2
DSA_SUBTILES = 4
DSA_SB = 256
DSA_PAIR = 2
MIX_SUBTILES = 4
KC = 512
KC_DIFF = 512
TM = 512
HALO = 16
NHC = 8

_SLOPES = 2.0 ** (-8.0 * np.arange(1, 9) / 8.0)
SLOPES_C = [float(s) for s in _SLOPES[0::2]]
SLOPES_D = [float(s) for s in _SLOPES[1::2]]

VMEM_LIMIT = 56 * 1024 * 1024


def _block_diag_mean(width, seg):
    idx = np.arange(width) // seg
    return jnp.asarray((idx[:, None] == idx[None, :]).astype(np.float32) / seg, dtype=BF16)


def _seg_mean(x2, p):
    hi = x2.astype(BF16)
    lo = (x2 - hi.astype(F32)).astype(BF16)
    return (jnp.dot(hi, p, preferred_element_type=F32)
            + jnp.dot(lo, p, preferred_element_type=F32))


def _silu(z):
    return z * jax.nn.sigmoid(z)


def _gelu(x):
    return 0.5 * x * (1.0 + lax.erf(x * (2.0 ** -0.5)))


def _fold_rows(x, op, stop=SUBLANES):
    r = x.shape[0]
    while r > stop:
        r //= 2
        x = op(x[:r], x[r:])
    return x


def _fold_lanes(x, op):
    c = x.shape[1]
    while c > LANES:
        c //= 2
        x = op(x[:, :c], x[:, c:])
    return x


def _wprep_kernel(w_ref, o_ref, *, n_valid):
    col = pl.program_id(1) * LANES + lax.broadcasted_iota(I32, (1, LANES), 1)
    o_ref[...] = jnp.where(col < n_valid, w_ref[...], 0.0).astype(BF16)


def _wprep(w):
    depth, d, n = w.shape
    nt = pl.cdiv(n, LANES)
    return pl.pallas_call(
        functools.partial(_wprep_kernel, n_valid=n),
        grid=(depth, nt),
        in_specs=[pl.BlockSpec((None, d, LANES), lambda l, j: (l, 0, j))],
        out_specs=pl.BlockSpec((None, d, LANES), lambda l, j: (l, 0, j)),
        out_shape=jax.ShapeDtypeStruct((depth, d, nt * LANES), BF16),
        compiler_params=pltpu.CompilerParams(dimension_semantics=("arbitrary", "arbitrary")),
        name="wprep",
    )(w)


_A0, _B0, _C0, _D0 = 0, 4 * GROUP_W, 7 * GROUP_W, 11 * GROUP_W
_DQ, _DKV, _DZ, _DIQ, _DEND = _D0, _D0 + 256, _D0 + 384, _D0 + 640, _D0 + 896
YD_W = 896


_Y_WIDTHS = (_B0 - _A0, _C0 - _B0, _D0 - _C0, YD_W)


def _norm_project(x, g_ref, w_ref, ya_ref, yb_ref, yc_ref, yd_ref):
    ms = jnp.mean(x * x, axis=-1, keepdims=True)
    xn = (x * lax.rsqrt(ms + EPS) * g_ref[...]).astype(BF16)

    def proj(lo, hi):
        return jnp.dot(xn, w_ref[:, lo:hi], preferred_element_type=F32).astype(BF16)

    ya_ref[...] = proj(_A0, _B0)
    yb_ref[...] = proj(_B0, _C0)
    yc_ref[...] = proj(_C0, _D0)
    yd_ref[:, 0:256] = proj(_DQ, _DKV)
    yd_ref[:, 256:512] = proj(_DZ, _DIQ)
    yd_ref[:, 512:640] = proj(_DKV, _DZ)
    yd_ref[:, 640:896] = proj(_DIQ, _DEND)


def _mix_project(x_ref, mab_ref, mc_ref, md_ref, wo_ref):
    acc = x_ref[...]
    acc = acc + jnp.dot(mab_ref[...], wo_ref[0:2 * GROUP_W, :], preferred_element_type=F32)
    acc = acc + jnp.dot(mc_ref[...], wo_ref[2 * GROUP_W:3 * GROUP_W, :], preferred_element_type=F32)
    return acc + jnp.dot(md_ref[...], wo_ref[3 * GROUP_W:4 * GROUP_W, :], preferred_element_type=F32)


def _inproj_kernel(x_ref, g_ref, w_ref, ya_ref, yb_ref, yc_ref, yd_ref):
    _norm_project(x_ref[...], g_ref, w_ref, ya_ref, yb_ref, yc_ref, yd_ref)


def _outproj_kernel(x_ref, mab_ref, mc_ref, md_ref, wo_ref, o_ref):
    o_ref[...] = _mix_project(x_ref, mab_ref, mc_ref, md_ref, wo_ref)


def _outin_kernel(x_ref, mab_ref, mc_ref, md_ref, wo_ref, g_ref, w_ref,
                  o_ref, ya_ref, yb_ref, yc_ref, yd_ref):
    x_new = _mix_project(x_ref, mab_ref, mc_ref, md_ref, wo_ref)
    o_ref[...] = x_new
    _norm_project(x_new, g_ref, w_ref, ya_ref, yb_ref, yc_ref, yd_ref)


def _row_spec(width):
    return pl.BlockSpec((TM, width), lambda i: (i, 0))


def _layer_spec(w, layer):
    return pl.BlockSpec((None,) + w.shape[1:], lambda i: (layer, 0, 0))


_PROJ_PARAMS = pltpu.CompilerParams(dimension_semantics=("arbitrary",), vmem_limit_bytes=VMEM_LIMIT)


def _inproj(xf, g, wb16, layer):
    m, d = xf.shape
    return pl.pallas_call(
        _inproj_kernel,
        grid=(m // TM,),
        in_specs=[_row_spec(d), pl.BlockSpec((1, d), lambda i: (0, 0)), _layer_spec(wb16, layer)],
        out_specs=[_row_spec(w) for w in _Y_WIDTHS],
        out_shape=[jax.ShapeDtypeStruct((m, w), BF16) for w in _Y_WIDTHS],
        compiler_params=_PROJ_PARAMS,
        name="inproj",
    )(xf, g, wb16)


def _outproj(xf, mab, mc, md, wo, layer):
    m, d = xf.shape
    return pl.pallas_call(
        _outproj_kernel,
        grid=(m // TM,),
        in_specs=[_row_spec(d), _row_spec(2 * GROUP_W), _row_spec(GROUP_W), _row_spec(GROUP_W),
                  _layer_spec(wo, layer)],
        out_specs=_row_spec(d),
        out_shape=jax.ShapeDtypeStruct((m, d), F32),
        compiler_params=_PROJ_PARAMS,
        name="outproj",
    )(xf, mab, mc, md, wo)


def _outin(xf, mab, mc, md, wo, layer, g_next, wb16):
    m, d = xf.shape
    return pl.pallas_call(
        _outin_kernel,
        grid=(m // TM,),
        in_specs=[_row_spec(d), _row_spec(2 * GROUP_W), _row_spec(GROUP_W), _row_spec(GROUP_W),
                  _layer_spec(wo, layer), pl.BlockSpec((1, d), lambda i: (0, 0)),
                  _layer_spec(wb16, layer + 1)],
        out_specs=[_row_spec(d)] + [_row_spec(w) for w in _Y_WIDTHS],
        out_shape=[jax.ShapeDtypeStruct((m, d), F32)]
                  + [jax.ShapeDtypeStruct((m, w), BF16) for w in _Y_WIDTHS],
        compiler_params=_PROJ_PARAMS,
        name="outin",
    )(xf, mab, mc, md, wo, g_next, wb16)


def _mixab_kernel(ya_ref, halo_ref, yb_ref, cw_ref, cb_ref, gg_ref, ws_ref, bfull_ref, p64_ref,
                  o_ref, u_scr):
    i = pl.program_id(1)
    W = GROUP_W
    TB = TQ * MIX_SUBTILES
    uh = halo_ref[:, 2 * W:3 * W].astype(F32) * halo_ref[:, 0:W].astype(F32)
    u_scr[0:HALO, :] = jnp.where(i > 0, uh, 0.0)
    u_scr[HALO:HALO + TB, :] = ya_ref[:, 2 * W:3 * W].astype(F32) * ya_ref[:, 0:W].astype(F32)
    for sb in range(MIX_SUBTILES):
        r0 = HALO + sb * TQ
        y = u_scr[r0 - 2:r0 - 2 + TQ, :] * cw_ref[0:1, :]
        y = y + u_scr[r0 - 1:r0 - 1 + TQ, :] * cw_ref[1:2, :]
        y = y + u_scr[r0:r0 + TQ, :] * cw_ref[2:3, :]
        rows = slice(sb * TQ, (sb + 1) * TQ)
        bg = ya_ref[rows, W:2 * W].astype(F32)
        za = ya_ref[rows, 3 * W:4 * W].astype(F32)
        o_ref[rows, 0:W] = (bg * (y + cb_ref[...]) * _silu(za)).astype(o_ref.dtype)

    t_idx = lax.broadcasted_iota(I32, (GMLP_BLOCK, GMLP_BLOCK), 0)
    s_idx = lax.broadcasted_iota(I32, (GMLP_BLOCK, GMLP_BLOCK), 1)
    causal = (t_idx >> CHUNK_SHIFT) >= (s_idx >> CHUNK_SHIFT)
    lane = lax.broadcasted_iota(I32, (1, W), 1)
    wms = [jnp.where(causal, ws_ref[hd], 0.0).astype(BF16) for hd in range(4)]

    def block(sb, carry):
        rows = pl.ds(pl.multiple_of(sb * TQ, TQ), TQ)
        u = _gelu(yb_ref[rows, 0:W].astype(F32))
        v = _gelu(yb_ref[rows, W:2 * W].astype(F32))
        zb = yb_ref[rows, 2 * W:3 * W].astype(F32)
        vn = v * lax.rsqrt(_seg_mean(v * v, p64_ref[...]) + EPS) * gg_ref[...]
        s = bfull_ref[...]
        for hd in range(4):
            vh = jnp.where((lane >> HEAD_SHIFT) == hd, vn, 0.0).astype(BF16)
            s = s + jnp.dot(wms[hd], vh, preferred_element_type=F32)
        o_ref[rows, W:2 * W] = (u * s * _silu(zb)).astype(o_ref.dtype)
        return carry

    lax.fori_loop(0, MIX_SUBTILES, block, 0)


def _mixab(ya, yb, cw, cb, gg, ws, bfull, p64):
    b, t, _ = ya.shape
    TB = TQ * MIX_SUBTILES
    nq = t // TB
    return pl.pallas_call(
        _mixab_kernel,
        grid=(b, nq),
        in_specs=[pl.BlockSpec((None, TB, 4 * GROUP_W), lambda bi, i: (bi, i, 0)),
                  pl.BlockSpec((None, HALO, 4 * GROUP_W),
                               lambda bi, i: (bi, jnp.maximum(i * (TB // HALO) - 1, 0), 0)),
                  pl.BlockSpec((None, TB, 3 * GROUP_W), lambda bi, i: (bi, i, 0)),
                  pl.BlockSpec(cw.shape, lambda bi, i: (0, 0)),
                  pl.BlockSpec(cb.shape, lambda bi, i: (0, 0)),
                  pl.BlockSpec(gg.shape, lambda bi, i: (0, 0)),
                  pl.BlockSpec(ws.shape, lambda bi, i: (0, 0, 0)),
                  pl.BlockSpec(bfull.shape, lambda bi, i: (0, 0)),
                  pl.BlockSpec(p64.shape, lambda bi, i: (0, 0))],
        out_specs=pl.BlockSpec((None, TB, 2 * GROUP_W), lambda bi, i: (bi, i, 0)),
        out_shape=jax.ShapeDtypeStruct((b, t, 2 * GROUP_W), BF16),
        scratch_shapes=[pltpu.VMEM((HALO + TB, GROUP_W), F32)],
        compiler_params=pltpu.CompilerParams(dimension_semantics=("arbitrary", "arbitrary"),
                                             vmem_limit_bytes=VMEM_LIMIT),
        name="mixab",
    )(ya, ya, yb, cw, cb, gg, ws, bfull, p64)


def _diff_kernel(q_ref, k_ref, v_ref, z_ref, qg_ref, kg_ref, lam_ref, subg_ref, p32_ref, p64_ref,
                 o_ref, kn_scr, qs_scr, s_scr, m_scr, l_scr, p_scr, acc_scr, *, lam_init, seq):
    TQ = TQ_DIFF
    i = pl.program_id(1)

    @pl.when(i == 0)
    def _():
        for c in range(seq // KC):
            kk = k_ref[c * KC:(c + 1) * KC, :].astype(F32)
            ms = _seg_mean(kk * kk, p32_ref[...])
            kn_scr[c * KC:(c + 1) * KC, :] = (kk * lax.rsqrt(ms + EPS) * kg_ref[...]).astype(BF16)

    lp = lam_ref[...]
    lam = (jnp.exp(jnp.sum(lp[0:1] * lp[1:2], axis=-1, keepdims=True))
           - jnp.exp(jnp.sum(lp[2:3] * lp[3:4], axis=-1, keepdims=True)) + lam_init)

    def sub_tile(sub, carry):
        rows = pl.ds(pl.multiple_of(sub * TQ, TQ), TQ)
        q0 = (i * DIFF_SUBTILES + sub) * TQ
        o = _diff_tile(q0, q_ref[rows, :].astype(F32), lam, v_ref, qg_ref, subg_ref, p32_ref, p64_ref,
                       kn_scr, qs_scr, s_scr, m_scr, l_scr, p_scr, acc_scr, lam_init)
        z = z_ref[rows, :].astype(F32)
        o_ref[rows, :] = (o * _silu(z)).astype(o_ref.dtype)
        return carry

    lax.fori_loop(0, DIFF_SUBTILES, sub_tile, 0)


def _diff_tile(q0, q, lam, v_ref, qg_ref, subg_ref, p32_ref, p64_ref,
               kn_scr, qs_scr, s_scr, m_scr, l_scr, p_scr, acc_scr, lam_init):
    TQ = TQ_DIFF
    KC = KC_DIFF
    nchunk = (q0 + TQ - 1) // KC + 1
    p32 = p32_ref[...]
    qn = q * lax.rsqrt(_seg_mean(q * q, p32) + EPS) * (qg_ref[...] * (DIFF_QD ** -0.5 * LOG2E))
    lane = lax.broadcasted_iota(I32, (1, GROUP_W), 1)
    for hc in range(NHC):
        qs_scr[hc * TQ:(hc + 1) * TQ, :] = jnp.where((lane >> QD_SHIFT) == hc, qn, 0.0).astype(BF16)

    m_scr[...] = jnp.full(m_scr.shape, -jnp.inf, F32)
    l_scr[...] = jnp.zeros(l_scr.shape, F32)
    acc_scr[...] = jnp.zeros(acc_scr.shape, F32)
    row = q0 + lax.broadcasted_iota(I32, (TQ, KC), 0)

    def s_body(c, carry):
        col = c * KC + lax.broadcasted_iota(I32, (TQ, KC), 1)
        dist = jnp.abs(row - col).astype(F32)
        dm = jnp.where((col >> CHUNK_SHIFT) <= (row >> CHUNK_SHIFT), dist, jnp.inf)
        kc = kn_scr[pl.ds(pl.multiple_of(c * KC, KC), KC), :]
        s_all = lax.dot_general(qs_scr[...], kc, (((1,), (1,)), ((), ())), preferred_element_type=F32)
        for h in range(NHC // 2):
            bias = (SLOPES_C[h] * LOG2E) * dm
            for j in range(2):
                rows = slice((2 * h + j) * TQ, (2 * h + j + 1) * TQ)
                s = s_all[rows, :] - bias
                s_scr[c, rows, :] = s
                m_scr[rows, :] = jnp.maximum(m_scr[rows, :], _fold_lanes(s, jnp.maximum))
        return carry

    lax.fori_loop(0, nchunk, s_body, 0)

    for hc in range(NHC):
        m = jnp.max(m_scr[hc * TQ:(hc + 1) * TQ, :], axis=1, keepdims=True)
        m_scr[hc * TQ:(hc + 1) * TQ, :] = jnp.broadcast_to(m, (TQ, LANES))

    def e_body(c, carry):
        for hc in range(NHC):
            rows = slice(hc * TQ, (hc + 1) * TQ)
            m = m_scr[rows, :]
            s = s_scr[c, rows, :]
            ps = [jnp.exp2(s[:, k * LANES:(k + 1) * LANES] - m) for k in range(KC // LANES)]
            l_scr[rows, :] += _fold_lanes(jnp.concatenate(ps, axis=1), jnp.add)
            p_scr[rows, :] = jnp.concatenate(ps, axis=1).astype(BF16)
        vc = v_ref[pl.ds(pl.multiple_of(c * KC, KC), KC), :]
        acc_scr[...] += jnp.dot(p_scr[...], vc, preferred_element_type=F32)
        return carry

    lax.fori_loop(0, nchunk, e_body, 0)

    o = jnp.zeros((TQ, GROUP_W), F32)
    for h in range(NHC // 2):
        r1 = slice(2 * h * TQ, (2 * h + 1) * TQ)
        r2 = slice((2 * h + 1) * TQ, (2 * h + 2) * TQ)
        l1 = jnp.sum(l_scr[r1, :], axis=1, keepdims=True)
        l2 = jnp.sum(l_scr[r2, :], axis=1, keepdims=True)
        o_h = acc_scr[r1, :] * (1.0 / l1) - acc_scr[r2, :] * (lam / l2)
        o = jnp.where((lane >> HEAD_SHIFT) == h, o_h, o)

    ms = _seg_mean(o * o, p64_ref[...])
    return o * lax.rsqrt(ms + EPS) * (subg_ref[...] * (1.0 - lam_init))


def _diff(yc, qg, kg, lam_p, subg, p32, p64, lam_init):
    TQ = TQ_DIFF
    KC = KC_DIFF
    TB = TQ * DIFF_SUBTILES
    b, t, _ = yc.shape
    nq = t // TB
    nkc = t // KC
    W = GROUP_W
    kern = functools.partial(_diff_kernel, lam_init=lam_init, seq=t)
    small = lambda a: pl.BlockSpec(a.shape, lambda bi, i: (0,) * a.ndim)
    return pl.pallas_call(
        kern,
        grid=(b, nq),
        in_specs=[pl.BlockSpec((None, TB, W), lambda bi, i: (bi, i, 0)),
                  pl.BlockSpec((None, t, W), lambda bi, i: (bi, 0, 1)),
                  pl.BlockSpec((None, t, W), lambda bi, i: (bi, 0, 2)),
                  pl.BlockSpec((None, TB, W), lambda bi, i: (bi, i, 3)),
                  small(qg), small(kg), small(lam_p), small(subg), small(p32), small(p64)],
        out_specs=pl.BlockSpec((None, TB, W), lambda bi, i: (bi, i, 0)),
        out_shape=jax.ShapeDtypeStruct((b, t, W), BF16),
        scratch_shapes=[pltpu.VMEM((t, W), BF16),
                        pltpu.VMEM((NHC * TQ, W), BF16),
                        pltpu.VMEM((nkc, NHC * TQ, KC), F32),
                        pltpu.VMEM((NHC * TQ, LANES), F32),
                        pltpu.VMEM((NHC * TQ, LANES), F32),
                        pltpu.VMEM((NHC * TQ, KC), BF16),
                        pltpu.VMEM((NHC * TQ, W), F32)],
        compiler_params=pltpu.CompilerParams(dimension_semantics=("arbitrary", "arbitrary"),
                                             vmem_limit_bytes=VMEM_LIMIT),
        name="diffattn",
    )(yc, yc, yc, yc, qg, kg, lam_p, subg, p32, p64)


def _dsa_kernel(q_ref, z_ref, iq_ref, ikwq_ref, kv_ref, ikw_ref, qg_ref, kg_ref, p64_ref, p64h_ref,
                tri_ref, o_ref, knv_scr, vt_scr, key_scr, dm_scr, *, seq, topk):
    i = pl.program_id(1)

    @pl.when(i == 0)
    def _():
        p64h = p64h_ref[...]
        for c in range(seq // 128):
            blk = kv_ref[c * 128:(c + 1) * 128, :].astype(F32)
            ms = _seg_mean(blk * blk, p64h)
            knv_scr[c * 128:(c + 1) * 128, :] = (blk * lax.rsqrt(ms + EPS) * kg_ref[...]).astype(BF16)
            vt = blk.T
            cc, off = divmod(c * 128, KC)
            vt_scr[cc, :, off:off + 128] = vt[DSA_HD:2 * DSA_HD, :].astype(BF16)

    def tile_pair(nchunk, pair, carry):
        tiles = []
        for s in range(DSA_PAIR):
            sub = pair * DSA_PAIR + s
            rows = pl.ds(pl.multiple_of(sub * TQ, TQ), TQ)
            q0 = (i * DSA_SUBTILES + sub) * TQ

            iq_t = iq_ref[rows, :].astype(F32).T
            iw_t = ikwq_ref[rows, :].astype(F32).T[IDX_HD:IDX_HD + 8, :]
            wq = iw_t * (IDX_HEADS ** -0.5 * IDX_HD ** -0.5)
            zpad_i = jnp.zeros((128 - IDX_HD, TQ), F32)
            rhs_idx = jnp.concatenate(
                [jnp.concatenate([iq_t[IDX_HD * h:IDX_HD * (h + 1), :], zpad_i], axis=0)
                 for h in range(IDX_HEADS)], axis=1).astype(BF16)
            _dsa_index(nchunk, q0, rhs_idx, wq, ikw_ref, key_scr.at[s])
            tiles.append((rows, q0))

        taus = _dsa_search(nchunk, [key_scr.at[s] for s in range(DSA_PAIR)], tiles[-1][1], topk)

        for s, (rows, q0) in enumerate(tiles):
            q = q_ref[rows, :].astype(F32)
            qn = q * lax.rsqrt(_seg_mean(q * q, p64_ref[...]) + EPS) * (qg_ref[...] * (DSA_HD ** -0.5 * LOG2E))
            qn_t = qn.T
            zpad_q = jnp.zeros((128 - DSA_HD, TQ), F32)
            rhs_main = jnp.concatenate(
                [jnp.concatenate([qn_t[DSA_HD * h:DSA_HD * (h + 1), :], zpad_q], axis=0) for h in range(4)],
                axis=1).astype(BF16)
            o = _dsa_attend(nchunk, q0, taus[s], rhs_main, tri_ref, knv_scr, vt_scr, key_scr.at[s],
                            dm_scr.at[s], topk)
            z = z_ref[rows, :].astype(F32)
            o_ref[rows, :] = (o * _silu(z)).astype(o_ref.dtype)
        return carry

    assert TQ * DSA_SUBTILES == KC and DSA_SUBTILES % DSA_PAIR == 0
    for n in range(1, seq // KC + 1):
        @pl.when(i == n - 1)
        def _(n=n):
            lax.fori_loop(0, DSA_SUBTILES // DSA_PAIR, functools.partial(tile_pair, n), 0)


def _dsa_index(nchunk, q0, rhs_idx, wq, ikw_ref, key_scr):
    SB = DSA_SB
    n_sb = KC // SB
    qpos = q0 + lax.broadcasted_iota(I32, (SB, TQ), 1)
    for c in range(nchunk):
        for sb in range(n_sb):
            r0 = c * KC + sb * SB
            logit = jnp.dot(ikw_ref[pl.ds(r0, SB), :], rhs_idx, preferred_element_type=F32)
            sc = ((jnp.maximum(logit[:, 0:TQ], 0.0) * wq[0:1, :]
                   + jnp.maximum(logit[:, TQ:2 * TQ], 0.0) * wq[1:2, :])
                  + (jnp.maximum(logit[:, 2 * TQ:3 * TQ], 0.0) * wq[2:3, :]
                     + jnp.maximum(logit[:, 3 * TQ:4 * TQ], 0.0) * wq[3:4, :]))
            bits = lax.bitcast_convert_type(sc, I32)
            key = bits ^ ((bits >> 31) & jnp.int32(0x7FFFFFFF))
            key = jnp.where(key == -1, 0, key)
            if c == nchunk - 1:
                krow = r0 + lax.broadcasted_iota(I32, (SB, TQ), 0)
                key = jnp.where((krow >> CHUNK_SHIFT) <= (qpos >> CHUNK_SHIFT), key, jnp.int32(INT_MIN))
            key_scr[pl.ds(r0, SB), :] = key


def _dsa_search(nchunk, key_scrs, last_q0, topk):
    kf = jnp.float32(topk)

    def count_ge(key_scr, cand):
        acc = jnp.zeros((SUBLANES, TQ), F32)
        for c in range(nchunk):
            acc = acc + _fold_rows(jnp.where(key_scr[c * KC:(c + 1) * KC, :] >= cand, 1.0, 0.0), jnp.add)
        return jnp.sum(acc, axis=0, keepdims=True)

    def tau_step(it, us):
        out = []
        for key_scr, u in zip(key_scrs, us):
            cand_u = u | jnp.left_shift(jnp.int32(1), 31 - it)
            cand = cand_u ^ jnp.int32(INT_MIN)
            out.append(jnp.where(count_ge(key_scr, cand) >= kf, cand_u, u))
        return tuple(out)

    def search():
        zero = jnp.zeros((1, TQ), I32)
        us = lax.fori_loop(0, 32, tau_step, (zero,) * len(key_scrs))
        return tuple(u ^ jnp.int32(INT_MIN) for u in us)

    if nchunk == 1:
        lowest = jnp.full((1, TQ), INT_MIN, I32)
        return lax.cond(last_q0 + TQ <= topk, lambda: (lowest,) * len(key_scrs), search)
    return search()


def _dsa_attend(nchunk, q0, tau, rhs_main, tri_ref, knv_scr, vt_scr, key_scr, dm_scr, topk):
    SB = DSA_SB
    n_sb = KC // SB
    kf = jnp.float32(topk)
    qpos = q0 + lax.broadcasted_iota(I32, (SB, TQ), 1)

    def over_chunks(body, init):
        acc = init
        for c in range(nchunk):
            acc = body(c, acc)
        return acc

    def gt_body(c, acc):
        key = key_scr[c * KC:(c + 1) * KC, :]
        return acc + _fold_rows(jnp.where(key > tau, 1.0, 0.0), jnp.add)

    n_gt = over_chunks(gt_body, jnp.zeros((SUBLANES, TQ), F32))
    need = kf - jnp.sum(n_gt, axis=0, keepdims=True)
    need = jnp.where(tau == jnp.int32(INT_MIN), 0.0, need)

    def dm_body(c, ties_before):
        for sb in range(n_sb):
            r0 = c * KC + sb * SB
            krow = r0 + lax.broadcasted_iota(I32, (SB, TQ), 0)
            dist = jnp.abs(qpos - krow).astype(F32)
            key = key_scr[pl.ds(r0, SB), :]
            is_tie = key == tau
            tie01 = jnp.where(is_tie, 1.0, 0.0)
            rank = ties_before + jnp.dot(tri_ref[...], tie01.astype(BF16), preferred_element_type=F32)
            inner = jnp.where(is_tie, jnp.where(rank <= need, dist, jnp.inf), jnp.inf)
            dm_scr[pl.ds(r0, SB), :] = jnp.where(key > tau, dist, inner)
            ties_before = ties_before + jnp.sum(_fold_rows(tie01, jnp.add), axis=0, keepdims=True)
        return ties_before

    over_chunks(dm_body, jnp.zeros((1, TQ), F32))

    slopes = [s * LOG2E for s in SLOPES_D]

    def attn_body(c, carry):
        ms, ls, acc = list(carry[:4]), list(carry[4:8]), carry[8]
        atts = [jnp.dot(knv_scr[pl.ds(c * KC + sb * SB, SB), :], rhs_main,
                        preferred_element_type=F32) for sb in range(n_sb)]
        for sb in range(n_sb):
            r0 = c * KC + sb * SB
            att = atts[sb]
            dm = dm_scr[pl.ds(r0, SB), :]
            alphas, probs = [], []
            for h in range(4):
                a = att[:, h * TQ:(h + 1) * TQ] - slopes[h] * dm
                m_new = jnp.maximum(ms[h], jnp.max(_fold_rows(a, jnp.maximum), axis=0, keepdims=True))
                m_use = jnp.where(m_new == -jnp.inf, 0.0, m_new)
                alpha = jnp.exp2(ms[h] - m_use)
                p = jnp.exp2(a - m_use)
                ls[h] = ls[h] * alpha + jnp.sum(_fold_rows(p, jnp.add), axis=0, keepdims=True)
                ms[h] = m_new
                alphas.append(alpha)
                probs.append(p.astype(BF16))
            pv = jnp.dot(vt_scr[c, :, sb * SB:(sb + 1) * SB], jnp.concatenate(probs, axis=1),
                         preferred_element_type=F32)
            acc = acc * jnp.concatenate(alphas, axis=1) + pv
        return (*ms, *ls, acc)

    neg = jnp.full((1, TQ), -jnp.inf, F32)
    zero = jnp.zeros((1, TQ), F32)
    res = over_chunks(attn_body, (neg,) * 4 + (zero,) * 4 + (jnp.zeros((DSA_HD, 4 * TQ), F32),))
    ls = res[4:8]
    out_t = res[8]
    o_t = jnp.concatenate([out_t[:, h * TQ:(h + 1) * TQ] * (1.0 / ls[h]) for h in range(4)], axis=0)
    return o_t.T


def _dsa(yd, qg, kg, p64, p64h):
    b, t, _ = yd.shape
    TB = TQ * DSA_SUBTILES
    nq = t // TB
    nkc = t // KC
    W = GROUP_W
    topk = min(DSA_TOPK_MAX, t // 4)
    kern = functools.partial(_dsa_kernel, seq=t, topk=topk)
    small = lambda a: pl.BlockSpec(a.shape, lambda bi, i: (0,) * a.ndim)
    tri = jnp.asarray(np.tril(np.ones((DSA_SB, DSA_SB), np.float32)), dtype=BF16)
    return pl.pallas_call(
        kern,
        grid=(b, nq),
        in_specs=[pl.BlockSpec((None, TB, W), lambda bi, i: (bi, i, 0)),
                  pl.BlockSpec((None, TB, W), lambda bi, i: (bi, i, 1)),
                  pl.BlockSpec((None, TB, 128), lambda bi, i: (bi, i, 5)),
                  pl.BlockSpec((None, TB, 128), lambda bi, i: (bi, i, 6)),
                  pl.BlockSpec((None, t, 128), lambda bi, i: (bi, 0, 4)),
                  pl.BlockSpec((None, t, 128), lambda bi, i: (bi, 0, 6)),
                  small(qg), small(kg), small(p64), small(p64h), small(tri)],
        out_specs=pl.BlockSpec((None, TB, W), lambda bi, i: (bi, i, 0)),
        out_shape=jax.ShapeDtypeStruct((b, t, W), BF16),
        scratch_shapes=[pltpu.VMEM((t, 128), BF16),
                        pltpu.VMEM((nkc, DSA_HD, KC), BF16),
                        pltpu.VMEM((DSA_PAIR, t, TQ), I32),
                        pltpu.VMEM((DSA_PAIR, t, TQ), F32)],
        compiler_params=pltpu.CompilerParams(dimension_semantics=("arbitrary", "arbitrary"),
                                             vmem_limit_bytes=VMEM_LIMIT),
        name="dsa",
    )(yd, yd, yd, yd, yd, yd, qg, kg, p64, p64h, tri)


def kernel(x, norm_g, w_in, conv_w, conv_b, gmlp_g, gmlp_ws, gmlp_b, diff_qg, diff_kg, diff_lam,
           diff_subg, dsa_qg, dsa_kg, w_out):
    b, t, d = x.shape
    depth = w_in.shape[0]
    p32 = _block_diag_mean(GROUP_W, 32)
    p64 = _block_diag_mean(GROUP_W, 64)
    p64h = _block_diag_mean(128, 64)
    w_in16 = _wprep(w_in)
    w_out16 = _wprep(w_out)
    xf = x.reshape(b * t, d)
    ys = _inproj(xf, norm_g[0].reshape(1, d), w_in16, 0)
    for l in range(depth):
        ya, yb, yc, yd = (a.reshape(b, t, a.shape[-1]) for a in ys)
        bfull = jnp.repeat(gmlp_b[l].T, GROUP_W // 4, axis=1)
        mab = _mixab(ya, yb, conv_w[l], conv_b[l].reshape(1, -1), gmlp_g[l].reshape(1, -1),
                     gmlp_ws[l], bfull, p64)
        lam_init = 0.8 - 0.6 * math.exp(-0.3 * l)
        mc = _diff(yc, jnp.tile(diff_qg[l], 8).reshape(1, -1), jnp.tile(diff_kg[l], 8).reshape(1, -1),
                   diff_lam[l], diff_subg[l].reshape(1, -1), p32, p64, lam_init)
        md = _dsa(yd, jnp.tile(dsa_qg[l], 4).reshape(1, -1), jnp.tile(dsa_kg[l], 2).reshape(1, -1),
                  p64, p64h)
        mixes = (mab.reshape(b * t, -1), mc.reshape(b * t, -1), md.reshape(b * t, -1))
        if l + 1 < depth:
            xf, *ys = _outin(xf, *mixes, w_out16, l, norm_g[l + 1].reshape(1, d), w_in16)
        else:
            xf = _outproj(xf, *mixes, w_out16, l)
    return xf.reshape(b, t, d)
```

```python
import functools
import math

import numpy as np
import jax
import jax.numpy as jnp
from jax import lax
from jax.experimental import pallas as pl
from jax.experimental.pallas import tpu as pltpu

F32 = jnp.float32
BF16 = jnp.bfloat16
I32 = jnp.int32

GROUP_W = 256
CHUNK = 64
CHUNK_SHIFT = CHUNK.bit_length() - 1
HEAD_SHIFT = 6
QD_SHIFT = 5
GMLP_BLOCK = 128
DIFF_QD = 32
DSA_HD = 64
IDX_HD = 32
IDX_HEADS = 4
DSA_TOPK_MAX = 256
EPS = 1e-6
LOG2E = math.log2(math.e)
INT_MIN = -2 ** 31
LANES = 128
SUBLANES = 8

TQ = 128
TQ_DIFF = 256
DIFF_SUBTILES = 2
DSA_SUBTILES = 4
DSA_SB = 256
DSA_PAIR = 4
MIX_SUBTILES = 4
KC = 512
KC_DIFF = 512
TM = 512
HALO = 16
NHC = 8

_SLOPES = 2.0 ** (-8.0 * np.arange(1, 9) / 8.0)
SLOPES_C = [float(s) for s in _SLOPES[0::2]]
SLOPES_D = [float(s) for s in _SLOPES[1::2]]

VMEM_LIMIT = 56 * 1024 * 1024


def _block_diag_mean(width, seg):
    idx = np.arange(width) // seg
    return jnp.asarray((idx[:, None] == idx[None, :]).astype(np.float32) / seg, dtype=BF16)


def _seg_mean(x2, p):
    hi = x2.astype(BF16)
    lo = (x2 - hi.astype(F32)).astype(BF16)
    return (jnp.dot(hi, p, preferred_element_type=F32)
            + jnp.dot(lo, p, preferred_element_type=F32))


def _silu(z):
    return z * jax.nn.sigmoid(z)


def _gelu(x):
    return 0.5 * x * (1.0 + lax.erf(x * (2.0 ** -0.5)))


def _fold_rows(x, op, stop=SUBLANES):
    r = x.shape[0]
    while r > stop:
        r //= 2
        x = op(x[:r], x[r:])
    return x


def _fold_lanes(x, op):
    c = x.shape[1]
    while c > LANES:
        c //= 2
        x = op(x[:, :c], x[:, c:])
    return x


def _wprep_kernel(w_ref, o_ref, *, n_valid):
    col = pl.program_id(1) * LANES + lax.broadcasted_iota(I32, (1, LANES), 1)
    o_ref[...] = jnp.where(col < n_valid, w_ref[...], 0.0).astype(BF16)


def _wprep(w):
    depth, d, n = w.shape
    nt = pl.cdiv(n, LANES)
    return pl.pallas_call(
        functools.partial(_wprep_kernel, n_valid=n),
        grid=(depth, nt),
        in_specs=[pl.BlockSpec((None, d, LANES), lambda l, j: (l, 0, j))],
        out_specs=pl.BlockSpec((None, d, LANES), lambda l, j: (l, 0, j)),
        out_shape=jax.ShapeDtypeStruct((depth, d, nt * LANES), BF16),
        compiler_params=pltpu.CompilerParams(dimension_semantics=("arbitrary", "arbitrary")),
        name="wprep",
    )(w)


_A0, _B0, _C0, _D0 = 0, 4 * GROUP_W, 7 * GROUP_W, 11 * GROUP_W
_DQ, _DKV, _DZ, _DIQ, _DEND = _D0, _D0 + 256, _D0 + 384, _D0 + 640, _D0 + 896
YD_W = 896


_Y_WIDTHS = (_B0 - _A0, _C0 - _B0, _D0 - _C0, YD_W)


def _norm_project(x, g_ref, w_ref, ya_ref, yb_ref, yc_ref, yd_ref):
    ms = jnp.mean(x * x, axis=-1, keepdims=True)
    xn = (x * lax.rsqrt(ms + EPS) * g_ref[...]).astype(BF16)

    def proj(lo, hi):
        return jnp.dot(xn, w_ref[:, lo:hi], preferred_element_type=F32).astype(BF16)

    ya_ref[...] = proj(_A0, _B0)
    yb_ref[...] = proj(_B0, _C0)
    yc_ref[...] = proj(_C0, _D0)
    yd_ref[:, 0:256] = proj(_DQ, _DKV)
    yd_ref[:, 256:512] = proj(_DZ, _DIQ)
    yd_ref[:, 512:640] = proj(_DKV, _DZ)
    yd_ref[:, 640:896] = proj(_DIQ, _DEND)


def _mix_project(x_ref, mab_ref, mc_ref, md_ref, wo_ref):
    acc = x_ref[...]
    acc = acc + jnp.dot(mab_ref[...], wo_ref[0:2 * GROUP_W, :], preferred_element_type=F32)
    acc = acc + jnp.dot(mc_ref[...], wo_ref[2 * GROUP_W:3 * GROUP_W, :], preferred_element_type=F32)
    return acc + jnp.dot(md_ref[...], wo_ref[3 * GROUP_W:4 * GROUP_W, :], preferred_element_type=F32)


def _inproj_kernel(x_ref, g_ref, w_ref, ya_ref, yb_ref, yc_ref, yd_ref):
    _norm_project(x_ref[...], g_ref, w_ref, ya_ref, yb_ref, yc_ref, yd_ref)


def _outproj_kernel(x_ref, mab_ref, mc_ref, md_ref, wo_ref, o_ref):
    o_ref[...] = _mix_project(x_ref, mab_ref, mc_ref, md_ref, wo_ref)


def _outin_kernel(x_ref, mab_ref, mc_ref, md_ref, wo_ref, g_ref, w_ref,
                  o_ref, ya_ref, yb_ref, yc_ref, yd_ref):
    x_new = _mix_project(x_ref, mab_ref, mc_ref, md_ref, wo_ref)
    o_ref[...] = x_new
    _norm_project(x_new, g_ref, w_ref, ya_ref, yb_ref, yc_ref, yd_ref)


def _row_spec(width):
    return pl.BlockSpec((TM, width), lambda i: (i, 0))


def _layer_spec(w, layer):
    return pl.BlockSpec((None,) + w.shape[1:], lambda i: (layer, 0, 0))


_PROJ_PARAMS = pltpu.CompilerParams(dimension_semantics=("arbitrary",), vmem_limit_bytes=VMEM_LIMIT)


def _inproj(xf, g, wb16, layer):
    m, d = xf.shape
    return pl.pallas_call(
        _inproj_kernel,
        grid=(m // TM,),
        in_specs=[_row_spec(d), pl.BlockSpec((1, d), lambda i: (0, 0)), _layer_spec(wb16, layer)],
        out_specs=[_row_spec(w) for w in _Y_WIDTHS],
        out_shape=[jax.ShapeDtypeStruct((m, w), BF16) for w in _Y_WIDTHS],
        compiler_params=_PROJ_PARAMS,
        name="inproj",
    )(xf, g, wb16)


def _outproj(xf, mab, mc, md, wo, layer):
    m, d = xf.shape
    return pl.pallas_call(
        _outproj_kernel,
        grid=(m // TM,),
        in_specs=[_row_spec(d), _row_spec(2 * GROUP_W), _row_spec(GROUP_W), _row_spec(GROUP_W),
                  _layer_spec(wo, layer)],
        out_specs=_row_spec(d),
        out_shape=jax.ShapeDtypeStruct((m, d), F32),
        compiler_params=_PROJ_PARAMS,
        name="outproj",
    )(xf, mab, mc, md, wo)


def _outin(xf, mab, mc, md, wo, layer, g_next, wb16):
    m, d = xf.shape
    return pl.pallas_call(
        _outin_kernel,
        grid=(m // TM,),
        in_specs=[_row_spec(d), _row_spec(2 * GROUP_W), _row_spec(GROUP_W), _row_spec(GROUP_W),
                  _layer_spec(wo, layer), pl.BlockSpec((1, d), lambda i: (0, 0)),
                  _layer_spec(wb16, layer + 1)],
        out_specs=[_row_spec(d)] + [_row_spec(w) for w in _Y_WIDTHS],
        out_shape=[jax.ShapeDtypeStruct((m, d), F32)]
                  + [jax.ShapeDtypeStruct((m, w), BF16) for w in _Y_WIDTHS],
        compiler_params=_PROJ_PARAMS,
        name="outin",
    )(xf, mab, mc, md, wo, g_next, wb16)


def _mixab_kernel(ya_ref, halo_ref, yb_ref, cw_ref, cb_ref, gg_ref, ws_ref, bfull_ref, p64_ref,
                  o_ref, u_scr):
    i = pl.program_id(1)
    W = GROUP_W
    TB = TQ * MIX_SUBTILES
    uh = halo_ref[:, 2 * W:3 * W].astype(F32) * halo_ref[:, 0:W].astype(F32)
    u_scr[0:HALO, :] = jnp.where(i > 0, uh, 0.0)
    u_scr[HALO:HALO + TB, :] = ya_ref[:, 2 * W:3 * W].astype(F32) * ya_ref[:, 0:W].astype(F32)
    for sb in range(MIX_SUBTILES):
        r0 = HALO + sb * TQ
        y = u_scr[r0 - 2:r0 - 2 + TQ, :] * cw_ref[0:1, :]
        y = y + u_scr[r0 - 1:r0 - 1 + TQ, :] * cw_ref[1:2, :]
        y = y + u_scr[r0:r0 + TQ, :] * cw_ref[2:3, :]
        rows = slice(sb * TQ, (sb + 1) * TQ)
        bg = ya_ref[rows, W:2 * W].astype(F32)
        za = ya_ref[rows, 3 * W:4 * W].astype(F32)
        o_ref[rows, 0:W] = (bg * (y + cb_ref[...]) * _silu(za)).astype(o_ref.dtype)

    t_idx = lax.broadcasted_iota(I32, (GMLP_BLOCK, GMLP_BLOCK), 0)
    s_idx = lax.broadcasted_iota(I32, (GMLP_BLOCK, GMLP_BLOCK), 1)
    causal = (t_idx >> CHUNK_SHIFT) >= (s_idx >> CHUNK_SHIFT)
    lane = lax.broadcasted_iota(I32, (1, W), 1)
    wms = [jnp.where(causal, ws_ref[hd], 0.0).astype(BF16) for hd in range(4)]

    def block(sb, carry):
        rows = pl.ds(pl.multiple_of(sb * TQ, TQ), TQ)
        u = _gelu(yb_ref[rows, 0:W].astype(F32))
        v = _gelu(yb_ref[rows, W:2 * W].astype(F32))
        zb = yb_ref[rows, 2 * W:3 * W].astype(F32)
        vn = v * lax.rsqrt(_seg_mean(v * v, p64_ref[...]) + EPS) * gg_ref[...]
        s = bfull_ref[...]
        for hd in range(4):
            vh = jnp.where((lane >> HEAD_SHIFT) == hd, vn, 0.0).astype(BF16)
            s = s + jnp.dot(wms[hd], vh, preferred_element_type=F32)
        o_ref[rows, W:2 * W] = (u * s * _silu(zb)).astype(o_ref.dtype)
        return carry

    lax.fori_loop(0, MIX_SUBTILES, block, 0)


def _mixab(ya, yb, cw, cb, gg, ws, bfull, p64):
    b, t, _ = ya.shape
    TB = TQ * MIX_SUBTILES
    nq = t // TB
    return pl.pallas_call(
        _mixab_kernel,
        grid=(b, nq),
        in_specs=[pl.BlockSpec((None, TB, 4 * GROUP_W), lambda bi, i: (bi, i, 0)),
                  pl.BlockSpec((None, HALO, 4 * GROUP_W),
                               lambda bi, i: (bi, jnp.maximum(i * (TB // HALO) - 1, 0), 0)),
                  pl.BlockSpec((None, TB, 3 * GROUP_W), lambda bi, i: (bi, i, 0)),
                  pl.BlockSpec(cw.shape, lambda bi, i: (0, 0)),
                  pl.BlockSpec(cb.shape, lambda bi, i: (0, 0)),
                  pl.BlockSpec(gg.shape, lambda bi, i: (0, 0)),
                  pl.BlockSpec(ws.shape, lambda bi, i: (0, 0, 0)),
                  pl.BlockSpec(bfull.shape, lambda bi, i: (0, 0)),
                  pl.BlockSpec(p64.shape, lambda bi, i: (0, 0))],
        out_specs=pl.BlockSpec((None, TB, 2 * GROUP_W), lambda bi, i: (bi, i, 0)),
        out_shape=jax.ShapeDtypeStruct((b, t, 2 * GROUP_W), BF16),
        scratch_shapes=[pltpu.VMEM((HALO + TB, GROUP_W), F32)],
        compiler_params=pltpu.CompilerParams(dimension_semantics=("arbitrary", "arbitrary"),
                                             vmem_limit_bytes=VMEM_LIMIT),
        name="mixab",
    )(ya, ya, yb, cw, cb, gg, ws, bfull, p64)


def _diff_kernel(q_ref, k_ref, v_ref, z_ref, qg_ref, kg_ref, lam_ref, subg_ref, p32_ref, p64_ref,
                 o_ref, kn_scr, qs_scr, s_scr, m_scr, l_scr, p_scr, acc_scr, *, lam_init, seq):
    TQ = TQ_DIFF
    i = pl.program_id(1)

    @pl.when(i == 0)
    def _():
        for c in range(seq // KC):
            kk = k_ref[c * KC:(c + 1) * KC, :].astype(F32)
            ms = _seg_mean(kk * kk, p32_ref[...])
            kn_scr[c * KC:(c + 1) * KC, :] = (kk * lax.rsqrt(ms + EPS) * kg_ref[...]).astype(BF16)

    lp = lam_ref[...]
    lam = (jnp.exp(jnp.sum(lp[0:1] * lp[1:2], axis=-1, keepdims=True))
           - jnp.exp(jnp.sum(lp[2:3] * lp[3:4], axis=-1, keepdims=True)) + lam_init)

    def sub_tile(sub, carry):
        rows = pl.ds(pl.multiple_of(sub * TQ, TQ), TQ)
        q0 = (i * DIFF_SUBTILES + sub) * TQ
        o = _diff_tile(q0, q_ref[rows, :].astype(F32), lam, v_ref, qg_ref, subg_ref, p32_ref, p64_ref,
                       kn_scr, qs_scr, s_scr, m_scr, l_scr, p_scr, acc_scr, lam_init)
        z = z_ref[rows, :].astype(F32)
        o_ref[rows, :] = (o * _silu(z)).astype(o_ref.dtype)
        return carry

    lax.fori_loop(0, DIFF_SUBTILES, sub_tile, 0)


def _diff_tile(q0, q, lam, v_ref, qg_ref, subg_ref, p32_ref, p64_ref,
               kn_scr, qs_scr, s_scr, m_scr, l_scr, p_scr, acc_scr, lam_init):
    TQ = TQ_DIFF
    KC = KC_DIFF
    nchunk = (q0 + TQ - 1) // KC + 1
    p32 = p32_ref[...]
    qn = q * lax.rsqrt(_seg_mean(q * q, p32) + EPS) * (qg_ref[...] * (DIFF_QD ** -0.5 * LOG2E))
    lane = lax.broadcasted_iota(I32, (1, GROUP_W), 1)
    for hc in range(NHC):
        qs_scr[hc * TQ:(hc + 1) * TQ, :] = jnp.where((lane >> QD_SHIFT) == hc, qn, 0.0).astype(BF16)

    m_scr[...] = jnp.full(m_scr.shape, -jnp.inf, F32)
    l_scr[...] = jnp.zeros(l_scr.shape, F32)
    acc_scr[...] = jnp.zeros(acc_scr.shape, F32)
    row = q0 + lax.broadcasted_iota(I32, (TQ, KC), 0)

    def s_body(c, carry):
        col = c * KC + lax.broadcasted_iota(I32, (TQ, KC), 1)
        dist = jnp.abs(row - col).astype(F32)
        dm = jnp.where((col >> CHUNK_SHIFT) <= (row >> CHUNK_SHIFT), dist, jnp.inf)
        kc = kn_scr[pl.ds(pl.multiple_of(c * KC, KC), KC), :]
        s_all = lax.dot_general(qs_scr[...], kc, (((1,), (1,)), ((), ())), preferred_element_type=F32)
        for h in range(NHC // 2):
            bias = (SLOPES_C[h] * LOG2E) * dm
            for j in range(2):
                rows = slice((2 * h + j) * TQ, (2 * h + j + 1) * TQ)
                s = s_all[rows, :] - bias
                s_scr[c, rows, :] = s
                m_scr[rows, :] = jnp.maximum(m_scr[rows, :], _fold_lanes(s, jnp.maximum))
        return carry

    lax.fori_loop(0, nchunk, s_body, 0)

    for hc in range(NHC):
        m = jnp.max(m_scr[hc * TQ:(hc + 1) * TQ, :], axis=1, keepdims=True)
        m_scr[hc * TQ:(hc + 1) * TQ, :] = jnp.broadcast_to(m, (TQ, LANES))

    def e_body(c, carry):
        for hc in range(NHC):
            rows = slice(hc * TQ, (hc + 1) * TQ)
            m = m_scr[rows, :]
            s = s_scr[c, rows, :]
            ps = [jnp.exp2(s[:, k * LANES:(k + 1) * LANES] - m) for k in range(KC // LANES)]
            l_scr[rows, :] += _fold_lanes(jnp.concatenate(ps, axis=1), jnp.add)
            p_scr[rows, :] = jnp.concatenate(ps, axis=1).astype(BF16)
        vc = v_ref[pl.ds(pl.multiple_of(c * KC, KC), KC), :]
        acc_scr[...] += jnp.dot(p_scr[...], vc, preferred_element_type=F32)
        return carry

    lax.fori_loop(0, nchunk, e_body, 0)

    o = jnp.zeros((TQ, GROUP_W), F32)
    for h in range(NHC // 2):
        r1 = slice(2 * h * TQ, (2 * h + 1) * TQ)
        r2 = slice((2 * h + 1) * TQ, (2 * h + 2) * TQ)
        l1 = jnp.sum(l_scr[r1, :], axis=1, keepdims=True)
        l2 = jnp.sum(l_scr[r2, :], axis=1, keepdims=True)
        o_h = acc_scr[r1, :] * (1.0 / l1) - acc_scr[r2, :] * (lam / l2)
        o = jnp.where((lane >> HEAD_SHIFT) == h, o_h, o)

    ms = _seg_mean(o * o, p64_ref[...])
    return o * lax.rsqrt(ms + EPS) * (subg_ref[...] * (1.0 - lam_init))


def _diff(yc, qg, kg, lam_p, subg, p32, p64, lam_init):
    TQ = TQ_DIFF
    KC = KC_DIFF
    TB = TQ * DIFF_SUBTILES
    b, t, _ = yc.shape
    nq = t // TB
    nkc = t // KC
    W = GROUP_W
    kern = functools.partial(_diff_kernel, lam_init=lam_init, seq=t)
    small = lambda a: pl.BlockSpec(a.shape, lambda bi, i: (0,) * a.ndim)
    return pl.pallas_call(
        kern,
        grid=(b, nq),
        in_specs=[pl.BlockSpec((None, TB, W), lambda bi, i: (bi, i, 0)),
                  pl.BlockSpec((None, t, W), lambda bi, i: (bi, 0, 1)),
                  pl.BlockSpec((None, t, W), lambda bi, i: (bi, 0, 2)),
                  pl.BlockSpec((None, TB, W), lambda bi, i: (bi, i, 3)),
                  small(qg), small(kg), small(lam_p), small(subg), small(p32), small(p64)],
        out_specs=pl.BlockSpec((None, TB, W), lambda bi, i: (bi, i, 0)),
        out_shape=jax.ShapeDtypeStruct((b, t, W), BF16),
        scratch_shapes=[pltpu.VMEM((t, W), BF16),
                        pltpu.VMEM((NHC * TQ, W), BF16),
                        pltpu.VMEM((nkc, NHC * TQ, KC), F32),
                        pltpu.VMEM((NHC * TQ, LANES), F32),
                        pltpu.VMEM((NHC * TQ, LANES), F32),
                        pltpu.VMEM((NHC * TQ, KC), BF16),
                        pltpu.VMEM((NHC * TQ, W), F32)],
        compiler_params=pltpu.CompilerParams(dimension_semantics=("arbitrary", "arbitrary"),
                                             vmem_limit_bytes=VMEM_LIMIT),
        name="diffattn",
    )(yc, yc, yc, yc, qg, kg, lam_p, subg, p32, p64)


def _dsa_kernel(q_ref, z_ref, iq_ref, ikwq_ref, kv_ref, ikw_ref, qg_ref, kg_ref, p64_ref, p64h_ref,
                tri_ref, o_ref, knv_scr, vt_scr, key_scr, dm_scr, *, seq, topk):
    i = pl.program_id(1)

    @pl.when(i == 0)
    def _():
        p64h = p64h_ref[...]
        for c in range(seq // 128):
            blk = kv_ref[c * 128:(c + 1) * 128, :].astype(F32)
            ms = _seg_mean(blk * blk, p64h)
            knv_scr[c * 128:(c + 1) * 128, :] = (blk * lax.rsqrt(ms + EPS) * kg_ref[...]).astype(BF16)
            vt = blk.T
            cc, off = divmod(c * 128, KC)
            vt_scr[cc, :, off:off + 128] = vt[DSA_HD:2 * DSA_HD, :].astype(BF16)

    def tile_pair(nchunk, pair, carry):
        tiles = []
        for s in range(DSA_PAIR):
            sub = pair * DSA_PAIR + s
            rows = pl.ds(sub * TQ, TQ)
            q0 = ((nchunk - 1) * DSA_SUBTILES + sub) * TQ

            iq_t = iq_ref[rows, :].astype(F32).T
            iw_t = ikwq_ref[rows, :].astype(F32).T[IDX_HD:IDX_HD + 8, :]
            wq = iw_t * (IDX_HEADS ** -0.5 * IDX_HD ** -0.5)
            zpad_i = jnp.zeros((128 - IDX_HD, TQ), F32)
            rhs_idx = jnp.concatenate(
                [jnp.concatenate([iq_t[IDX_HD * h:IDX_HD * (h + 1), :], zpad_i], axis=0)
                 for h in range(IDX_HEADS)], axis=1).astype(BF16)
            _dsa_index(nchunk, q0, rhs_idx, wq, ikw_ref, key_scr.at[s])
            tiles.append((rows, q0))

        taus = _dsa_search(nchunk, [key_scr.at[s] for s in range(DSA_PAIR)], [q0 for _, q0 in tiles], topk)

        for s, (rows, q0) in enumerate(tiles):
            q = q_ref[rows, :].astype(F32)
            qn = q * lax.rsqrt(_seg_mean(q * q, p64_ref[...]) + EPS) * (qg_ref[...] * (DSA_HD ** -0.5 * LOG2E))
            qn_t = qn.T
            zpad_q = jnp.zeros((128 - DSA_HD, TQ), F32)
            rhs_main = jnp.concatenate(
                [jnp.concatenate([qn_t[DSA_HD * h:DSA_HD * (h + 1), :], zpad_q], axis=0) for h in range(4)],
                axis=1).astype(BF16)
            o = _dsa_attend(nchunk, q0, taus[s], rhs_main, tri_ref, knv_scr, vt_scr, key_scr.at[s],
                            dm_scr.at[s], topk)
            z = z_ref[rows, :].astype(F32)
            o_ref[rows, :] = (o * _silu(z)).astype(o_ref.dtype)
        return carry

    assert TQ * DSA_SUBTILES == KC and DSA_SUBTILES % DSA_PAIR == 0
    for n in range(1, seq // KC + 1):
        @pl.when(i == n - 1)
        def _(n=n):
            for pair in range(DSA_SUBTILES // DSA_PAIR):
                tile_pair(n, pair, 0)


def _dsa_index(nchunk, q0, rhs_idx, wq, ikw_ref, key_scr):
    SB = DSA_SB
    n_sb = KC // SB
    qpos = q0 + lax.broadcasted_iota(I32, (SB, TQ), 1)
    for c in range(nchunk):
        for sb in range(n_sb):
            r0 = c * KC + sb * SB
            logit = jnp.dot(ikw_ref[pl.ds(r0, SB), :], rhs_idx, preferred_element_type=F32)
            sc = ((jnp.maximum(logit[:, 0:TQ], 0.0) * wq[0:1, :]
                   + jnp.maximum(logit[:, TQ:2 * TQ], 0.0) * wq[1:2, :])
                  + (jnp.maximum(logit[:, 2 * TQ:3 * TQ], 0.0) * wq[2:3, :]
                     + jnp.maximum(logit[:, 3 * TQ:4 * TQ], 0.0) * wq[3:4, :]))
            bits = lax.bitcast_convert_type(sc, I32)
            key = bits ^ ((bits >> 31) & jnp.int32(0x7FFFFFFF))
            key = jnp.where(key == -1, 0, key)
            if c == nchunk - 1:
                krow = r0 + lax.broadcasted_iota(I32, (SB, TQ), 0)
                key = jnp.where((krow >> CHUNK_SHIFT) <= (qpos >> CHUNK_SHIFT), key, jnp.int32(INT_MIN))
            key_scr[pl.ds(r0, SB), :] = key


def _dsa_search(nchunk, key_scrs, q0s, topk):
    kf = jnp.float32(topk)
    lowest = jnp.full((1, TQ), INT_MIN, I32)
    searched = [t for t, q0 in enumerate(q0s) if q0 + TQ > topk]
    if len(searched) < len(q0s):
        found = iter(_dsa_search(nchunk, [key_scrs[t] for t in searched], [q0s[t] for t in searched], topk)
                     if searched else ())
        return tuple(next(found) if t in searched else lowest for t in range(len(q0s)))

    def count_ge(key_scr, cand):
        acc = jnp.zeros((SUBLANES, TQ), F32)
        for c in range(nchunk):
            acc = acc + _fold_rows(jnp.where(key_scr[c * KC:(c + 1) * KC, :] >= cand, 1.0, 0.0), jnp.add)
        return jnp.sum(acc, axis=0, keepdims=True)

    def tau_step(it, us):
        out = []
        for key_scr, u in zip(key_scrs, us):
            cand_u = u | jnp.left_shift(jnp.int32(1), 31 - it)
            cand = cand_u ^ jnp.int32(INT_MIN)
            out.append(jnp.where(count_ge(key_scr, cand) >= kf, cand_u, u))
        return tuple(out)

    zero = jnp.zeros((1, TQ), I32)
    us = lax.fori_loop(0, 32, tau_step, (zero,) * len(key_scrs))
    return tuple(u ^ jnp.int32(INT_MIN) for u in us)


def _dsa_attend(nchunk, q0, tau, rhs_main, tri_ref, knv_scr, vt_scr, key_scr, dm_scr, topk):
    SB = DSA_SB
    n_sb = KC // SB
    kf = jnp.float32(topk)
    qpos = q0 + lax.broadcasted_iota(I32, (SB, TQ), 1)

    def over_chunks(body, init):
        acc = init
        for c in range(nchunk):
            acc = body(c, acc)
        return acc

    def gt_body(c, acc):
        key = key_scr[c * KC:(c + 1) * KC, :]
        return acc + _fold_rows(jnp.where(key > tau, 1.0, 0.0), jnp.add)

    n_gt = over_chunks(gt_body, jnp.zeros((SUBLANES, TQ), F32))
    need = kf - jnp.sum(n_gt, axis=0, keepdims=True)
    need = jnp.where(tau == jnp.int32(INT_MIN), 0.0, need)

    def dm_body(c, ties_before):
        for sb in range(n_sb):
            r0 = c * KC + sb * SB
            krow = r0 + lax.broadcasted_iota(I32, (SB, TQ), 0)
            dist = jnp.abs(qpos - krow).astype(F32)
            key = key_scr[pl.ds(r0, SB), :]
            is_tie = key == tau
            tie01 = jnp.where(is_tie, 1.0, 0.0)
            rank = ties_before + jnp.dot(tri_ref[...], tie01.astype(BF16), preferred_element_type=F32)
            inner = jnp.where(is_tie, jnp.where(rank <= need, dist, jnp.inf), jnp.inf)
            dm_scr[pl.ds(r0, SB), :] = jnp.where(key > tau, dist, inner)
            ties_before = ties_before + jnp.sum(_fold_rows(tie01, jnp.add), axis=0, keepdims=True)
        return ties_before

    over_chunks(dm_body, jnp.zeros((1, TQ), F32))

    slopes = [s * LOG2E for s in SLOPES_D]

    def attn_body(c, carry):
        ms, ls, acc = list(carry[:4]), list(carry[4:8]), carry[8]
        atts = [jnp.dot(knv_scr[pl.ds(c * KC + sb * SB, SB), :], rhs_main,
                        preferred_element_type=F32) for sb in range(n_sb)]
        for sb in range(n_sb):
            r0 = c * KC + sb * SB
            att = atts[sb]
            dm = dm_scr[pl.ds(r0, SB), :]
            alphas, probs = [], []
            for h in range(4):
                a = att[:, h * TQ:(h + 1) * TQ] - slopes[h] * dm
                m_new = jnp.maximum(ms[h], jnp.max(_fold_rows(a, jnp.maximum), axis=0, keepdims=True))
                m_use = jnp.where(m_new == -jnp.inf, 0.0, m_new)
                alpha = jnp.exp2(ms[h] - m_use)
                p = jnp.exp2(a - m_use)
                ls[h] = ls[h] * alpha + jnp.sum(_fold_rows(p, jnp.add), axis=0, keepdims=True)
                ms[h] = m_new
                alphas.append(alpha)
                probs.append(p.astype(BF16))
            pv = jnp.dot(vt_scr[c, :, sb * SB:(sb + 1) * SB], jnp.concatenate(probs, axis=1),
                         preferred_element_type=F32)
            acc = acc * jnp.concatenate(alphas, axis=1) + pv
        return (*ms, *ls, acc)

    neg = jnp.full((1, TQ), -jnp.inf, F32)
    zero = jnp.zeros((1, TQ), F32)
    res = over_chunks(attn_body, (neg,) * 4 + (zero,) * 4 + (jnp.zeros((DSA_HD, 4 * TQ), F32),))
    ls = res[4:8]
    out_t = res[8]
    o_t = jnp.concatenate([out_t[:, h * TQ:(h + 1) * TQ] * (1.0 / ls[h]) for h in range(4)], axis=0)
    return o_t.T


def _dsa(yd, qg, kg, p64, p64h):
    b, t, _ = yd.shape
    TB = TQ * DSA_SUBTILES
    nq = t // TB
    nkc = t // KC
    W = GROUP_W
    topk = min(DSA_TOPK_MAX, t // 4)
    kern = functools.partial(_dsa_kernel, seq=t, topk=topk)
    small = lambda a: pl.BlockSpec(a.shape, lambda bi, i: (0,) * a.ndim)
    tri = jnp.asarray(np.tril(np.ones((DSA_SB, DSA_SB), np.float32)), dtype=BF16)
    return pl.pallas_call(
        kern,
        grid=(b, nq),
        in_specs=[pl.BlockSpec((None, TB, W), lambda bi, i: (bi, i, 0)),
                  pl.BlockSpec((None, TB, W), lambda bi, i: (bi, i, 1)),
                  pl.BlockSpec((None, TB, 128), lambda bi, i: (bi, i, 5)),
                  pl.BlockSpec((None, TB, 128), lambda bi, i: (bi, i, 6)),
                  pl.BlockSpec((None, t, 128), lambda bi, i: (bi, 0, 4)),
                  pl.BlockSpec((None, t, 128), lambda bi, i: (bi, 0, 6)),
                  small(qg), small(kg), small(p64), small(p64h), small(tri)],
        out_specs=pl.BlockSpec((None, TB, W), lambda bi, i: (bi, i, 0)),
        out_shape=jax.ShapeDtypeStruct((b, t, W), BF16),
        scratch_shapes=[pltpu.VMEM((t, 128), BF16),
                        pltpu.VMEM((nkc, DSA_HD, KC), BF16),
                        pltpu.VMEM((DSA_PAIR, t, TQ), I32),
                        pltpu.VMEM((DSA_PAIR, t, TQ), F32)],
        compiler_params=pltpu.CompilerParams(dimension_semantics=("arbitrary", "arbitrary"),
                                             vmem_limit_bytes=VMEM_LIMIT),
        name="dsa",
    )(yd, yd, yd, yd, yd, yd, qg, kg, p64, p64h, tri)


def kernel(x, norm_g, w_in, conv_w, conv_b, gmlp_g, gmlp_ws, gmlp_b, diff_qg, diff_kg, diff_lam,
           diff_subg, dsa_qg, dsa_kg, w_out):
    b, t, d = x.shape
    depth = w_in.shape[0]
    p32 = _block_diag_mean(GROUP_W, 32)
    p64 = _block_diag_mean(GROUP_W, 64)
    p64h = _block_diag_mean(128, 64)
    w_in16 = _wprep(w_in)
    w_out16 = _wprep(w_out)
    xf = x.reshape(b * t, d)
    ys = _inproj(xf, norm_g[0].reshape(1, d), w_in16, 0)
    for l in range(depth):
        ya, yb, yc, yd = (a.reshape(b, t, a.shape[-1]) for a in ys)
        bfull = jnp.repeat(gmlp_b[l].T, GROUP_W // 4, axis=1)
        mab = _mixab(ya, yb, conv_w[l], conv_b[l].reshape(1, -1), gmlp_g[l].reshape(1, -1),
                     gmlp_ws[l], bfull, p64)
        lam_init = 0.8 - 0.6 * math.exp(-0.3 * l)
        mc = _diff(yc, jnp.tile(diff_qg[l], 8).reshape(1, -1), jnp.tile(diff_kg[l], 8).reshape(1, -1),
                   diff_lam[l], diff_subg[l].reshape(1, -1), p32, p64, lam_init)
        md = _dsa(yd, jnp.tile(dsa_qg[l], 4).reshape(1, -1), jnp.tile(dsa_kg[l], 2).reshape(1, -1),
                  p64, p64h)
        mixes = (mab.reshape(b * t, -1), mc.reshape(b * t, -1), md.reshape(b * t, -1))
        if l + 1 < depth:
            xf, *ys = _outin(xf, *mixes, w_out16, l, norm_g[l + 1].reshape(1, d), w_in16)
        else:
            xf = _outproj(xf, *mixes, w_out16, l)
    return xf.reshape(b, t, d)
```

```python
import functools
import math

import numpy as np
import jax
import jax.numpy as jnp
from jax import lax
from jax.experimental import pallas as pl
from jax.experimental.pallas import tpu as pltpu

F32 = jnp.float32
BF16 = jnp.bfloat16
I32 = jnp.int32

GROUP_W = 256
CHUNK = 64
CHUNK_SHIFT = CHUNK.bit_length() - 1
HEAD_SHIFT = 6
QD_SHIFT = 5
GMLP_BLOCK = 128
DIFF_QD = 32
DSA_HD = 64
IDX_HD = 32
IDX_HEADS = 4
DSA_TOPK_MAX = 256
EPS = 1e-6
LOG2E = math.log2(math.e)
INT_MIN = -2 ** 31
LANES = 128
SUBLANES = 8

TQ = 128
TQ_DIFF = 256
DIFF_SUBTILES = 4
DSA_SUBTILES = 4
DSA_SB = 256
DSA_PAIR = 4
MIX_SUBTILES = 8
KC = 512
KC_DIFF = 512
TM = 512
HALO = 16
NHC = 8

_SLOPES = 2.0 ** (-8.0 * np.arange(1, 9) / 8.0)
SLOPES_C = [float(s) for s in _SLOPES[0::2]]
SLOPES_D = [float(s) for s in _SLOPES[1::2]]

VMEM_LIMIT = 56 * 1024 * 1024


def _block_diag_mean(width, seg):
    idx = np.arange(width) // seg
    return jnp.asarray((idx[:, None] == idx[None, :]).astype(np.float32) / seg, dtype=BF16)


def _seg_mean(x2, p):
    hi = x2.astype(BF16)
    lo = (x2 - hi.astype(F32)).astype(BF16)
    return (jnp.dot(hi, p, preferred_element_type=F32)
            + jnp.dot(lo, p, preferred_element_type=F32))


def _silu(z):
    return z * jax.nn.sigmoid(z)


def _gelu(x):
    return 0.5 * x * (1.0 + lax.erf(x * (2.0 ** -0.5)))


def _fold_rows(x, op, stop=SUBLANES):
    r = x.shape[0]
    while r > stop:
        r //= 2
        x = op(x[:r], x[r:])
    return x


def _fold_lanes(x, op):
    c = x.shape[1]
    while c > LANES:
        c //= 2
        x = op(x[:, :c], x[:, c:])
    return x


def _wprep_kernel(w_ref, o_ref, *, n_valid):
    col = pl.program_id(1) * LANES + lax.broadcasted_iota(I32, (1, LANES), 1)
    o_ref[...] = jnp.where(col < n_valid, w_ref[...], 0.0).astype(BF16)


def _wprep(w):
    depth, d, n = w.shape
    nt = pl.cdiv(n, LANES)
    return pl.pallas_call(
        functools.partial(_wprep_kernel, n_valid=n),
        grid=(depth, nt),
        in_specs=[pl.BlockSpec((None, d, LANES), lambda l, j: (l, 0, j))],
        out_specs=pl.BlockSpec((None, d, LANES), lambda l, j: (l, 0, j)),
        out_shape=jax.ShapeDtypeStruct((depth, d, nt * LANES), BF16),
        compiler_params=pltpu.CompilerParams(dimension_semantics=("arbitrary", "arbitrary")),
        name="wprep",
    )(w)


_A0, _B0, _C0, _D0 = 0, 4 * GROUP_W, 7 * GROUP_W, 11 * GROUP_W
_DQ, _DKV, _DZ, _DIQ, _DEND = _D0, _D0 + 256, _D0 + 384, _D0 + 640, _D0 + 896
YD_W = 896


_Y_WIDTHS = (_B0 - _A0, _C0 - _B0, _D0 - _C0, YD_W)


def _norm_project(x, g_ref, w_ref, ya_ref, yb_ref, yc_ref, yd_ref):
    ms = jnp.mean(x * x, axis=-1, keepdims=True)
    xn = (x * lax.rsqrt(ms + EPS) * g_ref[...]).astype(BF16)

    def proj(lo, hi):
        return jnp.dot(xn, w_ref[:, lo:hi], preferred_element_type=F32).astype(BF16)

    ya_ref[...] = proj(_A0, _B0)
    yb_ref[...] = proj(_B0, _C0)
    yc_ref[...] = proj(_C0, _D0)
    yd_ref[:, 0:256] = proj(_DQ, _DKV)
    yd_ref[:, 256:512] = proj(_DZ, _DIQ)
    yd_ref[:, 512:640] = proj(_DKV, _DZ)
    yd_ref[:, 640:896] = proj(_DIQ, _DEND)


def _mix_project(x_ref, mab_ref, mc_ref, md_ref, wo_ref):
    acc = x_ref[...]
    acc = acc + jnp.dot(mab_ref[...], wo_ref[0:2 * GROUP_W, :], preferred_element_type=F32)
    acc = acc + jnp.dot(mc_ref[...], wo_ref[2 * GROUP_W:3 * GROUP_W, :], preferred_element_type=F32)
    return acc + jnp.dot(md_ref[...], wo_ref[3 * GROUP_W:4 * GROUP_W, :], preferred_element_type=F32)


def _inproj_kernel(x_ref, g_ref, w_ref, ya_ref, yb_ref, yc_ref, yd_ref):
    _norm_project(x_ref[...], g_ref, w_ref, ya_ref, yb_ref, yc_ref, yd_ref)


def _outproj_kernel(x_ref, mab_ref, mc_ref, md_ref, wo_ref, o_ref):
    o_ref[...] = _mix_project(x_ref, mab_ref, mc_ref, md_ref, wo_ref)


def _outin_kernel(x_ref, mab_ref, mc_ref, md_ref, wo_ref, g_ref, w_ref,
                  o_ref, ya_ref, yb_ref, yc_ref, yd_ref):
    x_new = _mix_project(x_ref, mab_ref, mc_ref, md_ref, wo_ref)
    o_ref[...] = x_new
    _norm_project(x_new, g_ref, w_ref, ya_ref, yb_ref, yc_ref, yd_ref)


def _row_spec(width):
    return pl.BlockSpec((TM, width), lambda i: (i, 0))


def _layer_spec(w, layer):
    return pl.BlockSpec((None,) + w.shape[1:], lambda i: (layer, 0, 0))


_PROJ_PARAMS = pltpu.CompilerParams(dimension_semantics=("arbitrary",), vmem_limit_bytes=VMEM_LIMIT)


def _inproj(xf, g, wb16, layer):
    m, d = xf.shape
    return pl.pallas_call(
        _inproj_kernel,
        grid=(m // TM,),
        in_specs=[_row_spec(d), pl.BlockSpec((1, d), lambda i: (0, 0)), _layer_spec(wb16, layer)],
        out_specs=[_row_spec(w) for w in _Y_WIDTHS],
        out_shape=[jax.ShapeDtypeStruct((m, w), BF16) for w in _Y_WIDTHS],
        compiler_params=_PROJ_PARAMS,
        name="inproj",
    )(xf, g, wb16)


def _outproj(xf, mab, mc, md, wo, layer):
    m, d = xf.shape
    return pl.pallas_call(
        _outproj_kernel,
        grid=(m // TM,),
        in_specs=[_row_spec(d), _row_spec(2 * GROUP_W), _row_spec(GROUP_W), _row_spec(GROUP_W),
                  _layer_spec(wo, layer)],
        out_specs=_row_spec(d),
        out_shape=jax.ShapeDtypeStruct((m, d), F32),
        compiler_params=_PROJ_PARAMS,
        name="outproj",
    )(xf, mab, mc, md, wo)


def _outin(xf, mab, mc, md, wo, layer, g_next, wb16):
    m, d = xf.shape
    return pl.pallas_call(
        _outin_kernel,
        grid=(m // TM,),
        in_specs=[_row_spec(d), _row_spec(2 * GROUP_W), _row_spec(GROUP_W), _row_spec(GROUP_W),
                  _layer_spec(wo, layer), pl.BlockSpec((1, d), lambda i: (0, 0)),
                  _layer_spec(wb16, layer + 1)],
        out_specs=[_row_spec(d)] + [_row_spec(w) for w in _Y_WIDTHS],
        out_shape=[jax.ShapeDtypeStruct((m, d), F32)]
                  + [jax.ShapeDtypeStruct((m, w), BF16) for w in _Y_WIDTHS],
        compiler_params=_PROJ_PARAMS,
        name="outin",
    )(xf, mab, mc, md, wo, g_next, wb16)


def _mixab_kernel(ya_ref, halo_ref, yb_ref, cw_ref, cb_ref, gg_ref, ws_ref, bfull_ref, p64_ref,
                  o_ref, u_scr):
    i = pl.program_id(1)
    W = GROUP_W
    TB = TQ * MIX_SUBTILES
    uh = halo_ref[:, 2 * W:3 * W].astype(F32) * halo_ref[:, 0:W].astype(F32)
    u_scr[0:HALO, :] = jnp.where(i > 0, uh, 0.0)
    u_scr[HALO:HALO + TB, :] = ya_ref[:, 2 * W:3 * W].astype(F32) * ya_ref[:, 0:W].astype(F32)
    for sb in range(MIX_SUBTILES):
        r0 = HALO + sb * TQ
        y = u_scr[r0 - 2:r0 - 2 + TQ, :] * cw_ref[0:1, :]
        y = y + u_scr[r0 - 1:r0 - 1 + TQ, :] * cw_ref[1:2, :]
        y = y + u_scr[r0:r0 + TQ, :] * cw_ref[2:3, :]
        rows = slice(sb * TQ, (sb + 1) * TQ)
        bg = ya_ref[rows, W:2 * W].astype(F32)
        za = ya_ref[rows, 3 * W:4 * W].astype(F32)
        o_ref[rows, 0:W] = (bg * (y + cb_ref[...]) * _silu(za)).astype(o_ref.dtype)

    t_idx = lax.broadcasted_iota(I32, (GMLP_BLOCK, GMLP_BLOCK), 0)
    s_idx = lax.broadcasted_iota(I32, (GMLP_BLOCK, GMLP_BLOCK), 1)
    causal = (t_idx >> CHUNK_SHIFT) >= (s_idx >> CHUNK_SHIFT)
    lane = lax.broadcasted_iota(I32, (1, W), 1)
    wms = [jnp.where(causal, ws_ref[hd], 0.0).astype(BF16) for hd in range(4)]

    def block(sb, carry):
        rows = pl.ds(pl.multiple_of(sb * TQ, TQ), TQ)
        u = _gelu(yb_ref[rows, 0:W].astype(F32))
        v = _gelu(yb_ref[rows, W:2 * W].astype(F32))
        zb = yb_ref[rows, 2 * W:3 * W].astype(F32)
        vn = v * lax.rsqrt(_seg_mean(v * v, p64_ref[...]) + EPS) * gg_ref[...]
        s = bfull_ref[...]
        for hd in range(4):
            vh = jnp.where((lane >> HEAD_SHIFT) == hd, vn, 0.0).astype(BF16)
            s = s + jnp.dot(wms[hd], vh, preferred_element_type=F32)
        o_ref[rows, W:2 * W] = (u * s * _silu(zb)).astype(o_ref.dtype)
        return carry

    lax.fori_loop(0, MIX_SUBTILES, block, 0)


def _mixab(ya, yb, cw, cb, gg, ws, bfull, p64):
    b, t, _ = ya.shape
    TB = TQ * MIX_SUBTILES
    nq = t // TB
    return pl.pallas_call(
        _mixab_kernel,
        grid=(b, nq),
        in_specs=[pl.BlockSpec((None, TB, 4 * GROUP_W), lambda bi, i: (bi, i, 0)),
                  pl.BlockSpec((None, HALO, 4 * GROUP_W),
                               lambda bi, i: (bi, jnp.maximum(i * (TB // HALO) - 1, 0), 0)),
                  pl.BlockSpec((None, TB, 3 * GROUP_W), lambda bi, i: (bi, i, 0)),
                  pl.BlockSpec(cw.shape, lambda bi, i: (0, 0)),
                  pl.BlockSpec(cb.shape, lambda bi, i: (0, 0)),
                  pl.BlockSpec(gg.shape, lambda bi, i: (0, 0)),
                  pl.BlockSpec(ws.shape, lambda bi, i: (0, 0, 0)),
                  pl.BlockSpec(bfull.shape, lambda bi, i: (0, 0)),
                  pl.BlockSpec(p64.shape, lambda bi, i: (0, 0))],
        out_specs=pl.BlockSpec((None, TB, 2 * GROUP_W), lambda bi, i: (bi, i, 0)),
        out_shape=jax.ShapeDtypeStruct((b, t, 2 * GROUP_W), BF16),
        scratch_shapes=[pltpu.VMEM((HALO + TB, GROUP_W), F32)],
        compiler_params=pltpu.CompilerParams(dimension_semantics=("arbitrary", "arbitrary"),
                                             vmem_limit_bytes=VMEM_LIMIT),
        name="mixab",
    )(ya, ya, yb, cw, cb, gg, ws, bfull, p64)


def _diff_kernel(q_ref, k_ref, v_ref, z_ref, qg_ref, kg_ref, lam_ref, subg_ref, p32_ref, p64_ref,
                 o_ref, kn_scr, qs_scr, s_scr, m_scr, l_scr, p_scr, acc_scr, *, lam_init, seq):
    TQ = TQ_DIFF
    i = pl.program_id(1)

    @pl.when(i == 0)
    def _():
        for c in range(seq // KC):
            kk = k_ref[c * KC:(c + 1) * KC, :].astype(F32)
            ms = _seg_mean(kk * kk, p32_ref[...])
            kn_scr[c * KC:(c + 1) * KC, :] = (kk * lax.rsqrt(ms + EPS) * kg_ref[...]).astype(BF16)

    lp = lam_ref[...]
    lam = (jnp.exp(jnp.sum(lp[0:1] * lp[1:2], axis=-1, keepdims=True))
           - jnp.exp(jnp.sum(lp[2:3] * lp[3:4], axis=-1, keepdims=True)) + lam_init)

    def sub_tile(sub, carry):
        rows = pl.ds(pl.multiple_of(sub * TQ, TQ), TQ)
        q0 = (i * DIFF_SUBTILES + sub) * TQ
        o = _diff_tile(q0, q_ref[rows, :].astype(F32), lam, v_ref, qg_ref, subg_ref, p32_ref, p64_ref,
                       kn_scr, qs_scr, s_scr, m_scr, l_scr, p_scr, acc_scr, lam_init)
        z = z_ref[rows, :].astype(F32)
        o_ref[rows, :] = (o * _silu(z)).astype(o_ref.dtype)
        return carry

    lax.fori_loop(0, DIFF_SUBTILES, sub_tile, 0)


def _diff_tile(q0, q, lam, v_ref, qg_ref, subg_ref, p32_ref, p64_ref,
               kn_scr, qs_scr, s_scr, m_scr, l_scr, p_scr, acc_scr, lam_init):
    TQ = TQ_DIFF
    KC = KC_DIFF
    nchunk = (q0 + TQ - 1) // KC + 1
    p32 = p32_ref[...]
    qn = q * lax.rsqrt(_seg_mean(q * q, p32) + EPS) * (qg_ref[...] * (DIFF_QD ** -0.5 * LOG2E))
    lane = lax.broadcasted_iota(I32, (1, GROUP_W), 1)
    for hc in range(NHC):
        qs_scr[hc * TQ:(hc + 1) * TQ, :] = jnp.where((lane >> QD_SHIFT) == hc, qn, 0.0).astype(BF16)

    m_scr[...] = jnp.full(m_scr.shape, -jnp.inf, F32)
    l_scr[...] = jnp.zeros(l_scr.shape, F32)
    acc_scr[...] = jnp.zeros(acc_scr.shape, F32)
    row = q0 + lax.broadcasted_iota(I32, (TQ, KC), 0)

    def s_body(c, carry):
        col = c * KC + lax.broadcasted_iota(I32, (TQ, KC), 1)
        dist = jnp.abs(row - col).astype(F32)
        dm = jnp.where((col >> CHUNK_SHIFT) <= (row >> CHUNK_SHIFT), dist, jnp.inf)
        kc = kn_scr[pl.ds(pl.multiple_of(c * KC, KC), KC), :]
        s_all = lax.dot_general(qs_scr[...], kc, (((1,), (1,)), ((), ())), preferred_element_type=F32)
        for h in range(NHC // 2):
            bias = (SLOPES_C[h] * LOG2E) * dm
            for j in range(2):
                rows = slice((2 * h + j) * TQ, (2 * h + j + 1) * TQ)
                s = s_all[rows, :] - bias
                s_scr[c, rows, :] = s
                m_scr[rows, :] = jnp.maximum(m_scr[rows, :], _fold_lanes(s, jnp.maximum))
        return carry

    lax.fori_loop(0, nchunk, s_body, 0)

    for hc in range(NHC):
        m = jnp.max(m_scr[hc * TQ:(hc + 1) * TQ, :], axis=1, keepdims=True)
        m_scr[hc * TQ:(hc + 1) * TQ, :] = jnp.broadcast_to(m, (TQ, LANES))

    def e_body(c, carry):
        for hc in range(NHC):
            rows = slice(hc * TQ, (hc + 1) * TQ)
            m = m_scr[rows, :]
            s = s_scr[c, rows, :]
            ps = [jnp.exp2(s[:, k * LANES:(k + 1) * LANES] - m) for k in range(KC // LANES)]
            l_scr[rows, :] += _fold_lanes(jnp.concatenate(ps, axis=1), jnp.add)
            p_scr[rows, :] = jnp.concatenate(ps, axis=1).astype(BF16)
        vc = v_ref[pl.ds(pl.multiple_of(c * KC, KC), KC), :]
        acc_scr[...] += jnp.dot(p_scr[...], vc, preferred_element_type=F32)
        return carry

    lax.fori_loop(0, nchunk, e_body, 0)

    o = jnp.zeros((TQ, GROUP_W), F32)
    for h in range(NHC // 2):
        r1 = slice(2 * h * TQ, (2 * h + 1) * TQ)
        r2 = slice((2 * h + 1) * TQ, (2 * h + 2) * TQ)
        l1 = jnp.sum(l_scr[r1, :], axis=1, keepdims=True)
        l2 = jnp.sum(l_scr[r2, :], axis=1, keepdims=True)
        o_h = acc_scr[r1, :] * (1.0 / l1) - acc_scr[r2, :] * (lam / l2)
        o = jnp.where((lane >> HEAD_SHIFT) == h, o_h, o)

    ms = _seg_mean(o * o, p64_ref[...])
    return o * lax.rsqrt(ms + EPS) * (subg_ref[...] * (1.0 - lam_init))


def _diff(yc, qg, kg, lam_p, subg, p32, p64, lam_init):
    TQ = TQ_DIFF
    KC = KC_DIFF
    TB = TQ * DIFF_SUBTILES
    b, t, _ = yc.shape
    nq = t // TB
    nkc = t // KC
    W = GROUP_W
    kern = functools.partial(_diff_kernel, lam_init=lam_init, seq=t)
    small = lambda a: pl.BlockSpec(a.shape, lambda bi, i: (0,) * a.ndim)
    return pl.pallas_call(
        kern,
        grid=(b, nq),
        in_specs=[pl.BlockSpec((None, TB, W), lambda bi, i: (bi, i, 0)),
                  pl.BlockSpec((None, t, W), lambda bi, i: (bi, 0, 1)),
                  pl.BlockSpec((None, t, W), lambda bi, i: (bi, 0, 2)),
                  pl.BlockSpec((None, TB, W), lambda bi, i: (bi, i, 3)),
                  small(qg), small(kg), small(lam_p), small(subg), small(p32), small(p64)],
        out_specs=pl.BlockSpec((None, TB, W), lambda bi, i: (bi, i, 0)),
        out_shape=jax.ShapeDtypeStruct((b, t, W), BF16),
        scratch_shapes=[pltpu.VMEM((t, W), BF16),
                        pltpu.VMEM((NHC * TQ, W), BF16),
                        pltpu.VMEM((nkc, NHC * TQ, KC), F32),
                        pltpu.VMEM((NHC * TQ, LANES), F32),
                        pltpu.VMEM((NHC * TQ, LANES), F32),
                        pltpu.VMEM((NHC * TQ, KC), BF16),
                        pltpu.VMEM((NHC * TQ, W), F32)],
        compiler_params=pltpu.CompilerParams(dimension_semantics=("arbitrary", "arbitrary"),
                                             vmem_limit_bytes=VMEM_LIMIT),
        name="diffattn",
    )(yc, yc, yc, yc, qg, kg, lam_p, subg, p32, p64)


def _dsa_kernel(q_ref, z_ref, iq_ref, ikwq_ref, kv_ref, ikw_ref, qg_ref, kg_ref, p64_ref, p64h_ref,
                tri_ref, o_ref, knv_scr, vt_scr, key_scr, dm_scr, *, seq, topk):
    i = pl.program_id(1)

    @pl.when(i == 0)
    def _():
        p64h = p64h_ref[...]
        for c in range(seq // 128):
            blk = kv_ref[c * 128:(c + 1) * 128, :].astype(F32)
            ms = _seg_mean(blk * blk, p64h)
            knv_scr[c * 128:(c + 1) * 128, :] = (blk * lax.rsqrt(ms + EPS) * kg_ref[...]).astype(BF16)
            vt = blk.T
            cc, off = divmod(c * 128, KC)
            vt_scr[cc, :, off:off + 128] = vt[DSA_HD:2 * DSA_HD, :].astype(BF16)

    def tile_pair(nchunk, pair, carry):
        tiles = []
        for s in range(DSA_PAIR):
            sub = pair * DSA_PAIR + s
            rows = pl.ds(sub * TQ, TQ)
            q0 = ((nchunk - 1) * DSA_SUBTILES + sub) * TQ

            iq_t = iq_ref[rows, :].astype(F32).T
            iw_t = ikwq_ref[rows, :].astype(F32).T[IDX_HD:IDX_HD + 8, :]
            wq = iw_t * (IDX_HEADS ** -0.5 * IDX_HD ** -0.5)
            zpad_i = jnp.zeros((128 - IDX_HD, TQ), F32)
            rhs_idx = jnp.concatenate(
                [jnp.concatenate([iq_t[IDX_HD * h:IDX_HD * (h + 1), :], zpad_i], axis=0)
                 for h in range(IDX_HEADS)], axis=1).astype(BF16)
            _dsa_index(nchunk, q0, rhs_idx, wq, ikw_ref, key_scr.at[s])
            tiles.append((rows, q0))

        taus = _dsa_search(nchunk, [key_scr.at[s] for s in range(DSA_PAIR)], [q0 for _, q0 in tiles], topk)

        for s, (rows, q0) in enumerate(tiles):
            q = q_ref[rows, :].astype(F32)
            qn = q * lax.rsqrt(_seg_mean(q * q, p64_ref[...]) + EPS) * (qg_ref[...] * (DSA_HD ** -0.5 * LOG2E))
            qn_t = qn.T
            zpad_q = jnp.zeros((128 - DSA_HD, TQ), F32)
            rhs_main = jnp.concatenate(
                [jnp.concatenate([qn_t[DSA_HD * h:DSA_HD * (h + 1), :], zpad_q], axis=0) for h in range(4)],
                axis=1).astype(BF16)
            o = _dsa_attend(nchunk, q0, taus[s], rhs_main, tri_ref, knv_scr, vt_scr, key_scr.at[s],
                            dm_scr.at[s], topk)
            z = z_ref[rows, :].astype(F32)
            o_ref[rows, :] = (o * _silu(z)).astype(o_ref.dtype)
        return carry

    assert TQ * DSA_SUBTILES == KC and DSA_SUBTILES % DSA_PAIR == 0
    for n in range(1, seq // KC + 1):
        @pl.when(i == n - 1)
        def _(n=n):
            for pair in range(DSA_SUBTILES // DSA_PAIR):
                tile_pair(n, pair, 0)


def _dsa_index(nchunk, q0, rhs_idx, wq, ikw_ref, key_scr):
    SB = DSA_SB
    n_sb = KC // SB
    qpos = q0 + lax.broadcasted_iota(I32, (SB, TQ), 1)
    for c in range(nchunk):
        for sb in range(n_sb):
            r0 = c * KC + sb * SB
            logit = jnp.dot(ikw_ref[pl.ds(r0, SB), :], rhs_idx, preferred_element_type=F32)
            sc = ((jnp.maximum(logit[:, 0:TQ], 0.0) * wq[0:1, :]
                   + jnp.maximum(logit[:, TQ:2 * TQ], 0.0) * wq[1:2, :])
                  + (jnp.maximum(logit[:, 2 * TQ:3 * TQ], 0.0) * wq[2:3, :]
                     + jnp.maximum(logit[:, 3 * TQ:4 * TQ], 0.0) * wq[3:4, :]))
            bits = lax.bitcast_convert_type(sc, I32)
            key = bits ^ ((bits >> 31) & jnp.int32(0x7FFFFFFF))
            key = jnp.where(key == -1, 0, key)
            if c == nchunk - 1:
                krow = r0 + lax.broadcasted_iota(I32, (SB, TQ), 0)
                key = jnp.where((krow >> CHUNK_SHIFT) <= (qpos >> CHUNK_SHIFT), key, jnp.int32(INT_MIN))
            key_scr[pl.ds(r0, SB), :] = key


def _dsa_search(nchunk, key_scrs, q0s, topk):
    kf = jnp.float32(topk)
    lowest = jnp.full((1, TQ), INT_MIN, I32)
    searched = [t for t, q0 in enumerate(q0s) if q0 + TQ > topk]
    if len(searched) < len(q0s):
        found = iter(_dsa_search(nchunk, [key_scrs[t] for t in searched], [q0s[t] for t in searched], topk)
                     if searched else ())
        return tuple(next(found) if t in searched else lowest for t in range(len(q0s)))

    def count_ge(key_scr, cand):
        acc = jnp.zeros((SUBLANES, TQ), F32)
        for c in range(nchunk):
            acc = acc + _fold_rows(jnp.where(key_scr[c * KC:(c + 1) * KC, :] >= cand, 1.0, 0.0), jnp.add)
        return jnp.sum(acc, axis=0, keepdims=True)

    def tau_step(it, us):
        out = []
        for key_scr, u in zip(key_scrs, us):
            cand_u = u | jnp.left_shift(jnp.int32(1), 31 - it)
            cand = cand_u ^ jnp.int32(INT_MIN)
            out.append(jnp.where(count_ge(key_scr, cand) >= kf, cand_u, u))
        return tuple(out)

    zero = jnp.zeros((1, TQ), I32)
    us = lax.fori_loop(0, 32, tau_step, (zero,) * len(key_scrs))
    return tuple(u ^ jnp.int32(INT_MIN) for u in us)


def _dsa_attend(nchunk, q0, tau, rhs_main, tri_ref, knv_scr, vt_scr, key_scr, dm_scr, topk):
    SB = DSA_SB
    n_sb = KC // SB
    kf = jnp.float32(topk)
    qpos = q0 + lax.broadcasted_iota(I32, (SB, TQ), 1)

    def over_chunks(body, init):
        acc = init
        for c in range(nchunk):
            acc = body(c, acc)
        return acc

    def gt_body(c, acc):
        key = key_scr[c * KC:(c + 1) * KC, :]
        return acc + _fold_rows(jnp.where(key > tau, 1.0, 0.0), jnp.add)

    n_gt = over_chunks(gt_body, jnp.zeros((SUBLANES, TQ), F32))
    need = kf - jnp.sum(n_gt, axis=0, keepdims=True)
    need = jnp.where(tau == jnp.int32(INT_MIN), 0.0, need)

    def dm_body(c, ties_before):
        for sb in range(n_sb):
            r0 = c * KC + sb * SB
            krow = r0 + lax.broadcasted_iota(I32, (SB, TQ), 0)
            dist = jnp.abs(qpos - krow).astype(F32)
            key = key_scr[pl.ds(r0, SB), :]
            is_tie = key == tau
            tie01 = jnp.where(is_tie, 1.0, 0.0)
            rank = ties_before + jnp.dot(tri_ref[...], tie01.astype(BF16), preferred_element_type=F32)
            inner = jnp.where(is_tie, jnp.where(rank <= need, dist, jnp.inf), jnp.inf)
            dm_scr[pl.ds(r0, SB), :] = jnp.where(key > tau, dist, inner)
            ties_before = ties_before + jnp.sum(_fold_rows(tie01, jnp.add), axis=0, keepdims=True)
        return ties_before

    over_chunks(dm_body, jnp.zeros((1, TQ), F32))

    slopes = [s * LOG2E for s in SLOPES_D]

    def attn_body(c, carry):
        ms, ls, acc = list(carry[:4]), list(carry[4:8]), carry[8]
        atts = [jnp.dot(knv_scr[pl.ds(c * KC + sb * SB, SB), :], rhs_main,
                        preferred_element_type=F32) for sb in range(n_sb)]
        for sb in range(n_sb):
            r0 = c * KC + sb * SB
            att = atts[sb]
            dm = dm_scr[pl.ds(r0, SB), :]
            alphas, probs = [], []
            for h in range(4):
                a = att[:, h * TQ:(h + 1) * TQ] - slopes[h] * dm
                m_new = jnp.maximum(ms[h], jnp.max(_fold_rows(a, jnp.maximum), axis=0, keepdims=True))
                m_use = jnp.where(m_new == -jnp.inf, 0.0, m_new)
                alpha = jnp.exp2(ms[h] - m_use)
                p = jnp.exp2(a - m_use)
                ls[h] = ls[h] * alpha + jnp.sum(_fold_rows(p, jnp.add), axis=0, keepdims=True)
                ms[h] = m_new
                alphas.append(alpha)
                probs.append(p.astype(BF16))
            pv = jnp.dot(vt_scr[c, :, sb * SB:(sb + 1) * SB], jnp.concatenate(probs, axis=1),
                         preferred_element_type=F32)
            acc = acc * jnp.concatenate(alphas, axis=1) + pv
        return (*ms, *ls, acc)

    neg = jnp.full((1, TQ), -jnp.inf, F32)
    zero = jnp.zeros((1, TQ), F32)
    res = over_chunks(attn_body, (neg,) * 4 + (zero,) * 4 + (jnp.zeros((DSA_HD, 4 * TQ), F32),))
    ls = res[4:8]
    out_t = res[8]
    o_t = jnp.concatenate([out_t[:, h * TQ:(h + 1) * TQ] * (1.0 / ls[h]) for h in range(4)], axis=0)
    return o_t.T


def _dsa(yd, qg, kg, p64, p64h):
    b, t, _ = yd.shape
    TB = TQ * DSA_SUBTILES
    nq = t // TB
    nkc = t // KC
    W = GROUP_W
    topk = min(DSA_TOPK_MAX, t // 4)
    kern = functools.partial(_dsa_kernel, seq=t, topk=topk)
    small = lambda a: pl.BlockSpec(a.shape, lambda bi, i: (0,) * a.ndim)
    tri = jnp.asarray(np.tril(np.ones((DSA_SB, DSA_SB), np.float32)), dtype=BF16)
    return pl.pallas_call(
        kern,
        grid=(b, nq),
        in_specs=[pl.BlockSpec((None, TB, W), lambda bi, i: (bi, i, 0)),
                  pl.BlockSpec((None, TB, W), lambda bi, i: (bi, i, 1)),
                  pl.BlockSpec((None, TB, 128), lambda bi, i: (bi, i, 5)),
                  pl.BlockSpec((None, TB, 128), lambda bi, i: (bi, i, 6)),
                  pl.BlockSpec((None, t, 128), lambda bi, i: (bi, 0, 4)),
                  pl.BlockSpec((None, t, 128), lambda bi, i: (bi, 0, 6)),
                  small(qg), small(kg), small(p64), small(p64h), small(tri)],
        out_specs=pl.BlockSpec((None, TB, W), lambda bi, i: (bi, i, 0)),
        out_shape=jax.ShapeDtypeStruct((b, t, W), BF16),
        scratch_shapes=[pltpu.VMEM((t, 128), BF16),
                        pltpu.VMEM((nkc, DSA_HD, KC), BF16),
                        pltpu.VMEM((DSA_PAIR, t, TQ), I32),
                        pltpu.VMEM((DSA_PAIR, t, TQ), F32)],
        compiler_params=pltpu.CompilerParams(dimension_semantics=("arbitrary", "arbitrary"),
                                             vmem_limit_bytes=VMEM_LIMIT),
        name="dsa",
    )(yd, yd, yd, yd, yd, yd, qg, kg, p64, p64h, tri)


def kernel(x, norm_g, w_in, conv_w, conv_b, gmlp_g, gmlp_ws, gmlp_b, diff_qg, diff_kg, diff_lam,
           diff_subg, dsa_qg, dsa_kg, w_out):
    b, t, d = x.shape
    depth = w_in.shape[0]
    p32 = _block_diag_mean(GROUP_W, 32)
    p64 = _block_diag_mean(GROUP_W, 64)
    p64h = _block_diag_mean(128, 64)
    w_in16 = _wprep(w_in)
    w_out16 = _wprep(w_out)
    xf = x.reshape(b * t, d)
    ys = _inproj(xf, norm_g[0].reshape(1, d), w_in16, 0)
    for l in range(depth):
        ya, yb, yc, yd = (a.reshape(b, t, a.shape[-1]) for a in ys)
        bfull = jnp.repeat(gmlp_b[l].T, GROUP_W // 4, axis=1)
        mab = _mixab(ya, yb, conv_w[l], conv_b[l].reshape(1, -1), gmlp_g[l].reshape(1, -1),
                     gmlp_ws[l], bfull, p64)
        lam_init = 0.8 - 0.6 * math.exp(-0.3 * l)
        mc = _diff(yc, jnp.tile(diff_qg[l], 8).reshape(1, -1), jnp.tile(diff_kg[l], 8).reshape(1, -1),
                   diff_lam[l], diff_subg[l].reshape(1, -1), p32, p64, lam_init)
        md = _dsa(yd, jnp.tile(dsa_qg[l], 4).reshape(1, -1), jnp.tile(dsa_kg[l], 2).reshape(1, -1),
                  p64, p64h)
        mixes = (mab.reshape(b * t, -1), mc.reshape(b * t, -1), md.reshape(b * t, -1))
        if l + 1 < depth:
            xf, *ys = _outin(xf, *mixes, w_out16, l, norm_g[l + 1].reshape(1, d), w_in16)
        else:
            xf = _outproj(xf, *mixes, w_out16, l)
    return xf.reshape(b, t, d)
```

```python
import functools
import math

import numpy as np
import jax
import jax.numpy as jnp
from jax import lax
from jax.experimental import pallas as pl
from jax.experimental.pallas import tpu as pltpu

F32 = jnp.float32
BF16 = jnp.bfloat16
I32 = jnp.int32

GROUP_W = 256
CHUNK = 64
CHUNK_SHIFT = CHUNK.bit_length() - 1
HEAD_SHIFT = 6
QD_SHIFT = 5
GMLP_BLOCK = 128
DIFF_QD = 32
DSA_HD = 64
IDX_HD = 32
IDX_HEADS = 4
DSA_TOPK_MAX = 256
EPS = 1e-6
LOG2E = math.log2(math.e)
INT_MIN = -2 ** 31
LANES = 128
SUBLANES = 8

TQ = 128
TQ_DIFF = 256
DIFF_SUBTILES = 4
DSA_SUBTILES = 4
DSA_SB = 256
DSA_PAIR = 4
MIX_SUBTILES = 8
MIX_UNROLL = 8
KC = 512
KC_DIFF = 512
TM = 512
HALO = 16
NHC = 8

_SLOPES = 2.0 ** (-8.0 * np.arange(1, 9) / 8.0)
SLOPES_C = [float(s) for s in _SLOPES[0::2]]
SLOPES_D = [float(s) for s in _SLOPES[1::2]]

VMEM_LIMIT = 56 * 1024 * 1024


def _block_diag_mean(width, seg):
    idx = np.arange(width) // seg
    return jnp.asarray((idx[:, None] == idx[None, :]).astype(np.float32) / seg, dtype=BF16)


def _seg_mean(x2, p):
    hi = x2.astype(BF16)
    lo = (x2 - hi.astype(F32)).astype(BF16)
    return (jnp.dot(hi, p, preferred_element_type=F32)
            + jnp.dot(lo, p, preferred_element_type=F32))


def _silu(z):
    return z * jax.nn.sigmoid(z)


def _gelu(x):
    return 0.5 * x * (1.0 + lax.erf(x * (2.0 ** -0.5)))


def _fold_rows(x, op, stop=SUBLANES):
    r = x.shape[0]
    while r > stop:
        r //= 2
        x = op(x[:r], x[r:])
    return x


def _fold_lanes(x, op):
    c = x.shape[1]
    while c > LANES:
        c //= 2
        x = op(x[:, :c], x[:, c:])
    return x


def _wprep_kernel(w_ref, o_ref, *, n_valid):
    col = pl.program_id(1) * LANES + lax.broadcasted_iota(I32, (1, LANES), 1)
    o_ref[...] = jnp.where(col < n_valid, w_ref[...], 0.0).astype(BF16)


def _wprep(w):
    depth, d, n = w.shape
    nt = pl.cdiv(n, LANES)
    return pl.pallas_call(
        functools.partial(_wprep_kernel, n_valid=n),
        grid=(depth, nt),
        in_specs=[pl.BlockSpec((None, d, LANES), lambda l, j: (l, 0, j))],
        out_specs=pl.BlockSpec((None, d, LANES), lambda l, j: (l, 0, j)),
        out_shape=jax.ShapeDtypeStruct((depth, d, nt * LANES), BF16),
        compiler_params=pltpu.CompilerParams(dimension_semantics=("arbitrary", "arbitrary")),
        name="wprep",
    )(w)


_A0, _B0, _C0, _D0 = 0, 4 * GROUP_W, 7 * GROUP_W, 11 * GROUP_W
_DQ, _DKV, _DZ, _DIQ, _DEND = _D0, _D0 + 256, _D0 + 384, _D0 + 640, _D0 + 896
YD_W = 896


_Y_WIDTHS = (_B0 - _A0, _C0 - _B0, _D0 - _C0, YD_W)


def _norm_project(x, g_ref, w_ref, ya_ref, yb_ref, yc_ref, yd_ref):
    ms = jnp.mean(x * x, axis=-1, keepdims=True)
    xn = (x * lax.rsqrt(ms + EPS) * g_ref[...]).astype(BF16)

    def proj(lo, hi):
        return jnp.dot(xn, w_ref[:, lo:hi], preferred_element_type=F32).astype(BF16)

    ya_ref[...] = proj(_A0, _B0)
    yb_ref[...] = proj(_B0, _C0)
    yc_ref[...] = proj(_C0, _D0)
    yd_ref[:, 0:256] = proj(_DQ, _DKV)
    yd_ref[:, 256:512] = proj(_DZ, _DIQ)
    yd_ref[:, 512:640] = proj(_DKV, _DZ)
    yd_ref[:, 640:896] = proj(_DIQ, _DEND)


def _mix_project(x_ref, mab_ref, mc_ref, md_ref, wo_ref):
    acc = x_ref[...]
    acc = acc + jnp.dot(mab_ref[...], wo_ref[0:2 * GROUP_W, :], preferred_element_type=F32)
    acc = acc + jnp.dot(mc_ref[...], wo_ref[2 * GROUP_W:3 * GROUP_W, :], preferred_element_type=F32)
    return acc + jnp.dot(md_ref[...], wo_ref[3 * GROUP_W:4 * GROUP_W, :], preferred_element_type=F32)


def _inproj_kernel(x_ref, g_ref, w_ref, ya_ref, yb_ref, yc_ref, yd_ref):
    _norm_project(x_ref[...], g_ref, w_ref, ya_ref, yb_ref, yc_ref, yd_ref)


def _outproj_kernel(x_ref, mab_ref, mc_ref, md_ref, wo_ref, o_ref):
    o_ref[...] = _mix_project(x_ref, mab_ref, mc_ref, md_ref, wo_ref)


def _outin_kernel(x_ref, mab_ref, mc_ref, md_ref, wo_ref, g_ref, w_ref,
                  o_ref, ya_ref, yb_ref, yc_ref, yd_ref):
    x_new = _mix_project(x_ref, mab_ref, mc_ref, md_ref, wo_ref)
    o_ref[...] = x_new
    _norm_project(x_new, g_ref, w_ref, ya_ref, yb_ref, yc_ref, yd_ref)


def _row_spec(width):
    return pl.BlockSpec((TM, width), lambda i: (i, 0))


def _layer_spec(w, layer):
    return pl.BlockSpec((None,) + w.shape[1:], lambda i: (layer, 0, 0))


_PROJ_PARAMS = pltpu.CompilerParams(dimension_semantics=("arbitrary",), vmem_limit_bytes=VMEM_LIMIT)


def _inproj(xf, g, wb16, layer):
    m, d = xf.shape
    return pl.pallas_call(
        _inproj_kernel,
        grid=(m // TM,),
        in_specs=[_row_spec(d), pl.BlockSpec((1, d), lambda i: (0, 0)), _layer_spec(wb16, layer)],
        out_specs=[_row_spec(w) for w in _Y_WIDTHS],
        out_shape=[jax.ShapeDtypeStruct((m, w), BF16) for w in _Y_WIDTHS],
        compiler_params=_PROJ_PARAMS,
        name="inproj",
    )(xf, g, wb16)


def _outproj(xf, mab, mc, md, wo, layer):
    m, d = xf.shape
    return pl.pallas_call(
        _outproj_kernel,
        grid=(m // TM,),
        in_specs=[_row_spec(d), _row_spec(2 * GROUP_W), _row_spec(GROUP_W), _row_spec(GROUP_W),
                  _layer_spec(wo, layer)],
        out_specs=_row_spec(d),
        out_shape=jax.ShapeDtypeStruct((m, d), F32),
        compiler_params=_PROJ_PARAMS,
        name="outproj",
    )(xf, mab, mc, md, wo)


def _outin(xf, mab, mc, md, wo, layer, g_next, wb16):
    m, d = xf.shape
    return pl.pallas_call(
        _outin_kernel,
        grid=(m // TM,),
        in_specs=[_row_spec(d), _row_spec(2 * GROUP_W), _row_spec(GROUP_W), _row_spec(GROUP_W),
                  _layer_spec(wo, layer), pl.BlockSpec((1, d), lambda i: (0, 0)),
                  _layer_spec(wb16, layer + 1)],
        out_specs=[_row_spec(d)] + [_row_spec(w) for w in _Y_WIDTHS],
        out_shape=[jax.ShapeDtypeStruct((m, d), F32)]
                  + [jax.ShapeDtypeStruct((m, w), BF16) for w in _Y_WIDTHS],
        compiler_params=_PROJ_PARAMS,
        name="outin",
    )(xf, mab, mc, md, wo, g_next, wb16)


def _mixab_kernel(ya_ref, halo_ref, yb_ref, cw_ref, cb_ref, gg_ref, ws_ref, bfull_ref, p64_ref,
                  o_ref, u_scr):
    i = pl.program_id(1)
    W = GROUP_W
    TB = TQ * MIX_SUBTILES
    uh = halo_ref[:, 2 * W:3 * W].astype(F32) * halo_ref[:, 0:W].astype(F32)
    u_scr[0:HALO, :] = jnp.where(i > 0, uh, 0.0)
    u_scr[HALO:HALO + TB, :] = ya_ref[:, 2 * W:3 * W].astype(F32) * ya_ref[:, 0:W].astype(F32)
    for sb in range(MIX_SUBTILES):
        r0 = HALO + sb * TQ
        y = u_scr[r0 - 2:r0 - 2 + TQ, :] * cw_ref[0:1, :]
        y = y + u_scr[r0 - 1:r0 - 1 + TQ, :] * cw_ref[1:2, :]
        y = y + u_scr[r0:r0 + TQ, :] * cw_ref[2:3, :]
        rows = slice(sb * TQ, (sb + 1) * TQ)
        bg = ya_ref[rows, W:2 * W].astype(F32)
        za = ya_ref[rows, 3 * W:4 * W].astype(F32)
        o_ref[rows, 0:W] = (bg * (y + cb_ref[...]) * _silu(za)).astype(o_ref.dtype)

    t_idx = lax.broadcasted_iota(I32, (GMLP_BLOCK, GMLP_BLOCK), 0)
    s_idx = lax.broadcasted_iota(I32, (GMLP_BLOCK, GMLP_BLOCK), 1)
    causal = (t_idx >> CHUNK_SHIFT) >= (s_idx >> CHUNK_SHIFT)
    lane = lax.broadcasted_iota(I32, (1, W), 1)
    wms = [jnp.where(causal, ws_ref[hd], 0.0).astype(BF16) for hd in range(4)]

    def block_group(sp, carry):
        for k in range(MIX_UNROLL):
            block(pl.ds(pl.multiple_of((MIX_UNROLL * sp + k) * TQ, TQ), TQ))
        return carry

    def block(rows):
        u = _gelu(yb_ref[rows, 0:W].astype(F32))
        v = _gelu(yb_ref[rows, W:2 * W].astype(F32))
        zb = yb_ref[rows, 2 * W:3 * W].astype(F32)
        vn = v * lax.rsqrt(_seg_mean(v * v, p64_ref[...]) + EPS) * gg_ref[...]
        s = bfull_ref[...]
        for hd in range(4):
            vh = jnp.where((lane >> HEAD_SHIFT) == hd, vn, 0.0).astype(BF16)
            s = s + jnp.dot(wms[hd], vh, preferred_element_type=F32)
        o_ref[rows, W:2 * W] = (u * s * _silu(zb)).astype(o_ref.dtype)

    lax.fori_loop(0, MIX_SUBTILES // MIX_UNROLL, block_group, 0)


def _mixab(ya, yb, cw, cb, gg, ws, bfull, p64):
    b, t, _ = ya.shape
    TB = TQ * MIX_SUBTILES
    nq = t // TB
    return pl.pallas_call(
        _mixab_kernel,
        grid=(b, nq),
        in_specs=[pl.BlockSpec((None, TB, 4 * GROUP_W), lambda bi, i: (bi, i, 0)),
                  pl.BlockSpec((None, HALO, 4 * GROUP_W),
                               lambda bi, i: (bi, jnp.maximum(i * (TB // HALO) - 1, 0), 0)),
                  pl.BlockSpec((None, TB, 3 * GROUP_W), lambda bi, i: (bi, i, 0)),
                  pl.BlockSpec(cw.shape, lambda bi, i: (0, 0)),
                  pl.BlockSpec(cb.shape, lambda bi, i: (0, 0)),
                  pl.BlockSpec(gg.shape, lambda bi, i: (0, 0)),
                  pl.BlockSpec(ws.shape, lambda bi, i: (0, 0, 0)),
                  pl.BlockSpec(bfull.shape, lambda bi, i: (0, 0)),
                  pl.BlockSpec(p64.shape, lambda bi, i: (0, 0))],
        out_specs=pl.BlockSpec((None, TB, 2 * GROUP_W), lambda bi, i: (bi, i, 0)),
        out_shape=jax.ShapeDtypeStruct((b, t, 2 * GROUP_W), BF16),
        scratch_shapes=[pltpu.VMEM((HALO + TB, GROUP_W), F32)],
        compiler_params=pltpu.CompilerParams(dimension_semantics=("arbitrary", "arbitrary"),
                                             vmem_limit_bytes=VMEM_LIMIT),
        name="mixab",
    )(ya, ya, yb, cw, cb, gg, ws, bfull, p64)


def _diff_kernel(q_ref, k_ref, v_ref, z_ref, qg_ref, kg_ref, lam_ref, subg_ref, p32_ref, p64_ref,
                 o_ref, kn_scr, qs_scr, s_scr, m_scr, l_scr, p_scr, acc_scr, *, lam_init, seq):
    TQ = TQ_DIFF
    i = pl.program_id(1)

    @pl.when(i == 0)
    def _():
        for c in range(seq // KC):
            kk = k_ref[c * KC:(c + 1) * KC, :].astype(F32)
            ms = _seg_mean(kk * kk, p32_ref[...])
            kn_scr[c * KC:(c + 1) * KC, :] = (kk * lax.rsqrt(ms + EPS) * kg_ref[...]).astype(BF16)

    lp = lam_ref[...]
    lam = (jnp.exp(jnp.sum(lp[0:1] * lp[1:2], axis=-1, keepdims=True))
           - jnp.exp(jnp.sum(lp[2:3] * lp[3:4], axis=-1, keepdims=True)) + lam_init)

    def sub_tile(sub, carry):
        rows = pl.ds(pl.multiple_of(sub * TQ, TQ), TQ)
        q0 = (i * DIFF_SUBTILES + sub) * TQ
        o = _diff_tile(q0, q_ref[rows, :].astype(F32), lam, v_ref, qg_ref, subg_ref, p32_ref, p64_ref,
                       kn_scr, qs_scr, s_scr, m_scr, l_scr, p_scr, acc_scr, lam_init)
        z = z_ref[rows, :].astype(F32)
        o_ref[rows, :] = (o * _silu(z)).astype(o_ref.dtype)
        return carry

    lax.fori_loop(0, DIFF_SUBTILES, sub_tile, 0)


def _diff_tile(q0, q, lam, v_ref, qg_ref, subg_ref, p32_ref, p64_ref,
               kn_scr, qs_scr, s_scr, m_scr, l_scr, p_scr, acc_scr, lam_init):
    TQ = TQ_DIFF
    KC = KC_DIFF
    nchunk = (q0 + TQ - 1) // KC + 1
    p32 = p32_ref[...]
    qn = q * lax.rsqrt(_seg_mean(q * q, p32) + EPS) * (qg_ref[...] * (DIFF_QD ** -0.5 * LOG2E))
    lane = lax.broadcasted_iota(I32, (1, GROUP_W), 1)
    for hc in range(NHC):
        qs_scr[hc * TQ:(hc + 1) * TQ, :] = jnp.where((lane >> QD_SHIFT) == hc, qn, 0.0).astype(BF16)

    m_scr[...] = jnp.full(m_scr.shape, -jnp.inf, F32)
    l_scr[...] = jnp.zeros(l_scr.shape, F32)
    acc_scr[...] = jnp.zeros(acc_scr.shape, F32)
    row = q0 + lax.broadcasted_iota(I32, (TQ, KC), 0)

    def s_body(c, carry):
        col = c * KC + lax.broadcasted_iota(I32, (TQ, KC), 1)
        dist = jnp.abs(row - col).astype(F32)
        dm = jnp.where((col >> CHUNK_SHIFT) <= (row >> CHUNK_SHIFT), dist, jnp.inf)
        kc = kn_scr[pl.ds(pl.multiple_of(c * KC, KC), KC), :]
        s_all = lax.dot_general(qs_scr[...], kc, (((1,), (1,)), ((), ())), preferred_element_type=F32)
        for h in range(NHC // 2):
            bias = (SLOPES_C[h] * LOG2E) * dm
            for j in range(2):
                rows = slice((2 * h + j) * TQ, (2 * h + j + 1) * TQ)
                s = s_all[rows, :] - bias
                s_scr[c, rows, :] = s
                m_scr[rows, :] = jnp.maximum(m_scr[rows, :], _fold_lanes(s, jnp.maximum))
        return carry

    lax.fori_loop(0, nchunk, s_body, 0)

    for hc in range(NHC):
        m = jnp.max(m_scr[hc * TQ:(hc + 1) * TQ, :], axis=1, keepdims=True)
        m_scr[hc * TQ:(hc + 1) * TQ, :] = jnp.broadcast_to(m, (TQ, LANES))

    def e_body(c, carry):
        for hc in range(NHC):
            rows = slice(hc * TQ, (hc + 1) * TQ)
            m = m_scr[rows, :]
            s = s_scr[c, rows, :]
            ps = [jnp.exp2(s[:, k * LANES:(k + 1) * LANES] - m) for k in range(KC // LANES)]
            l_scr[rows, :] += _fold_lanes(jnp.concatenate(ps, axis=1), jnp.add)
            p_scr[rows, :] = jnp.concatenate(ps, axis=1).astype(BF16)
        vc = v_ref[pl.ds(pl.multiple_of(c * KC, KC), KC), :]
        acc_scr[...] += jnp.dot(p_scr[...], vc, preferred_element_type=F32)
        return carry

    lax.fori_loop(0, nchunk, e_body, 0)

    o = jnp.zeros((TQ, GROUP_W), F32)
    for h in range(NHC // 2):
        r1 = slice(2 * h * TQ, (2 * h + 1) * TQ)
        r2 = slice((2 * h + 1) * TQ, (2 * h + 2) * TQ)
        l1 = jnp.sum(l_scr[r1, :], axis=1, keepdims=True)
        l2 = jnp.sum(l_scr[r2, :], axis=1, keepdims=True)
        o_h = acc_scr[r1, :] * (1.0 / l1) - acc_scr[r2, :] * (lam / l2)
        o = jnp.where((lane >> HEAD_SHIFT) == h, o_h, o)

    ms = _seg_mean(o * o, p64_ref[...])
    return o * lax.rsqrt(ms + EPS) * (subg_ref[...] * (1.0 - lam_init))


def _diff(yc, qg, kg, lam_p, subg, p32, p64, lam_init):
    TQ = TQ_DIFF
    KC = KC_DIFF
    TB = TQ * DIFF_SUBTILES
    b, t, _ = yc.shape
    nq = t // TB
    nkc = t // KC
    W = GROUP_W
    kern = functools.partial(_diff_kernel, lam_init=lam_init, seq=t)
    small = lambda a: pl.BlockSpec(a.shape, lambda bi, i: (0,) * a.ndim)
    return pl.pallas_call(
        kern,
        grid=(b, nq),
        in_specs=[pl.BlockSpec((None, TB, W), lambda bi, i: (bi, i, 0)),
                  pl.BlockSpec((None, t, W), lambda bi, i: (bi, 0, 1)),
                  pl.BlockSpec((None, t, W), lambda bi, i: (bi, 0, 2)),
                  pl.BlockSpec((None, TB, W), lambda bi, i: (bi, i, 3)),
                  small(qg), small(kg), small(lam_p), small(subg), small(p32), small(p64)],
        out_specs=pl.BlockSpec((None, TB, W), lambda bi, i: (bi, i, 0)),
        out_shape=jax.ShapeDtypeStruct((b, t, W), BF16),
        scratch_shapes=[pltpu.VMEM((t, W), BF16),
                        pltpu.VMEM((NHC * TQ, W), BF16),
                        pltpu.VMEM((nkc, NHC * TQ, KC), F32),
                        pltpu.VMEM((NHC * TQ, LANES), F32),
                        pltpu.VMEM((NHC * TQ, LANES), F32),
                        pltpu.VMEM((NHC * TQ, KC), BF16),
                        pltpu.VMEM((NHC * TQ, W), F32)],
        compiler_params=pltpu.CompilerParams(dimension_semantics=("arbitrary", "arbitrary"),
                                             vmem_limit_bytes=VMEM_LIMIT),
        name="diffattn",
    )(yc, yc, yc, yc, qg, kg, lam_p, subg, p32, p64)


def _dsa_kernel(q_ref, z_ref, iq_ref, ikwq_ref, kv_ref, ikw_ref, qg_ref, kg_ref, p64_ref, p64h_ref,
                tri_ref, o_ref, knv_scr, vt_scr, key_scr, dm_scr, *, seq, topk):
    i = pl.program_id(1)

    @pl.when(i == 0)
    def _():
        p64h = p64h_ref[...]
        for c in range(seq // 128):
            blk = kv_ref[c * 128:(c + 1) * 128, :].astype(F32)
            ms = _seg_mean(blk * blk, p64h)
            knv_scr[c * 128:(c + 1) * 128, :] = (blk * lax.rsqrt(ms + EPS) * kg_ref[...]).astype(BF16)
            vt = blk.T
            cc, off = divmod(c * 128, KC)
            vt_scr[cc, :, off:off + 128] = vt[DSA_HD:2 * DSA_HD, :].astype(BF16)

    def tile_pair(nchunk, pair, carry):
        tiles = []
        for s in range(DSA_PAIR):
            sub = pair * DSA_PAIR + s
            rows = pl.ds(sub * TQ, TQ)
            q0 = ((nchunk - 1) * DSA_SUBTILES + sub) * TQ

            iq_t = iq_ref[rows, :].astype(F32).T
            iw_t = ikwq_ref[rows, :].astype(F32).T[IDX_HD:IDX_HD + 8, :]
            wq = iw_t * (IDX_HEADS ** -0.5 * IDX_HD ** -0.5)
            zpad_i = jnp.zeros((128 - IDX_HD, TQ), F32)
            rhs_idx = jnp.concatenate(
                [jnp.concatenate([iq_t[IDX_HD * h:IDX_HD * (h + 1), :], zpad_i], axis=0)
                 for h in range(IDX_HEADS)], axis=1).astype(BF16)
            _dsa_index(nchunk, q0, rhs_idx, wq, ikw_ref, key_scr.at[s])
            tiles.append((rows, q0))

        taus = _dsa_search(nchunk, [key_scr.at[s] for s in range(DSA_PAIR)], [q0 for _, q0 in tiles], topk)

        for s, (rows, q0) in enumerate(tiles):
            q = q_ref[rows, :].astype(F32)
            qn = q * lax.rsqrt(_seg_mean(q * q, p64_ref[...]) + EPS) * (qg_ref[...] * (DSA_HD ** -0.5 * LOG2E))
            qn_t = qn.T
            zpad_q = jnp.zeros((128 - DSA_HD, TQ), F32)
            rhs_main = jnp.concatenate(
                [jnp.concatenate([qn_t[DSA_HD * h:DSA_HD * (h + 1), :], zpad_q], axis=0) for h in range(4)],
                axis=1).astype(BF16)
            o = _dsa_attend(nchunk, q0, taus[s], rhs_main, tri_ref, knv_scr, vt_scr, key_scr.at[s],
                            dm_scr.at[s], topk)
            z = z_ref[rows, :].astype(F32)
            o_ref[rows, :] = (o * _silu(z)).astype(o_ref.dtype)
        return carry

    assert TQ * DSA_SUBTILES == KC and DSA_SUBTILES % DSA_PAIR == 0
    for n in range(1, seq // KC + 1):
        @pl.when(i == n - 1)
        def _(n=n):
            for pair in range(DSA_SUBTILES // DSA_PAIR):
                tile_pair(n, pair, 0)


def _dsa_index(nchunk, q0, rhs_idx, wq, ikw_ref, key_scr):
    SB = DSA_SB
    n_sb = KC // SB
    qpos = q0 + lax.broadcasted_iota(I32, (SB, TQ), 1)
    for c in range(nchunk):
        for sb in range(n_sb):
            r0 = c * KC + sb * SB
            logit = jnp.dot(ikw_ref[pl.ds(r0, SB), :], rhs_idx, preferred_element_type=F32)
            sc = ((jnp.maximum(logit[:, 0:TQ], 0.0) * wq[0:1, :]
                   + jnp.maximum(logit[:, TQ:2 * TQ], 0.0) * wq[1:2, :])
                  + (jnp.maximum(logit[:, 2 * TQ:3 * TQ], 0.0) * wq[2:3, :]
                     + jnp.maximum(logit[:, 3 * TQ:4 * TQ], 0.0) * wq[3:4, :]))
            bits = lax.bitcast_convert_type(sc, I32)
            key = bits ^ ((bits >> 31) & jnp.int32(0x7FFFFFFF))
            key = jnp.where(key == -1, 0, key)
            if c == nchunk - 1:
                krow = r0 + lax.broadcasted_iota(I32, (SB, TQ), 0)
                key = jnp.where((krow >> CHUNK_SHIFT) <= (qpos >> CHUNK_SHIFT), key, jnp.int32(INT_MIN))
            key_scr[pl.ds(r0, SB), :] = key


def _dsa_search(nchunk, key_scrs, q0s, topk):
    kf = jnp.float32(topk)
    lowest = jnp.full((1, TQ), INT_MIN, I32)
    searched = [t for t, q0 in enumerate(q0s) if q0 + TQ > topk]
    if len(searched) < len(q0s):
        found = iter(_dsa_search(nchunk, [key_scrs[t] for t in searched], [q0s[t] for t in searched], topk)
                     if searched else ())
        return tuple(next(found) if t in searched else lowest for t in range(len(q0s)))

    def count_ge(key_scr, cand):
        acc = jnp.zeros((SUBLANES, TQ), F32)
        for c in range(nchunk):
            acc = acc + _fold_rows(jnp.where(key_scr[c * KC:(c + 1) * KC, :] >= cand, 1.0, 0.0), jnp.add)
        return jnp.sum(acc, axis=0, keepdims=True)

    def tau_step(it, us):
        out = []
        for key_scr, u in zip(key_scrs, us):
            cand_u = u | jnp.left_shift(jnp.int32(1), 31 - it)
            cand = cand_u ^ jnp.int32(INT_MIN)
            out.append(jnp.where(count_ge(key_scr, cand) >= kf, cand_u, u))
        return tuple(out)

    zero = jnp.zeros((1, TQ), I32)
    us = lax.fori_loop(0, 32, tau_step, (zero,) * len(key_scrs))
    return tuple(u ^ jnp.int32(INT_MIN) for u in us)


def _dsa_attend(nchunk, q0, tau, rhs_main, tri_ref, knv_scr, vt_scr, key_scr, dm_scr, topk):
    SB = DSA_SB
    n_sb = KC // SB
    kf = jnp.float32(topk)
    qpos = q0 + lax.broadcasted_iota(I32, (SB, TQ), 1)

    def over_chunks(body, init):
        acc = init
        for c in range(nchunk):
            acc = body(c, acc)
        return acc

    def gt_body(c, acc):
        key = key_scr[c * KC:(c + 1) * KC, :]
        return acc + _fold_rows(jnp.where(key > tau, 1.0, 0.0), jnp.add)

    n_gt = over_chunks(gt_body, jnp.zeros((SUBLANES, TQ), F32))
    need = kf - jnp.sum(n_gt, axis=0, keepdims=True)
    need = jnp.where(tau == jnp.int32(INT_MIN), 0.0, need)

    def dm_body(c, ties_before):
        for sb in range(n_sb):
            r0 = c * KC + sb * SB
            krow = r0 + lax.broadcasted_iota(I32, (SB, TQ), 0)
            dist = jnp.abs(qpos - krow).astype(F32)
            key = key_scr[pl.ds(r0, SB), :]
            is_tie = key == tau
            tie01 = jnp.where(is_tie, 1.0, 0.0)
            rank = ties_before + jnp.dot(tri_ref[...], tie01.astype(BF16), preferred_element_type=F32)
            inner = jnp.where(is_tie, jnp.where(rank <= need, dist, jnp.inf), jnp.inf)
            dm_scr[pl.ds(r0, SB), :] = jnp.where(key > tau, dist, inner)
            ties_before = ties_before + jnp.sum(_fold_rows(tie01, jnp.add), axis=0, keepdims=True)
        return ties_before

    over_chunks(dm_body, jnp.zeros((1, TQ), F32))

    slopes = [s * LOG2E for s in SLOPES_D]

    def attn_body(c, carry):
        ms, ls, acc = list(carry[:4]), list(carry[4:8]), carry[8]
        atts = [jnp.dot(knv_scr[pl.ds(c * KC + sb * SB, SB), :], rhs_main,
                        preferred_element_type=F32) for sb in range(n_sb)]
        for sb in range(n_sb):
            r0 = c * KC + sb * SB
            att = atts[sb]
            dm = dm_scr[pl.ds(r0, SB), :]
            alphas, probs = [], []
            for h in range(4):
                a = att[:, h * TQ:(h + 1) * TQ] - slopes[h] * dm
                m_new = jnp.maximum(ms[h], jnp.max(_fold_rows(a, jnp.maximum), axis=0, keepdims=True))
                m_use = jnp.where(m_new == -jnp.inf, 0.0, m_new)
                alpha = jnp.exp2(ms[h] - m_use)
                p = jnp.exp2(a - m_use)
                ls[h] = ls[h] * alpha + jnp.sum(_fold_rows(p, jnp.add), axis=0, keepdims=True)
                ms[h] = m_new
                alphas.append(alpha)
                probs.append(p.astype(BF16))
            pv = jnp.dot(vt_scr[c, :, sb * SB:(sb + 1) * SB], jnp.concatenate(probs, axis=1),
                         preferred_element_type=F32)
            acc = acc * jnp.concatenate(alphas, axis=1) + pv
        return (*ms, *ls, acc)

    neg = jnp.full((1, TQ), -jnp.inf, F32)
    zero = jnp.zeros((1, TQ), F32)
    res = over_chunks(attn_body, (neg,) * 4 + (zero,) * 4 + (jnp.zeros((DSA_HD, 4 * TQ), F32),))
    ls = res[4:8]
    out_t = res[8]
    o_t = jnp.concatenate([out_t[:, h * TQ:(h + 1) * TQ] * (1.0 / ls[h]) for h in range(4)], axis=0)
    return o_t.T


def _dsa(yd, qg, kg, p64, p64h):
    b, t, _ = yd.shape
    TB = TQ * DSA_SUBTILES
    nq = t // TB
    nkc = t // KC
    W = GROUP_W
    topk = min(DSA_TOPK_MAX, t // 4)
    kern = functools.partial(_dsa_kernel, seq=t, topk=topk)
    small = lambda a: pl.BlockSpec(a.shape, lambda bi, i: (0,) * a.ndim)
    tri = jnp.asarray(np.tril(np.ones((DSA_SB, DSA_SB), np.float32)), dtype=BF16)
    return pl.pallas_call(
        kern,
        grid=(b, nq),
        in_specs=[pl.BlockSpec((None, TB, W), lambda bi, i: (bi, i, 0)),
                  pl.BlockSpec((None, TB, W), lambda bi, i: (bi, i, 1)),
                  pl.BlockSpec((None, TB, 128), lambda bi, i: (bi, i, 5)),
                  pl.BlockSpec((None, TB, 128), lambda bi, i: (bi, i, 6)),
                  pl.BlockSpec((None, t, 128), lambda bi, i: (bi, 0, 4)),
                  pl.BlockSpec((None, t, 128), lambda bi, i: (bi, 0, 6)),
                  small(qg), small(kg), small(p64), small(p64h), small(tri)],
        out_specs=pl.BlockSpec((None, TB, W), lambda bi, i: (bi, i, 0)),
        out_shape=jax.ShapeDtypeStruct((b, t, W), BF16),
        scratch_shapes=[pltpu.VMEM((t, 128), BF16),
                        pltpu.VMEM((nkc, DSA_HD, KC), BF16),
                        pltpu.VMEM((DSA_PAIR, t, TQ), I32),
                        pltpu.VMEM((DSA_PAIR, t, TQ), F32)],
        compiler_params=pltpu.CompilerParams(dimension_semantics=("arbitrary", "arbitrary"),
                                             vmem_limit_bytes=VMEM_LIMIT),
        name="dsa",
    )(yd, yd, yd, yd, yd, yd, qg, kg, p64, p64h, tri)


def kernel(x, norm_g, w_in, conv_w, conv_b, gmlp_g, gmlp_ws, gmlp_b, diff_qg, diff_kg, diff_lam,
           diff_subg, dsa_qg, dsa_kg, w_out):
    b, t, d = x.shape
    depth = w_in.shape[0]
    p32 = _block_diag_mean(GROUP_W, 32)
    p64 = _block_diag_mean(GROUP_W, 64)
    p64h = _block_diag_mean(128, 64)
    w_in16 = _wprep(w_in)
    w_out16 = _wprep(w_out)
    xf = x.reshape(b * t, d)
    ys = _inproj(xf, norm_g[0].reshape(1, d), w_in16, 0)
    for l in range(depth):
        ya, yb, yc, yd = (a.reshape(b, t, a.shape[-1]) for a in ys)
        bfull = jnp.repeat(gmlp_b[l].T, GROUP_W // 4, axis=1)
        mab = _mixab(ya, yb, conv_w[l], conv_b[l].reshape(1, -1), gmlp_g[l].reshape(1, -1),
                     gmlp_ws[l], bfull, p64)
        lam_init = 0.8 - 0.6 * math.exp(-0.3 * l)
        mc = _diff(yc, jnp.tile(diff_qg[l], 8).reshape(1, -1), jnp.tile(diff_kg[l], 8).reshape(1, -1),
                   diff_lam[l], diff_subg[l].reshape(1, -1), p32, p64, lam_init)
        md = _dsa(yd, jnp.tile(dsa_qg[l], 4).reshape(1, -1), jnp.tile(dsa_kg[l], 2).reshape(1, -1),
                  p64, p64h)
        mixes = (mab.reshape(b * t, -1), mc.reshape(b * t, -1), md.reshape(b * t, -1))
        if l + 1 < depth:
            xf, *ys = _outin(xf, *mixes, w_out16, l, norm_g[l + 1].reshape(1, d), w_in16)
        else:
            xf = _outproj(xf, *mixes, w_out16, l)
    return xf.reshape(b, t, d)
```
